```python
import math
import jax, jax.numpy as jnp
from jax import lax
import numpy as np

D_MODEL = 1024
BATCH = 8
SEQ = 2048
DEPTH = 4
DEC_BATCH = 128
DEC_SEQ = 8
PAST_LEN = 8192
PAGE_SIZE = 128

ATTN_HEADS = 8
KV_HEADS = 2
HEAD_DIM = 64
Q_PER_KV = ATTN_HEADS // KV_HEADS
ATTN_WIDTH = ATTN_HEADS * HEAD_DIM
KV_WIDTH = KV_HEADS * HEAD_DIM
WINDOW = 128
ROPE_THETA = 10000.0
HG_HEADS = 4
HG_DK = 128
HG_DV = 128
HG_KW = HG_HEADS * HG_DK
HG_VW = HG_HEADS * HG_DV
HG_CHUNK = 64
MIX_WIDTH = ATTN_WIDTH + HG_VW
IN_COLS = 2 * ATTN_WIDTH + 2 * KV_WIDTH + 2 * HG_KW + 2 * HG_VW
PLE_DIM = 256
DN_ALPHA = (2 * DEPTH) ** 0.25
DN_BETA = (8 * DEPTH) ** -0.25
NORM_EPS = 1e-5
NEG_INF = -1e30

kernel_name = 'hymba_swa_sink_hgrn2_deepnorm_decode_step'


def _split_points():
    sizes = [ATTN_WIDTH, KV_WIDTH, KV_WIDTH, ATTN_WIDTH, HG_KW, HG_KW, HG_VW, HG_VW]
    return [int(s) for s in np.cumsum(sizes)[:-1]]


def _rmsnorm(x, g):
    xf = x.astype(jnp.float32)
    return xf * lax.rsqrt(jnp.mean(xf * xf, axis=-1, keepdims=True) + NORM_EPS) * g.astype(jnp.float32)


def _layernorm(x, g, b):
    xf = x.astype(jnp.float32)
    mu = jnp.mean(xf, axis=-1, keepdims=True)
    var = jnp.mean(jnp.square(xf - mu), axis=-1, keepdims=True)
    return (xf - mu) * lax.rsqrt(var + NORM_EPS) * g.astype(jnp.float32) + b.astype(jnp.float32)


def _rope(x, pos):
    dh = x.shape[-1]
    half = dh // 2
    inv = jnp.exp(-math.log(ROPE_THETA) * jnp.arange(half, dtype=jnp.float32) * 2.0 / dh)
    ang = pos.astype(jnp.float32)[:, None] * inv[None, :]
    cos = jnp.cos(ang)[:, None, :]
    sin = jnp.sin(ang)[:, None, :]
    xf = x.astype(jnp.float32)
    x1, x2 = xf[..., :half], xf[..., half:]
    return jnp.concatenate([x1 * cos - x2 * sin, x2 * cos + x1 * sin], axis=-1)


def _sink_window_attend(q, k, v, qpos, kpos, sinks):
    scale = HEAD_DIM ** -0.5
    s = jnp.einsum('bnqhgd,bnkhd->bnhgqk', q.astype(jnp.float32), k.astype(jnp.float32)) * scale
    diff = qpos[:, :, None] - kpos[:, None, :]
    valid = (diff >= 0) & (diff < WINDOW) & (kpos[:, None, :] >= 0)
    s = jnp.where(valid[None, :, None, None], s, NEG_INF)
    sink = sinks.astype(jnp.float32)[None, None, :, :, None, None]
    m = jnp.maximum(jnp.max(s, axis=-1, keepdims=True), sink)
    pexp = jnp.exp(s - m)
    denom = jnp.sum(pexp, axis=-1, keepdims=True) + jnp.exp(sink - m)
    return jnp.einsum('bnhgqk,bnkhd->bnqhgd', pexp / denom, v.astype(jnp.float32))


def _hgrn2(q, f_logit, i_in, lb, s0):
    B, T, _ = q.shape
    C = math.gcd(HG_CHUNK, T)
    nc = T // C
    lbf = lb.astype(jnp.float32)
    log_f = jnp.logaddexp(jnp.log(lbf), jnp.log1p(-lbf) + jax.nn.log_sigmoid(f_logit.astype(jnp.float32)))
    k = -jnp.expm1(log_f)

    def heads(a, d):
        return a.astype(jnp.float32).reshape(B, nc, C, HG_HEADS, d).transpose(1, 0, 2, 3, 4)

    qs, gs, ks, vs = heads(q, HG_DK), heads(log_f, HG_DK), heads(k, HG_DK), heads(i_in, HG_DV)
    causal = jnp.tril(jnp.ones((C, C), dtype=bool))[None, :, :, None, None]

    def step(S, blk):
        qc, gc, kc, vc = blk
        G = jnp.cumsum(gc, axis=1)
        diff = G[:, :, None] - G[:, None, :]
        decay = jnp.exp(jnp.where(causal, diff, NEG_INF))
        A = jnp.einsum('bthd,bshd,btshd->bhts', qc, kc, decay)
        o = jnp.einsum('bhts,bshv->bthv', A, vc) + jnp.einsum('bthd,bhdv->bthv', qc * jnp.exp(G), S)
        G_last = G[:, -1]
        S_new = jnp.exp(G_last)[..., None] * S + jnp.einsum(
            'bshd,bshv->bhdv', kc * jnp.exp(G_last[:, None] - G), vc)
        return S_new, o

    S_fin, o = lax.scan(step, s0.astype(jnp.float32), (qs, gs, ks, vs))
    o = o.transpose(1, 0, 2, 3, 4).reshape(B, T, HG_HEADS, HG_DV)
    return o, S_fin


def _mixer_layer(x, p, pos, k_buf, v_buf, s0, w_in, sinks, attn_g, lb, hg_g, w_out,
                 ln_g, ln_b, w_pp, w_pg):
    B, T, _ = x.shape
    z = jnp.einsum('btd,dc->btc', x, w_in)
    q, k, v, g_attn, hq, hf, hi, g_hg = jnp.split(z, _split_points(), axis=-1)
    q = _rope(q.reshape(B, T, ATTN_HEADS, HEAD_DIM), pos).reshape(B, T, KV_HEADS, Q_PER_KV, HEAD_DIM)
    k = _rope(k.reshape(B, T, KV_HEADS, HEAD_DIM), pos)
    v = v.reshape(B, T, KV_HEADS, HEAD_DIM).astype(jnp.float32)
    sk = sinks.reshape(KV_HEADS, Q_PER_KV)
    cache_w = min(WINDOW, PAST_LEN)
    if k_buf is None:
        nb = T // WINDOW
        qb = q.reshape(B, nb, WINDOW, KV_HEADS, Q_PER_KV, HEAD_DIM)
        kb = k.reshape(B, nb, WINDOW, KV_HEADS, HEAD_DIM)
        vb = v.reshape(B, nb, WINDOW, KV_HEADS, HEAD_DIM)
        k2 = jnp.concatenate([jnp.concatenate([jnp.zeros_like(kb[:, :1]), kb[:, :-1]], axis=1), kb], axis=2)
        v2 = jnp.concatenate([jnp.concatenate([jnp.zeros_like(vb[:, :1]), vb[:, :-1]], axis=1), vb], axis=2)
        qpos = pos.reshape(nb, WINDOW)
        kpos = jnp.concatenate([qpos - WINDOW, qpos], axis=1)
        attn = _sink_window_attend(qb, k2, v2, qpos, kpos, sk).reshape(B, T, ATTN_WIDTH)
        k_keep = k[:, T - cache_w:]
        v_keep = v[:, T - cache_w:]
    else:
        w = k_buf.shape[1]
        k_all = jnp.concatenate([k_buf.astype(jnp.float32), k], axis=1)
        v_all = jnp.concatenate([v_buf.astype(jnp.float32), v], axis=1)
        kpos = jnp.concatenate([PAST_LEN - w + jnp.arange(w, dtype=jnp.int32), pos])
        attn = _sink_window_attend(q[:, None], k_all[:, None], v_all[:, None], pos[None], kpos[None], sk)
        attn = attn.reshape(B, T, ATTN_WIDTH)
        k_keep = k_all[:, -w:]
        v_keep = v_all[:, -w:]
    attn = _rmsnorm(attn, attn_g) * jax.nn.silu(g_attn.astype(jnp.float32))
    o_hg, s_fin = _hgrn2(hq, hf, hi, lb, s0)
    o_hg = _rmsnorm(o_hg, hg_g.reshape(HG_HEADS, HG_DV)).reshape(B, T, HG_VW)
    o_hg = o_hg * jax.nn.silu(g_hg.astype(jnp.float32))
    mix = jnp.einsum('btc,cd->btd', jnp.concatenate([attn, o_hg], axis=-1), w_out.astype(jnp.float32))
    h = _layernorm(DN_ALPHA * x.astype(jnp.float32) + mix, ln_g, ln_b)
    gate = jax.nn.sigmoid(jnp.einsum('btd,de->bte', h, w_pg.astype(jnp.float32)))
    y = h + gate * jnp.einsum('btp,pd->btd', p.astype(jnp.float32), w_pp.astype(jnp.float32))
    return y, k_keep, v_keep, s_fin


def setup_inputs(seed: int = 0) -> dict:
    key = jax.random.key(seed)
    ks = jax.random.split(key, 20)
    cache_w = min(WINDOW, PAST_LEN)
    f32 = jnp.float32
    col_scale = jnp.concatenate([
        jnp.ones((ATTN_WIDTH + KV_WIDTH,), f32),
        jnp.full((KV_WIDTH,), DN_BETA, f32),
        jnp.ones((ATTN_WIDTH + 2 * HG_KW,), f32),
        jnp.full((HG_VW,), DN_BETA, f32),
        jnp.ones((HG_VW,), f32)])
    w_in = jax.random.normal(ks[7], (DEPTH, D_MODEL, IN_COLS), f32) * D_MODEL ** -0.5 * col_scale
    return {
        'x_prompt': jax.random.normal(ks[0], (BATCH, SEQ, D_MODEL), f32),
        'x_sample': jax.random.normal(ks[1], (DEC_BATCH, DEC_SEQ, D_MODEL), f32),
        'cache_k_win': jax.random.normal(ks[2], (DEPTH, DEC_BATCH, cache_w, KV_HEADS, HEAD_DIM), f32),
        'cache_v_win': jax.random.normal(ks[3], (DEPTH, DEC_BATCH, cache_w, KV_HEADS, HEAD_DIM), f32),
        'state_hgrn': 0.3 * jax.random.normal(ks[4], (DEPTH, DEC_BATCH, HG_HEADS, HG_DK, HG_DV), f32),
        'p_prompt': jax.random.normal(ks[5], (DEPTH, BATCH, SEQ, PLE_DIM), f32),
        'p_sample': jax.random.normal(ks[6], (DEPTH, DEC_BATCH, DEC_SEQ, PLE_DIM), f32),
        'w_in': w_in,
        'attn_sinks': 0.5 * jax.random.normal(ks[8], (DEPTH, ATTN_HEADS), f32),
        'attn_norm_g': 1.0 + 0.02 * jax.random.normal(ks[9], (DEPTH, ATTN_WIDTH), f32),
        'hg_lb_logits': 0.5 * jax.random.normal(ks[10], (DEPTH, HG_KW), f32),
        'hg_norm_g': 1.0 + 0.02 * jax.random.normal(ks[11], (DEPTH, HG_VW), f32),
        'w_out': jax.random.normal(ks[12], (DEPTH, MIX_WIDTH, D_MODEL), f32) * MIX_WIDTH ** -0.5 * DN_BETA,
        'ln_g': 1.0 + 0.02 * jax.random.normal(ks[13], (DEPTH, D_MODEL), f32),
        'ln_b': 0.02 * jax.random.normal(ks[14], (DEPTH, D_MODEL), f32),
        'w_ple_proj': jax.random.normal(ks[15], (DEPTH, PLE_DIM, D_MODEL), f32) * PLE_DIM ** -0.5,
        'w_ple_gate': jax.random.normal(ks[16], (DEPTH, D_MODEL, D_MODEL), f32) * D_MODEL ** -0.5,
    }


def reference(x_prompt, x_sample, cache_k_win, cache_v_win, state_hgrn, p_prompt, p_sample,
              w_in, attn_sinks, attn_norm_g, hg_lb_logits, hg_norm_g, w_out, ln_g, ln_b,
              w_ple_proj, w_ple_gate):
    cs = jnp.cumsum(jax.nn.softmax(hg_lb_logits.astype(jnp.float32), axis=0), axis=0)
    lbs = cs - cs[:1]
    pos_p = jnp.arange(x_prompt.shape[1], dtype=jnp.int32)
    pos_s = PAST_LEN + jnp.arange(x_sample.shape[1], dtype=jnp.int32)
    s0_p = jnp.zeros((x_prompt.shape[0], HG_HEADS, HG_DK, HG_DV), jnp.float32)
    yp, ys = x_prompt, x_sample
    kp_l, vp_l, sp_l, ks_l, vs_l, ss_l = [], [], [], [], [], []
    for i in range(DEPTH):
        lw = (w_in[i], attn_sinks[i], attn_norm_g[i], lbs[i], hg_norm_g[i], w_out[i],
              ln_g[i], ln_b[i], w_ple_proj[i], w_ple_gate[i])
        yp, kp, vp, sp = _mixer_layer(yp, p_prompt[i], pos_p, None, None, s0_p, *lw)
        ys, kk, vv, ss = _mixer_layer(ys, p_sample[i], pos_s, cache_k_win[i], cache_v_win[i],
                                      state_hgrn[i], *lw)
        kp_l.append(kp); vp_l.append(vp); sp_l.append(sp)
        ks_l.append(kk); vs_l.append(vv); ss_l.append(ss)
    new_k_win_prompt = jnp.stack(kp_l)
    new_v_win_prompt = jnp.stack(vp_l)
    new_state_hgrn_prompt = jnp.stack(sp_l)
    new_k_win_sample = jnp.stack(ks_l)
    new_v_win_sample = jnp.stack(vs_l)
    new_state_hgrn_sample = jnp.stack(ss_l)
    return (yp, ys, new_k_win_prompt, new_v_win_prompt, new_state_hgrn_prompt,
            new_k_win_sample, new_v_win_sample, new_state_hgrn_sample)
```

```python
import math

import numpy as np
import jax
import jax.numpy as jnp
from jax import lax
from jax.experimental import pallas as pl
from jax.experimental.pallas import tpu as pltpu

D_MODEL = 1024
DEPTH = 4
PAST_LEN = 8192
ATTN_HEADS = 8
KV_HEADS = 2
HEAD_DIM = 64
ATTN_WIDTH = ATTN_HEADS * HEAD_DIM
KV_WIDTH = KV_HEADS * HEAD_DIM
WINDOW = 128
ROPE_THETA = 10000.0
HG_HEADS = 4
HG_DK = 128
HG_DV = 128
HG_KW = HG_HEADS * HG_DK
HG_VW = HG_HEADS * HG_DV
MIX_WIDTH = ATTN_WIDTH + HG_VW
IN_COLS = 2 * ATTN_WIDTH + 2 * KV_WIDTH + 2 * HG_KW + 2 * HG_VW
PLE_DIM = 256
DN_ALPHA = (2 * DEPTH) ** 0.25
NORM_EPS = 1e-5
NEG_INF = -1e30

C_Q = 0
C_K = C_Q + ATTN_WIDTH
C_V = C_K + KV_WIDTH
C_GA = C_V + KV_WIDTH
C_HQ = C_GA + ATTN_WIDTH
C_HF = C_HQ + HG_KW
C_HI = C_HF + HG_KW
C_GH = C_HI + HG_VW

LANES = 128
BLK = 128
PROMPT_TILE = 256
SAMPLE_SEQS = 16
LOW_LEVELS = 3
PROMPT_LEVELS = 7
VMEM_LIMIT = 56 * 1024 * 1024

F32 = jnp.float32
BF16 = jnp.bfloat16


def _dot(a, b):
    return jnp.dot(a, b, preferred_element_type=F32)


def _dot_nt(a, b):
    return lax.dot_general(a, b, (((1,), (1,)), ((), ())), preferred_element_type=F32)


def _dot_tn(a, b):
    return lax.dot_general(a, b, (((0,), (0,)), ((), ())), preferred_element_type=F32)


def _split_bf16(x):
    hi = x.astype(BF16)
    lo = (x - hi.astype(F32)).astype(BF16)
    return hi, lo


def _sigmoid(x):
    return 1.0 / (1.0 + jnp.exp(-x))


def _lane_iota(rows):
    return lax.broadcasted_iota(jnp.int32, (rows, LANES), 1)


def _rope(x, cos_t, sin_t):
    lane = _lane_iota(x.shape[0])
    first_half = (lane & 32) == 0
    swapped = jnp.where(first_half, pltpu.roll(x, 96, 1), pltpu.roll(x, 32, 1))
    return x * cos_t + swapped * sin_t


def _kv_variants(a, fill):
    lane = _lane_iota(a.shape[0])
    lo = lane < HEAD_DIM
    sw = pltpu.roll(a, HEAD_DIM, 1)
    f = jnp.full_like(a, fill)
    out = [jnp.where(lo, a, f), jnp.where(lo, f, sw), jnp.where(lo, sw, f), jnp.where(lo, f, a)]
    return [o.astype(BF16) for o in out]


def _softmax_parts(s_cur, s_prev, sink_col):
    m = jnp.maximum(jnp.maximum(jnp.max(s_cur, axis=-1, keepdims=True),
                                jnp.max(s_prev, axis=-1, keepdims=True)), sink_col)
    return jnp.exp(s_cur - m), jnp.exp(s_prev - m), jnp.exp(sink_col - m)


def _sum_half(e, es, rows):
    lane = _lane_iota(rows)
    sum_half = (lane >= HEAD_DIM) if e == 0 else (lane < HEAD_DIM)
    return jnp.where(sum_half, es, 0.0)


def _assemble_attn(res, rows):
    lane = _lane_iota(rows)
    lo = lane < HEAD_DIM
    cols = []
    for c in range(4):
        g, cc = c // 2, c % 2
        r0 = res[g * 2 + 0][cc * rows:(cc + 1) * rows]
        r1 = res[g * 2 + 1][cc * rows:(cc + 1) * rows]
        num = jnp.where(lo, r0, r1)
        den = pltpu.roll(jnp.where(lo, r1, r0), HEAD_DIM, 1)
        cols.append(num / den)
    return cols


def _log_decay(hf, log_lb, log1m_lb):
    ls = jnp.minimum(hf, 0.0) - jnp.log1p(jnp.exp(-jnp.abs(hf)))
    b = log1m_lb + ls
    return jnp.maximum(log_lb, b) + jnp.log1p(jnp.exp(-jnp.abs(log_lb - b)))


def _hgrn_scores(q_h, k_h, e_low, g_h, lvl, levels):
    rows = lax.broadcasted_iota(jnp.int32, (BLK, LANES), 0)
    a = jnp.zeros((BLK, BLK), F32)
    for l in range(levels):
        upper = ((rows >> l) & 1) == 1
        if l < LOW_LEVELS:
            e = e_low[l]
        else:
            b, h = 2 << l, 1 << l
            pieces = []
            for i in range(BLK // b):
                r = i * b + h - 1
                pieces.append(jnp.broadcast_to(g_h[r:r + 1, :], (b, LANES)))
            gref = pieces[0] if len(pieces) == 1 else jnp.concatenate(pieces, axis=0)
            d = g_h - gref
            e = jnp.where(upper, d, -d)
        u = (jnp.where(upper, q_h, k_h) * jnp.exp(e)).astype(BF16)
        a = jnp.where(lvl == l, _dot_nt(u, u), a)
    diag = jnp.sum(q_h * k_h, axis=-1, keepdims=True)
    return jnp.where(lvl == -1, diag, a)


def _gated_rmsnorm_cols(cols, gain_row, gate, width):
    ss = None
    for c in cols:
        s = jnp.sum(c * c, axis=-1, keepdims=True)
        ss = s if ss is None else ss + s
    inv = lax.rsqrt(ss * (1.0 / width) + NORM_EPS)
    out = []
    for i, c in enumerate(cols):
        gt = gate[:, i * LANES:(i + 1) * LANES]
        out.append(c * inv * gain_row[:, i * LANES:(i + 1) * LANES] * (gt * _sigmoid(gt)))
    return out


def _epilogue(x, p, mix_in, w_out, w_pg, w_pp, ln_g, ln_b):
    mix = _dot(mix_in, w_out)
    hpre = DN_ALPHA * x + mix
    mu = jnp.mean(hpre, axis=-1, keepdims=True)
    cen = hpre - mu
    var = jnp.mean(cen * cen, axis=-1, keepdims=True)
    h = cen * lax.rsqrt(var + NORM_EPS) * ln_g + ln_b
    gate = _sigmoid(_dot(h.astype(BF16), w_pg))
    return h + gate * _dot(p.astype(BF16), w_pp)


def _prompt_kernel(sinks_ref, x_ref, p_ref, cq_ref, sq_ref, ck_ref, sk_ref, lvl_ref, mlow_ref,
                   w_in_ref, w_out_ref, w_pg_ref, w_pp_ref, v512_ref, v1024_ref,
                   y_ref, kk_ref, vk_ref, sfin_ref,
                   st_scr, kprev_scr, vprev_scr, mix_scr):
    t = pl.program_id(1)
    nblk = PROMPT_TILE // BLK

    @pl.when(t == 0)
    def _():
        st_scr[...] = jnp.zeros_like(st_scr)
        kprev_scr[...] = jnp.zeros_like(kprev_scr)
        vprev_scr[...] = jnp.zeros_like(vprev_scr)

    x = x_ref[...]
    z = _dot(x.astype(BF16), w_in_ref[...])
    lvl = lvl_ref[...]
    attn_g = v512_ref[0:1, :]
    log_lb = v512_ref[1:2, :]
    log1m_lb = v512_ref[2:3, :]
    hg_g = v512_ref[3:4, :]

    row = lax.broadcasted_iota(jnp.int32, (2 * BLK, BLK), 0)
    col = lax.broadcasted_iota(jnp.int32, (2 * BLK, BLK), 1)
    qrow = row & (BLK - 1)
    mask_cur = col <= qrow
    mask_prev_any = col > qrow
    first_rows = lax.broadcasted_iota(jnp.int32, (2 * BLK, 1), 0) < BLK

    k_var = v_var = None
    for j in range(nblk):
        r0 = j * BLK
        rs = slice(r0, r0 + BLK)
        qcols = [_rope(z[rs, C_Q + c * LANES:C_Q + (c + 1) * LANES], cq_ref[rs, :], sq_ref[rs, :]).astype(BF16)
                 for c in range(4)]
        k_rot = _rope(z[rs, C_K:C_K + KV_WIDTH], ck_ref[rs, :], sk_ref[rs, :])
        v_new = z[rs, C_V:C_V + KV_WIDTH]
        if j == 0:
            k_prev = [kprev_scr[i] for i in range(4)]
            v_prev = [vprev_scr[i] for i in range(4)]
            mask_prev = jnp.logical_and(mask_prev_any, t > 0)
        else:
            k_prev, v_prev = k_var, v_var
            mask_prev = mask_prev_any
        k_var = _kv_variants(k_rot, 0.0)
        v_var = _kv_variants(v_new, 1.0)
        res = []
        for g in range(2):
            qst = jnp.concatenate([qcols[2 * g], qcols[2 * g + 1]], axis=0)
            for e in range(2):
                i = g * 2 + e
                s_cur = jnp.where(mask_cur, _dot_nt(qst, k_var[i]), NEG_INF)
                s_prev = jnp.where(mask_prev, _dot_nt(qst, k_prev[i]), NEG_INF)
                sink_col = jnp.where(first_rows, sinks_ref[4 * g + e], sinks_ref[4 * g + 2 + e])
                p_cur, p_prev, es = _softmax_parts(s_cur, s_prev, sink_col)
                r = _dot(p_cur.astype(BF16), v_var[i]) + _dot(p_prev.astype(BF16), v_prev[i])
                res.append(r + _sum_half(e, es, 2 * BLK))
        acols = _assemble_attn(res, BLK)
        acols = _gated_rmsnorm_cols(acols, attn_g, z[rs, C_GA:C_GA + ATTN_WIDTH], ATTN_WIDTH)
        for c in range(4):
            mix_scr[rs, c * LANES:(c + 1) * LANES] = acols[c].astype(BF16)
        if j == nblk - 1:
            for i in range(4):
                kprev_scr[i] = k_var[i]
                vprev_scr[i] = v_var[i]

            @pl.when(t == pl.num_programs(1) - 1)
            def _():
                kk_ref[...] = k_rot
                vk_ref[...] = v_new

        lf = _log_decay(z[rs, C_HF:C_HF + HG_KW], log_lb, log1m_lb)
        kin = 1.0 - jnp.exp(lf)
        lf_hi, lf_lo = _split_bf16(lf)
        eg = _dot(mlow_ref[...], lf_hi) + _dot(mlow_ref[...], lf_lo)
        for h in range(HG_HEADS):
            hs = slice(h * LANES, (h + 1) * LANES)
            q_h = z[rs, C_HQ + h * LANES:C_HQ + (h + 1) * LANES]
            k_h = kin[:, hs]
            v_h = z[rs, C_HI + h * LANES:C_HI + (h + 1) * LANES].astype(BF16)
            e_low = [eg[l * BLK:(l + 1) * BLK, hs] for l in range(LOW_LEVELS)]
            g_h = eg[LOW_LEVELS * BLK:(LOW_LEVELS + 1) * BLK, hs]
            a = _hgrn_scores(q_h, k_h, e_low, g_h, lvl, PROMPT_LEVELS)
            st = st_scr[h]
            o_h = _dot(a.astype(BF16), v_h) + _dot_nt((q_h * jnp.exp(g_h)).astype(BF16), st.astype(BF16))
            g_last = g_h[BLK - 1:BLK, :]
            kd = (k_h * jnp.exp(g_last - g_h)).astype(BF16)
            st_scr[h] = st * jnp.exp(g_last) + _dot_tn(v_h, kd)
            gt = z[rs, C_GH + h * LANES:C_GH + (h + 1) * LANES]
            inv = lax.rsqrt(jnp.mean(o_h * o_h, axis=-1, keepdims=True) + NORM_EPS)
            mix_scr[rs, ATTN_WIDTH + h * LANES:ATTN_WIDTH + (h + 1) * LANES] = (
                o_h * inv * hg_g[:, hs] * (gt * _sigmoid(gt))).astype(BF16)

    y_ref[...] = _epilogue(x, p_ref[...], mix_scr[...], w_out_ref[...], w_pg_ref[...], w_pp_ref[...],
                           v1024_ref[0:1, :], v1024_ref[1:2, :])

    @pl.when(t == pl.num_programs(1) - 1)
    def _():
        for h in range(HG_HEADS):
            sfin_ref[h] = st_scr[h].T


def _sample_kernel(sinks_ref, x_ref, p_ref, ck_ref, cv_ref, s0_ref, cq_ref, sq_ref, ckt_ref, skt_ref,
                   lvl_ref, mlow_ref, w_in_ref, w_out_ref, w_pg_ref, w_pp_ref, v512_ref, v1024_ref,
                   y_ref, nk_ref, nv_ref, ns_ref,
                   q_scr, knew_scr, vnew_scr, scur_scr, pcur_scr, rprev_scr,
                   qe_scr, kd_scr, vh_scr, tot_scr, o_scr, mix_scr):
    rows = SAMPLE_SEQS * 8
    x = x_ref[...].reshape(rows, D_MODEL)
    z = _dot(x.astype(BF16), w_in_ref[...])
    lvl = lvl_ref[...]
    attn_g = v512_ref[0:1, :]
    log_lb = v512_ref[1:2, :]
    log1m_lb = v512_ref[2:3, :]
    hg_g = v512_ref[3:4, :]

    for c in range(4):
        q_scr[c] = _rope(z[:, C_Q + c * LANES:C_Q + (c + 1) * LANES], cq_ref[...], sq_ref[...])
    k_rot = _rope(z[:, C_K:C_K + KV_WIDTH], ckt_ref[...], skt_ref[...])
    v_new = z[:, C_V:C_V + KV_WIDTH]
    knew_scr[...] = k_rot
    vnew_scr[...] = v_new
    k_var = _kv_variants(k_rot, 0.0)
    v_var = _kv_variants(v_new, 1.0)
    row = lax.broadcasted_iota(jnp.int32, (2 * rows, rows), 0) & (rows - 1)
    col = lax.broadcasted_iota(jnp.int32, (2 * rows, rows), 1)
    mask_cur = jnp.logical_and((row >> 3) == (col >> 3), col <= row)
    for g in range(2):
        qst = jnp.concatenate([q_scr[2 * g], q_scr[2 * g + 1]], axis=0).astype(BF16)
        for e in range(2):
            i = g * 2 + e
            scur_scr[i] = jnp.where(mask_cur, _dot_nt(qst, k_var[i]), NEG_INF)

    lf = _log_decay(z[:, C_HF:C_HF + HG_KW], log_lb, log1m_lb)
    kin = 1.0 - jnp.exp(lf)
    lf_hi, lf_lo = _split_bf16(lf)
    eg = _dot(mlow_ref[...], lf_hi) + _dot(mlow_ref[...], lf_lo)
    tot_scr[...] = eg[(LOW_LEVELS + 1) * rows:(LOW_LEVELS + 2) * rows, :]
    for h in range(HG_HEADS):
        hs = slice(h * LANES, (h + 1) * LANES)
        q_h = z[:, C_HQ + h * LANES:C_HQ + (h + 1) * LANES]
        k_h = kin[:, hs]
        v_h = z[:, C_HI + h * LANES:C_HI + (h + 1) * LANES]
        e_low = [eg[l * rows:(l + 1) * rows, hs] for l in range(LOW_LEVELS)]
        g_h = eg[LOW_LEVELS * rows:(LOW_LEVELS + 1) * rows, hs]
        tot_h = eg[(LOW_LEVELS + 1) * rows:(LOW_LEVELS + 2) * rows, hs]
        a = _hgrn_scores(q_h, k_h, e_low, g_h, lvl, LOW_LEVELS)
        o_scr[h] = _dot(a.astype(BF16), v_h.astype(BF16))
        qe_scr[h] = q_h * jnp.exp(g_h)
        kd_scr[h] = k_h * jnp.exp(tot_h - g_h)
        vh_scr[h] = v_h

    lrow = lax.broadcasted_iota(jnp.int32, (16, LANES), 0)
    lcol = lax.broadcasted_iota(jnp.int32, (16, LANES), 1)
    mask_cache = lcol > (lrow & 7)
    first8 = lax.broadcasted_iota(jnp.int32, (16, 1), 0) < 8
    sub0 = lax.broadcasted_iota(jnp.int32, (8, LANES), 0) == 0
    ones8 = jnp.ones((8, LANES), BF16)

    def seq_body(s, carry):
        r0 = pl.multiple_of(s * 8, 8)
        rs = pl.ds(r0, 8)
        rs2 = pl.ds(pl.multiple_of(rows + s * 8, 8), 8)
        kc = ck_ref[s]
        vc = cv_ref[s]
        nk_ref[s, 0:WINDOW - 8, :] = kc[8:WINDOW, :]
        nk_ref[s, WINDOW - 8:WINDOW, :] = knew_scr[rs, :]
        nv_ref[s, 0:WINDOW - 8, :] = vc[8:WINDOW, :]
        nv_ref[s, WINDOW - 8:WINDOW, :] = vnew_scr[rs, :]
        kc_var = _kv_variants(kc, 0.0)
        vc_var = _kv_variants(vc, 1.0)
        for g in range(2):
            qst = jnp.concatenate([q_scr[2 * g, rs, :], q_scr[2 * g + 1, rs, :]], axis=0).astype(BF16)
            for e in range(2):
                i = g * 2 + e
                s_prev = jnp.where(mask_cache, _dot_nt(qst, kc_var[i]), NEG_INF)
                s_cur = jnp.concatenate([scur_scr[i, rs, :], scur_scr[i, rs2, :]], axis=0)
                sink_col = jnp.where(first8, sinks_ref[4 * g + e], sinks_ref[4 * g + 2 + e])
                p_cur, p_prev, es = _softmax_parts(s_cur, s_prev, sink_col)
                r = _dot(p_prev.astype(BF16), vc_var[i]) + _sum_half(e, es, 16)
                pcur_scr[i, rs, :] = p_cur[0:8]
                pcur_scr[i, rs2, :] = p_cur[8:16]
                rprev_scr[i, rs, :] = r[0:8]
                rprev_scr[i, rs2, :] = r[8:16]
        for h in range(HG_HEADS):
            hs = slice(h * LANES, (h + 1) * LANES)
            s0 = s0_ref[s, h]
            o_scr[h, rs, :] = o_scr[h, rs, :] + _dot(qe_scr[h, rs, :].astype(BF16), s0.astype(BF16))
            ds_ = _dot_tn(kd_scr[h, rs, :].astype(BF16), vh_scr[h, rs, :].astype(BF16))
            e8 = jnp.where(sub0, jnp.exp(tot_scr[rs, hs]), 0.0)
            e_hi, e_lo = _split_bf16(e8)
            decay = _dot_tn(e_hi, ones8) + _dot_tn(e_lo, ones8)
            ns_ref[s, h] = decay * s0 + ds_
        return carry

    lax.fori_loop(0, SAMPLE_SEQS, seq_body, 0)

    res = [rprev_scr[i] + _dot(pcur_scr[i].astype(BF16), v_var[i]) for i in range(4)]
    acols = _assemble_attn(res, rows)
    acols = _gated_rmsnorm_cols(acols, attn_g, z[:, C_GA:C_GA + ATTN_WIDTH], ATTN_WIDTH)
    for c in range(4):
        mix_scr[:, c * LANES:(c + 1) * LANES] = acols[c].astype(BF16)
    for h in range(HG_HEADS):
        hs = slice(h * LANES, (h + 1) * LANES)
        o_h = o_scr[h]
        gt = z[:, C_GH + h * LANES:C_GH + (h + 1) * LANES]
        inv = lax.rsqrt(jnp.mean(o_h * o_h, axis=-1, keepdims=True) + NORM_EPS)
        mix_scr[:, ATTN_WIDTH + h * LANES:ATTN_WIDTH + (h + 1) * LANES] = (
            o_h * inv * hg_g[:, hs] * (gt * _sigmoid(gt))).astype(BF16)

    y = _epilogue(x, p_ref[...].reshape(rows, PLE_DIM), mix_scr[...], w_out_ref[...], w_pg_ref[...],
                  w_pp_ref[...], v1024_ref[0:1, :], v1024_ref[1:2, :])
    y_ref[...] = y.reshape(SAMPLE_SEQS, 8, D_MODEL)


def _level_matrix():
    t = np.arange(BLK)[:, None]
    s = np.arange(BLK)[None, :]
    x = t ^ s
    lv = np.floor(np.log2(np.maximum(x, 1))).astype(np.int32)
    return np.where(t > s, lv, np.where(t == s, -1, -2)).astype(np.int32)


def _level_exponent_matrix(l):
    m = np.zeros((BLK, BLK), np.float32)
    b, h = 2 << l, 1 << l
    for t in range(BLK):
        mid = t - t % b + h
        if t >= mid:
            m[t, mid:t + 1] = 1.0
        else:
            m[t, t + 1:mid] = 1.0
    return m


def _mask_matrices(seq_rows):
    blocks = [_level_exponent_matrix(l) for l in range(LOW_LEVELS)]
    t = np.arange(BLK)[:, None]
    s = np.arange(BLK)[None, :]
    same = (t // seq_rows) == (s // seq_rows)
    blocks.append((same & (s <= t)).astype(np.float32))
    if seq_rows < BLK:
        blocks.append(same.astype(np.float32))
    return np.concatenate(blocks, axis=0)


def _rope_tables(pos, scale):
    half = HEAD_DIM // 2
    inv = jnp.exp(-math.log(ROPE_THETA) * jnp.arange(half, dtype=F32) * 2.0 / HEAD_DIM)
    ang = pos.astype(F32)[:, None] * inv[None, :]
    cos = jnp.cos(ang) * scale
    sin = jnp.sin(ang) * scale
    return jnp.tile(cos, (1, 4)), jnp.concatenate([-sin, sin, -sin, sin], axis=1)


def _const_spec(shape, layer=None):
    if layer is None:
        return pl.BlockSpec(shape, lambda *_: (0,) * len(shape))
    return pl.BlockSpec((None,) + shape, lambda *_: (layer,) + (0,) * len(shape))


def _prompt_layer(i, x, p_all, tables, lvl, mlow, weights, sinks, v512, v1024):
    B, T, _ = x.shape
    nt = T // PROMPT_TILE
    w_in, w_out, w_pg, w_pp = weights
    tab_spec = pl.BlockSpec((PROMPT_TILE, LANES), lambda b, t: (t, 0))
    in_specs = [
        pl.BlockSpec(memory_space=pltpu.SMEM),
        pl.BlockSpec((None, PROMPT_TILE, D_MODEL), lambda b, t: (b, t, 0)),
        pl.BlockSpec((None, None, PROMPT_TILE, PLE_DIM), lambda b, t: (i, b, t, 0)),
        tab_spec, tab_spec, tab_spec, tab_spec,
        _const_spec((BLK, BLK)),
        _const_spec(mlow.shape),
        _const_spec((D_MODEL, IN_COLS), i),
        _const_spec((MIX_WIDTH, D_MODEL), i),
        _const_spec((D_MODEL, D_MODEL), i),
        _const_spec((PLE_DIM, D_MODEL), i),
        _const_spec((4, ATTN_WIDTH), i),
        _const_spec((2, D_MODEL), i),
    ]
    out_shape = [
        jax.ShapeDtypeStruct((B, T, D_MODEL), F32),
        jax.ShapeDtypeStruct((B, WINDOW, KV_WIDTH), F32),
        jax.ShapeDtypeStruct((B, WINDOW, KV_WIDTH), F32),
        jax.ShapeDtypeStruct((B, HG_HEADS, HG_DK, HG_DV), F32),
    ]
    out_specs = [
        pl.BlockSpec((None, PROMPT_TILE, D_MODEL), lambda b, t: (b, t, 0)),
        pl.BlockSpec((None, WINDOW, KV_WIDTH), lambda b, t: (b, 0, 0)),
        pl.BlockSpec((None, WINDOW, KV_WIDTH), lambda b, t: (b, 0, 0)),
        pl.BlockSpec((None, HG_HEADS, HG_DK, HG_DV), lambda b, t: (b, 0, 0, 0)),
    ]
    scratch = [
        pltpu.VMEM((HG_HEADS, HG_DV, HG_DK), F32),
        pltpu.VMEM((4, BLK, LANES), BF16),
        pltpu.VMEM((4, BLK, LANES), BF16),
        pltpu.VMEM((PROMPT_TILE, MIX_WIDTH), BF16),
    ]
    return pl.pallas_call(
        _prompt_kernel,
        grid=(B, nt),
        in_specs=in_specs,
        out_specs=out_specs,
        out_shape=out_shape,
        scratch_shapes=scratch,
        compiler_params=pltpu.CompilerParams(
            dimension_semantics=("arbitrary", "arbitrary"), vmem_limit_bytes=VMEM_LIMIT),
        name=f"prompt_layer{i}",
    )(sinks, x, p_all, *tables, lvl, mlow, w_in, w_out, w_pg, w_pp, v512, v1024)


def _sample_layer(i, x, p_all, ck_all, cv_all, s0_all, tables, lvl, mlow, weights, sinks, v512, v1024):
    B, T, _ = x.shape
    rows = SAMPLE_SEQS * T
    w_in, w_out, w_pg, w_pp = weights
    tab_spec = _const_spec((rows, LANES))
    in_specs = [
        pl.BlockSpec(memory_space=pltpu.SMEM),
        pl.BlockSpec((SAMPLE_SEQS, T, D_MODEL), lambda b: (b, 0, 0)),
        pl.BlockSpec((None, SAMPLE_SEQS, T, PLE_DIM), lambda b: (i, b, 0, 0)),
        pl.BlockSpec((None, SAMPLE_SEQS, WINDOW, KV_WIDTH), lambda b: (i, b, 0, 0)),
        pl.BlockSpec((None, SAMPLE_SEQS, WINDOW, KV_WIDTH), lambda b: (i, b, 0, 0)),
        pl.BlockSpec((None, SAMPLE_SEQS, HG_HEADS, HG_DK, HG_DV), lambda b: (i, b, 0, 0, 0)),
        tab_spec, tab_spec, tab_spec, tab_spec,
        _const_spec((BLK, BLK)),
        _const_spec(mlow.shape),
        _const_spec((D_MODEL, IN_COLS), i),
        _const_spec((MIX_WIDTH, D_MODEL), i),
        _const_spec((D_MODEL, D_MODEL), i),
        _const_spec((PLE_DIM, D_MODEL), i),
        _const_spec((4, ATTN_WIDTH), i),
        _const_spec((2, D_MODEL), i),
    ]
    out_shape = [
        jax.ShapeDtypeStruct((B, T, D_MODEL), F32),
        jax.ShapeDtypeStruct((B, WINDOW, KV_WIDTH), F32),
        jax.ShapeDtypeStruct((B, WINDOW, KV_WIDTH), F32),
        jax.ShapeDtypeStruct((B, HG_HEADS, HG_DK, HG_DV), F32),
    ]
    out_specs = [
        pl.BlockSpec((SAMPLE_SEQS, T, D_MODEL), lambda b: (b, 0, 0)),
        pl.BlockSpec((SAMPLE_SEQS, WINDOW, KV_WIDTH), lambda b: (b, 0, 0)),
        pl.BlockSpec((SAMPLE_SEQS, WINDOW, KV_WIDTH), lambda b: (b, 0, 0)),
        pl.BlockSpec((SAMPLE_SEQS, HG_HEADS, HG_DK, HG_DV), lambda b: (b, 0, 0, 0)),
    ]
    scratch = [
        pltpu.VMEM((4, rows, LANES), F32),
        pltpu.VMEM((rows, LANES), F32),
        pltpu.VMEM((rows, LANES), F32),
        pltpu.VMEM((4, 2 * rows, rows), F32),
        pltpu.VMEM((4, 2 * rows, rows), F32),
        pltpu.VMEM((4, 2 * rows, LANES), F32),
        pltpu.VMEM((HG_HEADS, rows, LANES), F32),
        pltpu.VMEM((HG_HEADS, rows, LANES), F32),
        pltpu.VMEM((HG_HEADS, rows, LANES), F32),
        pltpu.VMEM((rows, HG_KW), F32),
        pltpu.VMEM((HG_HEADS, rows, LANES), F32),
        pltpu.VMEM((rows, MIX_WIDTH), BF16),
    ]
    return pl.pallas_call(
        _sample_kernel,
        grid=(B // SAMPLE_SEQS,),
        in_specs=in_specs,
        out_specs=out_specs,
        out_shape=out_shape,
        scratch_shapes=scratch,
        compiler_params=pltpu.CompilerParams(
            dimension_semantics=("arbitrary",), vmem_limit_bytes=VMEM_LIMIT),
        name=f"sample_layer{i}",
    )(sinks, x, p_all, ck_all, cv_all, s0_all, *tables, lvl, mlow, w_in, w_out, w_pg, w_pp, v512, v1024)


def kernel(x_prompt, x_sample, cache_k_win, cache_v_win, state_hgrn, p_prompt, p_sample, w_in, attn_sinks,
           attn_norm_g, hg_lb_logits, hg_norm_g, w_out, ln_g, ln_b, w_ple_proj, w_ple_gate):
    B, T, _ = x_prompt.shape
    SB, ST, _ = x_sample.shape
    assert T % PROMPT_TILE == 0 and SB % SAMPLE_SEQS == 0 and ST == 8 and SAMPLE_SEQS * ST == BLK
    assert cache_k_win.shape[2] == WINDOW

    cs = jnp.cumsum(jax.nn.softmax(hg_lb_logits.astype(F32), axis=0), axis=0)
    lbs = cs - cs[:1]
    v512 = jnp.stack([attn_norm_g.astype(F32), jnp.log(lbs), jnp.log1p(-lbs), hg_norm_g.astype(F32)], axis=1)
    v1024 = jnp.stack([ln_g.astype(F32), ln_b.astype(F32)], axis=1)
    weights = (w_in.astype(BF16), w_out.astype(BF16), w_ple_gate.astype(BF16), w_ple_proj.astype(BF16))

    scale = HEAD_DIM ** -0.5
    pos_p = jnp.arange(T, dtype=jnp.int32)
    pos_s = jnp.tile(PAST_LEN + jnp.arange(ST, dtype=jnp.int32), SAMPLE_SEQS)
    tab_p = _rope_tables(pos_p, scale) + _rope_tables(pos_p, 1.0)
    tab_s = _rope_tables(pos_s, scale) + _rope_tables(pos_s, 1.0)
    lvl = jnp.asarray(_level_matrix())
    mlow_p = jnp.asarray(_mask_matrices(BLK), dtype=BF16)
    mlow_s = jnp.asarray(_mask_matrices(ST), dtype=BF16)

    ck = cache_k_win.reshape(DEPTH, SB, WINDOW, KV_WIDTH)
    cv = cache_v_win.reshape(DEPTH, SB, WINDOW, KV_WIDTH)

    yp, ys = x_prompt, x_sample
    kp_l, vp_l, sp_l, ks_l, vs_l, ss_l = [], [], [], [], [], []
    for i in range(DEPTH):
        sinks = attn_sinks[i].astype(F32)
        yp, kp, vp, sp = _prompt_layer(i, yp, p_prompt, tab_p, lvl, mlow_p, weights, sinks, v512, v1024)
        ys, kk, vv, ss = _sample_layer(i, ys, p_sample, ck, cv, state_hgrn, tab_s, lvl, mlow_s, weights,
                                       sinks, v512, v1024)
        kp_l.append(kp); vp_l.append(vp); sp_l.append(sp)
        ks_l.append(kk); vs_l.append(vv); ss_l.append(ss)

    def kv5(lst, b):
        return jnp.stack(lst).reshape(DEPTH, b, WINDOW, KV_HEADS, HEAD_DIM)

    return (yp, ys, kv5(kp_l, B), kv5(vp_l, B), jnp.stack(sp_l),
            kv5(ks_l, SB), kv5(vs_l, SB), jnp.stack(ss_l))
```

```python
import math

import numpy as np
import jax
import jax.numpy as jnp
from jax import lax
from jax.experimental import pallas as pl
from jax.experimental.pallas import tpu as pltpu

D_MODEL = 1024
DEPTH = 4
PAST_LEN = 8192
ATTN_HEADS = 8
KV_HEADS = 2
HEAD_DIM = 64
ATTN_WIDTH = ATTN_HEADS * HEAD_DIM
KV_WIDTH = KV_HEADS * HEAD_DIM
WINDOW = 128
ROPE_THETA = 10000.0
HG_HEADS = 4
HG_DK = 128
HG_DV = 128
HG_KW = HG_HEADS * HG_DK
HG_VW = HG_HEADS * HG_DV
MIX_WIDTH = ATTN_WIDTH + HG_VW
IN_COLS = 2 * ATTN_WIDTH + 2 * KV_WIDTH + 2 * HG_KW + 2 * HG_VW
PLE_DIM = 256
DN_ALPHA = (2 * DEPTH) ** 0.25
NORM_EPS = 1e-5
NEG_INF = -1e30

C_Q = 0
C_K = C_Q + ATTN_WIDTH
C_V = C_K + KV_WIDTH
C_GA = C_V + KV_WIDTH
C_HQ = C_GA + ATTN_WIDTH
C_HF = C_HQ + HG_KW
C_HI = C_HF + HG_KW
C_GH = C_HI + HG_VW

LANES = 128
MXU_COLS = 256
BLK = 128
PROMPT_TILE = 512
SAMPLE_SEQS = 16
LOW_LEVELS = 3
PROMPT_LEVELS = 7
VMEM_LIMIT = 56 * 1024 * 1024
IN_CHUNKS = IN_COLS // MXU_COLS
OUT_CHUNKS = D_MODEL // MXU_COLS

F32 = jnp.float32
BF16 = jnp.bfloat16


def _dot(a, b):
    return jnp.dot(a, b, preferred_element_type=F32)


def _dot_nt(a, b):
    return lax.dot_general(a, b, (((1,), (1,)), ((), ())), preferred_element_type=F32)


def _dot_tn(a, b):
    return lax.dot_general(a, b, (((0,), (0,)), ((), ())), preferred_element_type=F32)


def _split_bf16(x):
    hi = x.astype(BF16)
    lo = (x - hi.astype(F32)).astype(BF16)
    return hi, lo


def _sigmoid(x):
    return 0.5 * jnp.tanh(0.5 * x) + 0.5


def _silu(x):
    h = 0.5 * x
    return h * jnp.tanh(h) + h


def _lane_iota(rows):
    return lax.broadcasted_iota(jnp.int32, (rows, LANES), 1)


def _rope(x, cos_t, sin_t):
    lane = _lane_iota(x.shape[0])
    first_half = (lane & 32) == 0
    swapped = jnp.where(first_half, pltpu.roll(x, 96, 1), pltpu.roll(x, 32, 1))
    return x * cos_t + swapped * sin_t


def _kv_variants(a, fill):
    lane = _lane_iota(a.shape[0])
    lo = lane < HEAD_DIM
    sw = pltpu.roll(a, HEAD_DIM, 1)
    f = jnp.full_like(a, fill)
    out = [jnp.where(lo, a, f), jnp.where(lo, f, sw), jnp.where(lo, sw, f), jnp.where(lo, f, a)]
    return [o.astype(BF16) for o in out]


def _softmax_parts(s_cur, s_prev, sink_col):
    m = jnp.maximum(jnp.maximum(jnp.max(s_cur, axis=-1, keepdims=True),
                                jnp.max(s_prev, axis=-1, keepdims=True)), sink_col)
    return jnp.exp(s_cur - m), jnp.exp(s_prev - m), jnp.exp(sink_col - m)


def _sum_half(e, es, rows):
    lane = _lane_iota(rows)
    sum_half = (lane >= HEAD_DIM) if e == 0 else (lane < HEAD_DIM)
    return jnp.where(sum_half, es, 0.0)


def _assemble_attn(res, rows):
    lane = _lane_iota(rows)
    lo = lane < HEAD_DIM
    cols = []
    for c in range(4):
        g, cc = c // 2, c % 2
        r0 = res[g * 2 + 0][cc * rows:(cc + 1) * rows]
        r1 = res[g * 2 + 1][cc * rows:(cc + 1) * rows]
        num = jnp.where(lo, r0, r1)
        den = pltpu.roll(jnp.where(lo, r1, r0), HEAD_DIM, 1)
        cols.append(num / den)
    return cols


def _log_decay(hf, log_lb, log1m_lb):
    ls = jnp.minimum(hf, 0.0) - jnp.log(1.0 + jnp.exp(-jnp.abs(hf)))
    b = log1m_lb + ls
    return jnp.maximum(log_lb, b) + jnp.log(1.0 + jnp.exp(-jnp.abs(log_lb - b)))


def _level_exponent(l, e_low, g_h, upper):
    if l < LOW_LEVELS:
        return e_low[l]
    b, h = 2 << l, 1 << l
    pieces = []
    for i in range(BLK // b):
        r = i * b + h - 1
        pieces.append(jnp.broadcast_to(g_h[r:r + 1, :], (b, LANES)))
    gref = pieces[0] if len(pieces) == 1 else jnp.concatenate(pieces, axis=0)
    d = g_h - gref
    return jnp.where(upper, d, -d)


def _hgrn_scores_steps(q_h, k_h, e_low, g_h, lvl, levels, out):
    rows = lax.broadcasted_iota(jnp.int32, (BLK, LANES), 0)
    a = jnp.zeros((BLK, BLK), F32)
    for l in range(levels):
        upper = ((rows >> l) & 1) == 1
        e = _level_exponent(l, e_low, g_h, upper)
        u = (jnp.where(upper, q_h, k_h) * jnp.exp(e)).astype(BF16)
        a = jnp.where(lvl == l, _dot_nt(u, u), a)
        yield
    diag = jnp.sum(q_h * k_h, axis=-1, keepdims=True)
    out["a"] = jnp.where(lvl == -1, diag, a)


def _hgrn_scores(q_h, k_h, e_low, g_h, lvl, levels):
    out = {}
    for _ in _hgrn_scores_steps(q_h, k_h, e_low, g_h, lvl, levels, out):
        pass
    return out["a"]


def _gated_rmsnorm_cols(cols, gain_row, gate, width):
    ss = None
    for c in cols:
        s = jnp.sum(c * c, axis=-1, keepdims=True)
        ss = s if ss is None else ss + s
    inv = lax.rsqrt(ss * (1.0 / width) + NORM_EPS)
    out = []
    for i, c in enumerate(cols):
        gt = gate[:, i * LANES:(i + 1) * LANES]
        out.append(c * inv * gain_row[:, i * LANES:(i + 1) * LANES] * _silu(gt))
    return out


def _layernorm(hpre, ln_g, ln_b):
    mu = jnp.mean(hpre, axis=-1, keepdims=True)
    cen = hpre - mu
    var = jnp.mean(cen * cen, axis=-1, keepdims=True)
    return cen * lax.rsqrt(var + NORM_EPS) * ln_g + ln_b


def _chunked_dot(a, w_ref, n):
    return jnp.concatenate([_dot(a, w_ref[c]) for c in range(n)], axis=1)


def _interleave(gen, n_yields, thunks):
    thunks = list(thunks)
    total, done, seen = len(thunks), 0, 0
    for _ in gen:
        seen += 1
        want = min(total, (seen * total + n_yields - 1) // n_yields)
        while done < want:
            thunks[done]()
            done += 1
    while done < total:
        thunks[done]()
        done += 1


def _prompt_kernel(sinks_ref, x_ref, p_ref, cq_ref, sq_ref, ck_ref, sk_ref, lvl_ref, mlow_ref,
                   w_in_ref, w_out_ref, w_pg_ref, w_pp_ref, v512_ref, v1024_ref,
                   y_ref, kk_ref, vk_ref, sfin_ref,
                   st_scr, kprev_scr, vprev_scr, z_scr, mix_scr):
    t = pl.program_id(1)
    nblk = PROMPT_TILE // BLK

    @pl.when(t == 0)
    def _():
        st_scr[...] = jnp.zeros_like(st_scr)
        kprev_scr[...] = jnp.zeros_like(kprev_scr)
        vprev_scr[...] = jnp.zeros_like(vprev_scr)

    lvl = lvl_ref[...]
    attn_g = v512_ref[0:1, :]
    log_lb = v512_ref[1:2, :]
    log1m_lb = v512_ref[2:3, :]
    hg_g = v512_ref[3:4, :]

    row = lax.broadcasted_iota(jnp.int32, (2 * BLK, BLK), 0)
    col = lax.broadcasted_iota(jnp.int32, (2 * BLK, BLK), 1)
    qrow = row & (BLK - 1)
    mask_cur = col <= qrow
    mask_prev_any = col > qrow
    first_rows = lax.broadcasted_iota(jnp.int32, (2 * BLK, 1), 0) < BLK
    carry = {}

    def stage_in(j):
        slot, rows_j, cell = j % 2, slice(j * BLK, (j + 1) * BLK), {}

        def chunk(c):
            def run():
                if "xb" not in cell:
                    cell["xb"] = x_ref[rows_j, :].astype(BF16)
                z_scr[slot, :, c * MXU_COLS:(c + 1) * MXU_COLS] = _dot(cell["xb"], w_in_ref[c])
            return run
        return [chunk(c) for c in range(IN_CHUNKS)]

    def stage_out(j):
        slot, rows_j, cell = j % 2, slice(j * BLK, (j + 1) * BLK), {"m": [], "g": []}

        def mix_chunk(c):
            def run():
                cell["m"].append(_dot(mix_scr[slot], w_out_ref[c]))
                if c == OUT_CHUNKS - 1:
                    hpre = DN_ALPHA * x_ref[rows_j, :] + jnp.concatenate(cell.pop("m"), axis=1)
                    h = _layernorm(hpre, v1024_ref[0:1, :], v1024_ref[1:2, :])
                    cell["h"] = h
                    cell["hb"] = h.astype(BF16)
            return run

        def gate_chunk(c):
            def run():
                cell["g"].append(_sigmoid(_dot(cell["hb"], w_pg_ref[c])))
            return run

        def finish():
            pb = p_ref[rows_j, :].astype(BF16)
            for c in range(OUT_CHUNKS):
                cs = slice(c * MXU_COLS, (c + 1) * MXU_COLS)
                y_ref[rows_j, cs] = cell["h"][:, cs] + cell["g"][c] * _dot(pb, w_pp_ref[c])
        return ([mix_chunk(c) for c in range(OUT_CHUNKS)] + [gate_chunk(c) for c in range(OUT_CHUNKS)]
                + [finish])

    mix_yields = 2 + 3 * 4 + 2 + HG_HEADS * (PROMPT_LEVELS + 2)

    def stage_mix(j):
        slot, rows_j = j % 2, slice(j * BLK, (j + 1) * BLK)

        def zc(c0, width):
            return z_scr[slot, :, c0:c0 + width]
        qcols = [_rope(zc(C_Q + c * LANES, LANES), cq_ref[rows_j, :], sq_ref[rows_j, :]).astype(BF16)
                 for c in range(4)]
        k_rot = _rope(zc(C_K, KV_WIDTH), ck_ref[rows_j, :], sk_ref[rows_j, :])
        v_new = zc(C_V, KV_WIDTH)
        yield
        if j == 0:
            k_prev = [kprev_scr[i] for i in range(4)]
            v_prev = [vprev_scr[i] for i in range(4)]
            mask_prev = jnp.logical_and(mask_prev_any, t > 0)
        else:
            k_prev, v_prev = carry["k_var"], carry["v_var"]
            mask_prev = mask_prev_any
        k_var = _kv_variants(k_rot, 0.0)
        v_var = _kv_variants(v_new, 1.0)
        carry["k_var"], carry["v_var"] = k_var, v_var
        if j == nblk - 1:
            for i in range(4):
                kprev_scr[i] = k_var[i]
                vprev_scr[i] = v_var[i]
            kk_ref[...] = k_rot
            vk_ref[...] = v_new
        yield
        res = []
        for g in range(2):
            qst = jnp.concatenate([qcols[2 * g], qcols[2 * g + 1]], axis=0)
            for e in range(2):
                i = g * 2 + e
                s_cur = jnp.where(mask_cur, _dot_nt(qst, k_var[i]), NEG_INF)
                s_prev = jnp.where(mask_prev, _dot_nt(qst, k_prev[i]), NEG_INF)
                yield
                sink_col = jnp.where(first_rows, sinks_ref[4 * g + e], sinks_ref[4 * g + 2 + e])
                p_cur, p_prev, es = _softmax_parts(s_cur, s_prev, sink_col)
                yield
                r = _dot(p_cur.astype(BF16), v_var[i]) + _dot(p_prev.astype(BF16), v_prev[i])
                res.append(r + _sum_half(e, es, 2 * BLK))
                yield
        acols = _assemble_attn(res, BLK)
        acols = _gated_rmsnorm_cols(acols, attn_g, zc(C_GA, ATTN_WIDTH), ATTN_WIDTH)
        for c in range(4):
            mix_scr[slot, :, c * LANES:(c + 1) * LANES] = acols[c].astype(BF16)
        yield
        lf = _log_decay(zc(C_HF, HG_KW), log_lb, log1m_lb)
        kin = 1.0 - jnp.exp(lf)
        lf_hi, lf_lo = _split_bf16(lf)
        eg = _dot(mlow_ref[...], lf_hi) + _dot(mlow_ref[...], lf_lo)
        yield
        for h in range(HG_HEADS):
            hs = slice(h * LANES, (h + 1) * LANES)
            q_h = zc(C_HQ + h * LANES, LANES)
            k_h = kin[:, hs]
            v_h = zc(C_HI + h * LANES, LANES).astype(BF16)
            e_low = [eg[l * BLK:(l + 1) * BLK, hs] for l in range(LOW_LEVELS)]
            g_h = eg[LOW_LEVELS * BLK:(LOW_LEVELS + 1) * BLK, hs]
            out = {}
            yield from _hgrn_scores_steps(q_h, k_h, e_low, g_h, lvl, PROMPT_LEVELS, out)
            st = st_scr[h]
            o_h = (_dot(out["a"].astype(BF16), v_h)
                   + _dot_nt((q_h * jnp.exp(g_h)).astype(BF16), st.astype(BF16)))
            g_last = g_h[BLK - 1:BLK, :]
            kd = (k_h * jnp.exp(g_last - g_h)).astype(BF16)
            st_scr[h] = st * jnp.exp(g_last) + _dot_tn(v_h, kd)
            yield
            gt = zc(C_GH + h * LANES, LANES)
            inv = lax.rsqrt(jnp.mean(o_h * o_h, axis=-1, keepdims=True) + NORM_EPS)
            mix_scr[slot, :, ATTN_WIDTH + h * LANES:ATTN_WIDTH + (h + 1) * LANES] = (
                o_h * inv * hg_g[:, hs] * _silu(gt)).astype(BF16)
            yield

    for th in stage_in(0):
        th()
    for j in range(nblk):
        side = stage_in(j + 1) if j + 1 < nblk else []
        if j >= 1:
            side = side + stage_out(j - 1)
        _interleave(stage_mix(j), mix_yields, side)
    for th in stage_out(nblk - 1):
        th()

    @pl.when(t == pl.num_programs(1) - 1)
    def _():
        for h in range(HG_HEADS):
            sfin_ref[h] = st_scr[h].T


def _sample_kernel(sinks_ref, x_ref, p_ref, ck_ref, cv_ref, s0_ref, cq_ref, sq_ref, ckt_ref, skt_ref,
                   lvl_ref, mlow_ref, w_in_ref, w_out_ref, w_pg_ref, w_pp_ref, v512_ref, v1024_ref,
                   y_ref, nk_ref, nv_ref, ns_ref,
                   q_scr, knew_scr, vnew_scr, scur_scr, pcur_scr, rprev_scr,
                   qe_scr, kd_scr, vh_scr, tot_scr, o_scr, mix_scr):
    rows = SAMPLE_SEQS * 8
    x = x_ref[...].reshape(rows, D_MODEL)
    z = _chunked_dot(x.astype(BF16), w_in_ref, IN_CHUNKS)
    lvl = lvl_ref[...]
    attn_g = v512_ref[0:1, :]
    log_lb = v512_ref[1:2, :]
    log1m_lb = v512_ref[2:3, :]
    hg_g = v512_ref[3:4, :]

    for c in range(4):
        q_scr[c] = _rope(z[:, C_Q + c * LANES:C_Q + (c + 1) * LANES], cq_ref[...], sq_ref[...])
    k_rot = _rope(z[:, C_K:C_K + KV_WIDTH], ckt_ref[...], skt_ref[...])
    v_new = z[:, C_V:C_V + KV_WIDTH]
    knew_scr[...] = k_rot
    vnew_scr[...] = v_new
    k_var = _kv_variants(k_rot, 0.0)
    v_var = _kv_variants(v_new, 1.0)
    row = lax.broadcasted_iota(jnp.int32, (2 * rows, rows), 0) & (rows - 1)
    col = lax.broadcasted_iota(jnp.int32, (2 * rows, rows), 1)
    mask_cur = jnp.logical_and((row >> 3) == (col >> 3), col <= row)
    for g in range(2):
        qst = jnp.concatenate([q_scr[2 * g], q_scr[2 * g + 1]], axis=0).astype(BF16)
        for e in range(2):
            i = g * 2 + e
            scur_scr[i] = jnp.where(mask_cur, _dot_nt(qst, k_var[i]), NEG_INF)

    lf = _log_decay(z[:, C_HF:C_HF + HG_KW], log_lb, log1m_lb)
    kin = 1.0 - jnp.exp(lf)
    lf_hi, lf_lo = _split_bf16(lf)
    eg = _dot(mlow_ref[...], lf_hi) + _dot(mlow_ref[...], lf_lo)
    tot_scr[...] = eg[(LOW_LEVELS + 1) * rows:(LOW_LEVELS + 2) * rows, :]
    for h in range(HG_HEADS):
        hs = slice(h * LANES, (h + 1) * LANES)
        q_h = z[:, C_HQ + h * LANES:C_HQ + (h + 1) * LANES]
        k_h = kin[:, hs]
        v_h = z[:, C_HI + h * LANES:C_HI + (h + 1) * LANES]
        e_low = [eg[l * rows:(l + 1) * rows, hs] for l in range(LOW_LEVELS)]
        g_h = eg[LOW_LEVELS * rows:(LOW_LEVELS + 1) * rows, hs]
        tot_h = eg[(LOW_LEVELS + 1) * rows:(LOW_LEVELS + 2) * rows, hs]
        a = _hgrn_scores(q_h, k_h, e_low, g_h, lvl, LOW_LEVELS)
        o_scr[h] = _dot(a.astype(BF16), v_h.astype(BF16))
        qe_scr[h] = q_h * jnp.exp(g_h)
        kd_scr[h] = k_h * jnp.exp(tot_h - g_h)
        vh_scr[h] = v_h

    lrow = lax.broadcasted_iota(jnp.int32, (16, LANES), 0)
    lcol = lax.broadcasted_iota(jnp.int32, (16, LANES), 1)
    mask_cache = lcol > (lrow & 7)
    first8 = lax.broadcasted_iota(jnp.int32, (16, 1), 0) < 8
    sub0 = lax.broadcasted_iota(jnp.int32, (8, LANES), 0) == 0
    ones8 = jnp.ones((8, LANES), BF16)

    def seq_body(s, carry):
        r0 = pl.multiple_of(s * 8, 8)
        rs = pl.ds(r0, 8)
        rs2 = pl.ds(pl.multiple_of(rows + s * 8, 8), 8)
        kc = ck_ref[s]
        vc = cv_ref[s]
        nk_ref[s, 0:WINDOW - 8, :] = kc[8:WINDOW, :]
        nk_ref[s, WINDOW - 8:WINDOW, :] = knew_scr[rs, :]
        nv_ref[s, 0:WINDOW - 8, :] = vc[8:WINDOW, :]
        nv_ref[s, WINDOW - 8:WINDOW, :] = vnew_scr[rs, :]
        kc_var = _kv_variants(kc, 0.0)
        vc_var = _kv_variants(vc, 1.0)
        for g in range(2):
            qst = jnp.concatenate([q_scr[2 * g, rs, :], q_scr[2 * g + 1, rs, :]], axis=0).astype(BF16)
            for e in range(2):
                i = g * 2 + e
                s_prev = jnp.where(mask_cache, _dot_nt(qst, kc_var[i]), NEG_INF)
                s_cur = jnp.concatenate([scur_scr[i, rs, :], scur_scr[i, rs2, :]], axis=0)
                sink_col = jnp.where(first8, sinks_ref[4 * g + e], sinks_ref[4 * g + 2 + e])
                p_cur, p_prev, es = _softmax_parts(s_cur, s_prev, sink_col)
                r = _dot(p_prev.astype(BF16), vc_var[i]) + _sum_half(e, es, 16)
                pcur_scr[i, rs, :] = p_cur[0:8]
                pcur_scr[i, rs2, :] = p_cur[8:16]
                rprev_scr[i, rs, :] = r[0:8]
                rprev_scr[i, rs2, :] = r[8:16]
        for h in range(HG_HEADS):
            hs = slice(h * LANES, (h + 1) * LANES)
            s0 = s0_ref[s, h]
            o_scr[h, rs, :] = o_scr[h, rs, :] + _dot(qe_scr[h, rs, :].astype(BF16), s0.astype(BF16))
            ds_ = _dot_tn(kd_scr[h, rs, :].astype(BF16), vh_scr[h, rs, :].astype(BF16))
            e8 = jnp.where(sub0, jnp.exp(tot_scr[rs, hs]), 0.0)
            e_hi, e_lo = _split_bf16(e8)
            decay = _dot_tn(e_hi, ones8) + _dot_tn(e_lo, ones8)
            ns_ref[s, h] = decay * s0 + ds_
        return carry

    lax.fori_loop(0, SAMPLE_SEQS, seq_body, 0, unroll=4)

    res = [rprev_scr[i] + _dot(pcur_scr[i].astype(BF16), v_var[i]) for i in range(4)]
    acols = _assemble_attn(res, rows)
    acols = _gated_rmsnorm_cols(acols, attn_g, z[:, C_GA:C_GA + ATTN_WIDTH], ATTN_WIDTH)
    for c in range(4):
        mix_scr[:, c * LANES:(c + 1) * LANES] = acols[c].astype(BF16)
    for h in range(HG_HEADS):
        hs = slice(h * LANES, (h + 1) * LANES)
        o_h = o_scr[h]
        gt = z[:, C_GH + h * LANES:C_GH + (h + 1) * LANES]
        inv = lax.rsqrt(jnp.mean(o_h * o_h, axis=-1, keepdims=True) + NORM_EPS)
        mix_scr[:, ATTN_WIDTH + h * LANES:ATTN_WIDTH + (h + 1) * LANES] = (
            o_h * inv * hg_g[:, hs] * _silu(gt)).astype(BF16)

    hpre = DN_ALPHA * x + _chunked_dot(mix_scr[...], w_out_ref, OUT_CHUNKS)
    h = _layernorm(hpre, v1024_ref[0:1, :], v1024_ref[1:2, :])
    gate = _sigmoid(_chunked_dot(h.astype(BF16), w_pg_ref, OUT_CHUNKS))
    pp = _chunked_dot(p_ref[...].reshape(rows, PLE_DIM).astype(BF16), w_pp_ref, OUT_CHUNKS)
    y_ref[...] = (h + gate * pp).reshape(SAMPLE_SEQS, 8, D_MODEL)


def _level_matrix():
    t = np.arange(BLK)[:, None]
    s = np.arange(BLK)[None, :]
    x = t ^ s
    lv = np.floor(np.log2(np.maximum(x, 1))).astype(np.int32)
    return np.where(t > s, lv, np.where(t == s, -1, -2)).astype(np.int32)


def _level_exponent_matrix(l):
    m = np.zeros((BLK, BLK), np.float32)
    b, h = 2 << l, 1 << l
    for t in range(BLK):
        mid = t - t % b + h
        if t >= mid:
            m[t, mid:t + 1] = 1.0
        else:
            m[t, t + 1:mid] = 1.0
    return m


def _mask_matrices(seq_rows):
    blocks = [_level_exponent_matrix(l) for l in range(LOW_LEVELS)]
    t = np.arange(BLK)[:, None]
    s = np.arange(BLK)[None, :]
    same = (t // seq_rows) == (s // seq_rows)
    blocks.append((same & (s <= t)).astype(np.float32))
    if seq_rows < BLK:
        blocks.append(same.astype(np.float32))
    return np.concatenate(blocks, axis=0)


def _rope_tables(pos, scale):
    half = HEAD_DIM // 2
    inv = jnp.exp(-math.log(ROPE_THETA) * jnp.arange(half, dtype=F32) * 2.0 / HEAD_DIM)
    ang = pos.astype(F32)[:, None] * inv[None, :]
    cos = jnp.cos(ang) * scale
    sin = jnp.sin(ang) * scale
    return jnp.tile(cos, (1, 4)), jnp.concatenate([-sin, sin, -sin, sin], axis=1)


def _column_chunks(w):
    d, k, n = w.shape
    return w.astype(BF16).reshape(d, k, n // MXU_COLS, MXU_COLS).transpose(0, 2, 1, 3)


def _const_spec(shape, layer=None):
    if layer is None:
        return pl.BlockSpec(shape, lambda *_: (0,) * len(shape))
    return pl.BlockSpec((None,) + shape, lambda *_: (layer,) + (0,) * len(shape))


def _weight_specs(i):
    return [
        _const_spec((IN_CHUNKS, D_MODEL, MXU_COLS), i),
        _const_spec((OUT_CHUNKS, MIX_WIDTH, MXU_COLS), i),
        _const_spec((OUT_CHUNKS, D_MODEL, MXU_COLS), i),
        _const_spec((OUT_CHUNKS, PLE_DIM, MXU_COLS), i),
        _const_spec((4, ATTN_WIDTH), i),
        _const_spec((2, D_MODEL), i),
    ]


def _prompt_layer(i, x, p_all, tables, lvl, mlow, weights, sinks, v512, v1024):
    B, T, _ = x.shape
    nt = T // PROMPT_TILE
    tab_spec = pl.BlockSpec((PROMPT_TILE, LANES), lambda b, t: (t, 0))
    in_specs = [
        pl.BlockSpec(memory_space=pltpu.SMEM),
        pl.BlockSpec((None, PROMPT_TILE, D_MODEL), lambda b, t: (b, t, 0)),
        pl.BlockSpec((None, None, PROMPT_TILE, PLE_DIM), lambda b, t: (i, b, t, 0)),
        tab_spec, tab_spec, tab_spec, tab_spec,
        _const_spec((BLK, BLK)),
        _const_spec(mlow.shape),
    ] + _weight_specs(i)
    out_shape = [
        jax.ShapeDtypeStruct((B, T, D_MODEL), F32),
        jax.ShapeDtypeStruct((B, WINDOW, KV_WIDTH), F32),
        jax.ShapeDtypeStruct((B, WINDOW, KV_WIDTH), F32),
        jax.ShapeDtypeStruct((B, HG_HEADS, HG_DK, HG_DV), F32),
    ]
    out_specs = [
        pl.BlockSpec((None, PROMPT_TILE, D_MODEL), lambda b, t: (b, t, 0)),
        pl.BlockSpec((None, WINDOW, KV_WIDTH), lambda b, t: (b, 0, 0)),
        pl.BlockSpec((None, WINDOW, KV_WIDTH), lambda b, t: (b, 0, 0)),
        pl.BlockSpec((None, HG_HEADS, HG_DK, HG_DV), lambda b, t: (b, 0, 0, 0)),
    ]
    scratch = [
        pltpu.VMEM((HG_HEADS, HG_DV, HG_DK), F32),
        pltpu.VMEM((4, BLK, LANES), BF16),
        pltpu.VMEM((4, BLK, LANES), BF16),
        pltpu.VMEM((2, BLK, IN_COLS), F32),
        pltpu.VMEM((2, BLK, MIX_WIDTH), BF16),
    ]
    return pl.pallas_call(
        _prompt_kernel,
        grid=(B, nt),
        in_specs=in_specs,
        out_specs=out_specs,
        out_shape=out_shape,
        scratch_shapes=scratch,
        compiler_params=pltpu.CompilerParams(
            dimension_semantics=("arbitrary", "arbitrary"), vmem_limit_bytes=VMEM_LIMIT),
        name=f"prompt_layer{i}",
    )(sinks, x, p_all, *tables, lvl, mlow, *weights, v512, v1024)


def _sample_layer(i, x, p_all, ck_all, cv_all, s0_all, tables, lvl, mlow, weights, sinks, v512, v1024):
    B, T, _ = x.shape
    rows = SAMPLE_SEQS * T
    tab_spec = _const_spec((rows, LANES))
    in_specs = [
        pl.BlockSpec(memory_space=pltpu.SMEM),
        pl.BlockSpec((SAMPLE_SEQS, T, D_MODEL), lambda b: (b, 0, 0)),
        pl.BlockSpec((None, SAMPLE_SEQS, T, PLE_DIM), lambda b: (i, b, 0, 0)),
        pl.BlockSpec((None, SAMPLE_SEQS, WINDOW, KV_WIDTH), lambda b: (i, b, 0, 0)),
        pl.BlockSpec((None, SAMPLE_SEQS, WINDOW, KV_WIDTH), lambda b: (i, b, 0, 0)),
        pl.BlockSpec((None, SAMPLE_SEQS, HG_HEADS, HG_DK, HG_DV), lambda b: (i, b, 0, 0, 0)),
        tab_spec, tab_spec, tab_spec, tab_spec,
        _const_spec((BLK, BLK)),
        _const_spec(mlow.shape),
    ] + _weight_specs(i)
    out_shape = [
        jax.ShapeDtypeStruct((B, T, D_MODEL), F32),
        jax.ShapeDtypeStruct((B, WINDOW, KV_WIDTH), F32),
        jax.ShapeDtypeStruct((B, WINDOW, KV_WIDTH), F32),
        jax.ShapeDtypeStruct((B, HG_HEADS, HG_DK, HG_DV), F32),
    ]
    out_specs = [
        pl.BlockSpec((SAMPLE_SEQS, T, D_MODEL), lambda b: (b, 0, 0)),
        pl.BlockSpec((SAMPLE_SEQS, WINDOW, KV_WIDTH), lambda b: (b, 0, 0)),
        pl.BlockSpec((SAMPLE_SEQS, WINDOW, KV_WIDTH), lambda b: (b, 0, 0)),
        pl.BlockSpec((SAMPLE_SEQS, HG_HEADS, HG_DK, HG_DV), lambda b: (b, 0, 0, 0)),
    ]
    scratch = [
        pltpu.VMEM((4, rows, LANES), F32),
        pltpu.VMEM((rows, LANES), F32),
        pltpu.VMEM((rows, LANES), F32),
        pltpu.VMEM((4, 2 * rows, rows), F32),
        pltpu.VMEM((4, 2 * rows, rows), F32),
        pltpu.VMEM((4, 2 * rows, LANES), F32),
        pltpu.VMEM((HG_HEADS, rows, LANES), F32),
        pltpu.VMEM((HG_HEADS, rows, LANES), F32),
        pltpu.VMEM((HG_HEADS, rows, LANES), F32),
        pltpu.VMEM((rows, HG_KW), F32),
        pltpu.VMEM((HG_HEADS, rows, LANES), F32),
        pltpu.VMEM((rows, MIX_WIDTH), BF16),
    ]
    return pl.pallas_call(
        _sample_kernel,
        grid=(B // SAMPLE_SEQS,),
        in_specs=in_specs,
        out_specs=out_specs,
        out_shape=out_shape,
        scratch_shapes=scratch,
        compiler_params=pltpu.CompilerParams(
            dimension_semantics=("arbitrary",), vmem_limit_bytes=VMEM_LIMIT),
        name=f"sample_layer{i}",
    )(sinks, x, p_all, ck_all, cv_all, s0_all, *tables, lvl, mlow, *weights, v512, v1024)


def kernel(x_prompt, x_sample, cache_k_win, cache_v_win, state_hgrn, p_prompt, p_sample, w_in, attn_sinks,
           attn_norm_g, hg_lb_logits, hg_norm_g, w_out, ln_g, ln_b, w_ple_proj, w_ple_gate):
    B, T, _ = x_prompt.shape
    SB, ST, _ = x_sample.shape
    assert T % PROMPT_TILE == 0 and SB % SAMPLE_SEQS == 0 and ST == 8 and SAMPLE_SEQS * ST == BLK
    assert cache_k_win.shape[2] == WINDOW

    cs = jnp.cumsum(jax.nn.softmax(hg_lb_logits.astype(F32), axis=0), axis=0)
    lbs = cs - cs[:1]
    v512 = jnp.stack([attn_norm_g.astype(F32), jnp.log(lbs), jnp.log1p(-lbs), hg_norm_g.astype(F32)], axis=1)
    v1024 = jnp.stack([ln_g.astype(F32), ln_b.astype(F32)], axis=1)
    weights = tuple(_column_chunks(w) for w in (w_in, w_out, w_ple_gate, w_ple_proj))

    scale = HEAD_DIM ** -0.5
    pos_p = jnp.arange(T, dtype=jnp.int32)
    pos_s = jnp.tile(PAST_LEN + jnp.arange(ST, dtype=jnp.int32), SAMPLE_SEQS)
    tab_p = _rope_tables(pos_p, scale) + _rope_tables(pos_p, 1.0)
    tab_s = _rope_tables(pos_s, scale) + _rope_tables(pos_s, 1.0)
    lvl = jnp.asarray(_level_matrix())
    mlow_p = jnp.asarray(_mask_matrices(BLK), dtype=BF16)
    mlow_s = jnp.asarray(_mask_matrices(ST), dtype=BF16)

    ck = cache_k_win.reshape(DEPTH, SB, WINDOW, KV_WIDTH)
    cv = cache_v_win.reshape(DEPTH, SB, WINDOW, KV_WIDTH)

    yp, ys = x_prompt, x_sample
    kp_l, vp_l, sp_l, ks_l, vs_l, ss_l = [], [], [], [], [], []
    for i in range(DEPTH):
        sinks = attn_sinks[i].astype(F32)
        yp, kp, vp, sp = _prompt_layer(i, yp, p_prompt, tab_p, lvl, mlow_p, weights, sinks, v512, v1024)
        ys, kk, vv, ss = _sample_layer(i, ys, p_sample, ck, cv, state_hgrn, tab_s, lvl, mlow_s, weights,
                                       sinks, v512, v1024)
        kp_l.append(kp); vp_l.append(vp); sp_l.append(sp)
        ks_l.append(kk); vs_l.append(vv); ss_l.append(ss)

    def kv5(lst, b):
        return jnp.stack(lst).reshape(DEPTH, b, WINDOW, KV_HEADS, HEAD_DIM)

    return (yp, ys, kv5(kp_l, B), kv5(vp_l, B), jnp.stack(sp_l),
            kv5(ks_l, SB), kv5(vs_l, SB), jnp.stack(ss_l))
```

```python
import math

import numpy as np
import jax
import jax.numpy as jnp
from jax import lax
from jax.experimental import pallas as pl
from jax.experimental.pallas import tpu as pltpu

D_MODEL = 1024
DEPTH = 4
PAST_LEN = 8192
ATTN_HEADS = 8
KV_HEADS = 2
HEAD_DIM = 64
ATTN_WIDTH = ATTN_HEADS * HEAD_DIM
KV_WIDTH = KV_HEADS * HEAD_DIM
WINDOW = 128
ROPE_THETA = 10000.0
HG_HEADS = 4
HG_DK = 128
HG_DV = 128
HG_KW = HG_HEADS * HG_DK
HG_VW = HG_HEADS * HG_DV
MIX_WIDTH = ATTN_WIDTH + HG_VW
IN_COLS = 2 * ATTN_WIDTH + 2 * KV_WIDTH + 2 * HG_KW + 2 * HG_VW
PLE_DIM = 256
DN_ALPHA = (2 * DEPTH) ** 0.25
NORM_EPS = 1e-5
NEG_INF = -1e30

C_Q = 0
C_K = C_Q + ATTN_WIDTH
C_V = C_K + KV_WIDTH
C_GA = C_V + KV_WIDTH
C_HQ = C_GA + ATTN_WIDTH
C_HF = C_HQ + HG_KW
C_HI = C_HF + HG_KW
C_GH = C_HI + HG_VW

LANES = 128
MXU_COLS = 256
BLK = 128
PROMPT_TILE = 512
SAMPLE_SEQS = 16
LOW_LEVELS = 3
PROMPT_LEVELS = 7
VMEM_LIMIT = 56 * 1024 * 1024
IN_CHUNKS = IN_COLS // MXU_COLS
OUT_CHUNKS = D_MODEL // MXU_COLS

F32 = jnp.float32
BF16 = jnp.bfloat16


def _dot(a, b):
    return jnp.dot(a, b, preferred_element_type=F32)


def _dot_nt(a, b):
    return lax.dot_general(a, b, (((1,), (1,)), ((), ())), preferred_element_type=F32)


def _dot_tn(a, b):
    return lax.dot_general(a, b, (((0,), (0,)), ((), ())), preferred_element_type=F32)


def _split_bf16(x):
    hi = x.astype(BF16)
    lo = (x - hi.astype(F32)).astype(BF16)
    return hi, lo


def _sigmoid(x):
    return 0.5 * jnp.tanh(0.5 * x) + 0.5


def _silu(x):
    h = 0.5 * x
    return h * jnp.tanh(h) + h


def _lane_iota(rows):
    return lax.broadcasted_iota(jnp.int32, (rows, LANES), 1)


def _rope(x, cos_t, sin_t):
    lane = _lane_iota(x.shape[0])
    first_half = (lane & 32) == 0
    swapped = jnp.where(first_half, pltpu.roll(x, 96, 1), pltpu.roll(x, 32, 1))
    return x * cos_t + swapped * sin_t


def _kv_variants(a, fill):
    lane = _lane_iota(a.shape[0])
    lo = lane < HEAD_DIM
    sw = pltpu.roll(a, HEAD_DIM, 1)
    f = jnp.full_like(a, fill)
    out = [jnp.where(lo, a, f), jnp.where(lo, f, sw), jnp.where(lo, sw, f), jnp.where(lo, f, a)]
    return [o.astype(BF16) for o in out]


def _softmax_parts(s_cur, s_prev, sink_col):
    m = jnp.maximum(jnp.maximum(jnp.max(s_cur, axis=-1, keepdims=True),
                                jnp.max(s_prev, axis=-1, keepdims=True)), sink_col)
    return jnp.exp(s_cur - m), jnp.exp(s_prev - m), jnp.exp(sink_col - m)


def _sum_half(e, es, rows):
    lane = _lane_iota(rows)
    sum_half = (lane >= HEAD_DIM) if e == 0 else (lane < HEAD_DIM)
    return jnp.where(sum_half, es, 0.0)


def _assemble_attn(res, rows):
    lane = _lane_iota(rows)
    lo = lane < HEAD_DIM
    cols = []
    for c in range(4):
        g, cc = c // 2, c % 2
        r0 = res[g * 2 + 0][cc * rows:(cc + 1) * rows]
        r1 = res[g * 2 + 1][cc * rows:(cc + 1) * rows]
        num = jnp.where(lo, r0, r1)
        den = pltpu.roll(jnp.where(lo, r1, r0), HEAD_DIM, 1)
        cols.append(num / den)
    return cols


def _log_decay(hf, log_lb, log1m_lb):
    ls = jnp.minimum(hf, 0.0) - jnp.log(1.0 + jnp.exp(-jnp.abs(hf)))
    b = log1m_lb + ls
    return jnp.maximum(log_lb, b) + jnp.log(1.0 + jnp.exp(-jnp.abs(log_lb - b)))


def _level_exponent(l, e_low, g_h, upper):
    if l < LOW_LEVELS:
        return e_low[l]
    b, h = 2 << l, 1 << l
    pieces = []
    for i in range(BLK // b):
        r = i * b + h - 1
        pieces.append(jnp.broadcast_to(g_h[r:r + 1, :], (b, LANES)))
    gref = pieces[0] if len(pieces) == 1 else jnp.concatenate(pieces, axis=0)
    d = g_h - gref
    return jnp.where(upper, d, -d)


def _hgrn_scores_steps(q_h, k_h, e_low, g_h, lvl, levels, out):
    rows = lax.broadcasted_iota(jnp.int32, (BLK, LANES), 0)
    a = jnp.zeros((BLK, BLK), F32)
    for l in range(levels):
        upper = ((rows >> l) & 1) == 1
        e = _level_exponent(l, e_low, g_h, upper)
        u = (jnp.where(upper, q_h, k_h) * jnp.exp(e)).astype(BF16)
        a = jnp.where(lvl == l, _dot_nt(u, u), a)
        yield
    diag = jnp.sum(q_h * k_h, axis=-1, keepdims=True)
    out["a"] = jnp.where(lvl == -1, diag, a)


def _hgrn_scores(q_h, k_h, e_low, g_h, lvl, levels):
    out = {}
    for _ in _hgrn_scores_steps(q_h, k_h, e_low, g_h, lvl, levels, out):
        pass
    return out["a"]


def _gated_rmsnorm_cols(cols, gain_row, gate, width):
    ss = None
    for c in cols:
        s = jnp.sum(c * c, axis=-1, keepdims=True)
        ss = s if ss is None else ss + s
    inv = lax.rsqrt(ss * (1.0 / width) + NORM_EPS)
    out = []
    for i, c in enumerate(cols):
        gt = gate[:, i * LANES:(i + 1) * LANES]
        out.append(c * inv * gain_row[:, i * LANES:(i + 1) * LANES] * _silu(gt))
    return out


def _layernorm(hpre, ln_g, ln_b):
    mu = jnp.mean(hpre, axis=-1, keepdims=True)
    cen = hpre - mu
    var = jnp.mean(cen * cen, axis=-1, keepdims=True)
    return cen * lax.rsqrt(var + NORM_EPS) * ln_g + ln_b


def _chunked_dot(a, w_ref, n):
    return jnp.concatenate([_dot(a, w_ref[c]) for c in range(n)], axis=1)


def _interleave(gen, n_yields, thunks):
    thunks = list(thunks)
    total, done, seen = len(thunks), 0, 0
    for _ in gen:
        seen += 1
        want = min(total, (seen * total + n_yields - 1) // n_yields)
        while done < want:
            thunks[done]()
            done += 1
    while done < total:
        thunks[done]()
        done += 1


def _prompt_kernel(sinks_ref, x_ref, p_ref, cq_ref, sq_ref, ck_ref, sk_ref, lvl_ref, mlow_ref,
                   w_in_ref, w_out_ref, w_pg_ref, w_pp_ref, v512_ref, v1024_ref,
                   y_ref, kk_ref, vk_ref, sfin_ref,
                   st_scr, kprev_scr, vprev_scr, z_scr, mix_scr):
    t = pl.program_id(1)
    nblk = PROMPT_TILE // BLK

    @pl.when(t == 0)
    def _():
        st_scr[...] = jnp.zeros_like(st_scr)
        kprev_scr[...] = jnp.zeros_like(kprev_scr)
        vprev_scr[...] = jnp.zeros_like(vprev_scr)

    lvl = lvl_ref[...]
    attn_g = v512_ref[0:1, :]
    log_lb = v512_ref[1:2, :]
    log1m_lb = v512_ref[2:3, :]
    hg_g = v512_ref[3:4, :]

    row = lax.broadcasted_iota(jnp.int32, (2 * BLK, BLK), 0)
    col = lax.broadcasted_iota(jnp.int32, (2 * BLK, BLK), 1)
    qrow = row & (BLK - 1)
    mask_cur = col <= qrow
    mask_prev_any = col > qrow
    first_rows = lax.broadcasted_iota(jnp.int32, (2 * BLK, 1), 0) < BLK
    carry = {}

    def stage_in(j):
        slot, rows_j, cell = j % 2, slice(j * BLK, (j + 1) * BLK), {}

        def chunk(c):
            def run():
                if "xb" not in cell:
                    cell["xb"] = x_ref[rows_j, :].astype(BF16)
                z_scr[slot, :, c * MXU_COLS:(c + 1) * MXU_COLS] = _dot(cell["xb"], w_in_ref[c])
            return run
        return [chunk(c) for c in range(IN_CHUNKS)]

    def stage_out(j):
        slot, rows_j, cell = j % 2, slice(j * BLK, (j + 1) * BLK), {"m": [], "g": []}

        def mix_chunk(c):
            def run():
                cell["m"].append(_dot(mix_scr[slot], w_out_ref[c]))
                if c == OUT_CHUNKS - 1:
                    hpre = DN_ALPHA * x_ref[rows_j, :] + jnp.concatenate(cell.pop("m"), axis=1)
                    h = _layernorm(hpre, v1024_ref[0:1, :], v1024_ref[1:2, :])
                    cell["h"] = h
                    cell["hb"] = h.astype(BF16)
            return run

        def gate_chunk(c):
            def run():
                cell["g"].append(_sigmoid(_dot(cell["hb"], w_pg_ref[c])))
            return run

        def finish():
            pb = p_ref[rows_j, :].astype(BF16)
            for c in range(OUT_CHUNKS):
                cs = slice(c * MXU_COLS, (c + 1) * MXU_COLS)
                y_ref[rows_j, cs] = cell["h"][:, cs] + cell["g"][c] * _dot(pb, w_pp_ref[c])
        return ([mix_chunk(c) for c in range(OUT_CHUNKS)] + [gate_chunk(c) for c in range(OUT_CHUNKS)]
                + [finish])

    mix_yields = 2 + 3 * 4 + 2 + HG_HEADS * (PROMPT_LEVELS + 2)

    def stage_mix(j):
        slot, rows_j = j % 2, slice(j * BLK, (j + 1) * BLK)

        def zc(c0, width):
            return z_scr[slot, :, c0:c0 + width]
        qcols = [_rope(zc(C_Q + c * LANES, LANES), cq_ref[rows_j, :], sq_ref[rows_j, :]).astype(BF16)
                 for c in range(4)]
        k_rot = _rope(zc(C_K, KV_WIDTH), ck_ref[rows_j, :], sk_ref[rows_j, :])
        v_new = zc(C_V, KV_WIDTH)
        yield
        if j == 0:
            k_prev = [kprev_scr[i] for i in range(4)]
            v_prev = [vprev_scr[i] for i in range(4)]
            mask_prev = jnp.logical_and(mask_prev_any, t > 0)
        else:
            k_prev, v_prev = carry["k_var"], carry["v_var"]
            mask_prev = mask_prev_any
        k_var = _kv_variants(k_rot, 0.0)
        v_var = _kv_variants(v_new, 1.0)
        carry["k_var"], carry["v_var"] = k_var, v_var
        if j == nblk - 1:
            for i in range(4):
                kprev_scr[i] = k_var[i]
                vprev_scr[i] = v_var[i]
            kk_ref[...] = k_rot
            vk_ref[...] = v_new
        yield
        res = []
        for g in range(2):
            qst = jnp.concatenate([qcols[2 * g], qcols[2 * g + 1]], axis=0)
            for e in range(2):
                i = g * 2 + e
                s_cur = jnp.where(mask_cur, _dot_nt(qst, k_var[i]), NEG_INF)
                s_prev = jnp.where(mask_prev, _dot_nt(qst, k_prev[i]), NEG_INF)
                yield
                sink_col = jnp.where(first_rows, sinks_ref[4 * g + e], sinks_ref[4 * g + 2 + e])
                p_cur, p_prev, es = _softmax_parts(s_cur, s_prev, sink_col)
                yield
                r = _dot(p_cur.astype(BF16), v_var[i]) + _dot(p_prev.astype(BF16), v_prev[i])
                res.append(r + _sum_half(e, es, 2 * BLK))
                yield
        acols = _assemble_attn(res, BLK)
        acols = _gated_rmsnorm_cols(acols, attn_g, zc(C_GA, ATTN_WIDTH), ATTN_WIDTH)
        for c in range(4):
            mix_scr[slot, :, c * LANES:(c + 1) * LANES] = acols[c].astype(BF16)
        yield
        lf = _log_decay(zc(C_HF, HG_KW), log_lb, log1m_lb)
        kin = 1.0 - jnp.exp(lf)
        lf_hi, lf_lo = _split_bf16(lf)
        eg = _dot(mlow_ref[...], lf_hi) + _dot(mlow_ref[...], lf_lo)
        yield
        for h in range(HG_HEADS):
            hs = slice(h * LANES, (h + 1) * LANES)
            q_h = zc(C_HQ + h * LANES, LANES)
            k_h = kin[:, hs]
            v_h = zc(C_HI + h * LANES, LANES).astype(BF16)
            e_low = [eg[l * BLK:(l + 1) * BLK, hs] for l in range(LOW_LEVELS)]
            g_h = eg[LOW_LEVELS * BLK:(LOW_LEVELS + 1) * BLK, hs]
            out = {}
            yield from _hgrn_scores_steps(q_h, k_h, e_low, g_h, lvl, PROMPT_LEVELS, out)
            st = st_scr[h]
            o_h = (_dot(out["a"].astype(BF16), v_h)
                   + _dot_nt((q_h * jnp.exp(g_h)).astype(BF16), st.astype(BF16)))
            g_last = g_h[BLK - 1:BLK, :]
            kd = (k_h * jnp.exp(g_last - g_h)).astype(BF16)
            st_scr[h] = st * jnp.exp(g_last) + _dot_tn(v_h, kd)
            yield
            gt = zc(C_GH + h * LANES, LANES)
            inv = lax.rsqrt(jnp.mean(o_h * o_h, axis=-1, keepdims=True) + NORM_EPS)
            mix_scr[slot, :, ATTN_WIDTH + h * LANES:ATTN_WIDTH + (h + 1) * LANES] = (
                o_h * inv * hg_g[:, hs] * _silu(gt)).astype(BF16)
            yield

    for th in stage_in(0):
        th()
    for j in range(nblk):
        side = stage_in(j + 1) if j + 1 < nblk else []
        if j >= 1:
            side = side + stage_out(j - 1)
        _interleave(stage_mix(j), mix_yields, side)
    for th in stage_out(nblk - 1):
        th()

    @pl.when(t == pl.num_programs(1) - 1)
    def _():
        for h in range(HG_HEADS):
            sfin_ref[h] = st_scr[h].T


def _sample_kernel(sinks_ref, x_ref, p_ref, ck_ref, cv_ref, s0_ref, cq_ref, sq_ref, ckt_ref, skt_ref,
                   lvl_ref, mlow_ref, seg_ref, w_in_ref, w_out_ref, w_pg_ref, w_pp_ref, v512_ref, v1024_ref,
                   y_ref, nk_ref, nv_ref, ns_ref,
                   y_scr, qm_scr, attn_scr, mix_scr):
    layer = pl.program_id(0)
    blk = pl.program_id(1)
    rows = SAMPLE_SEQS * 8
    yrows = pl.ds(pl.multiple_of(blk * rows, rows), rows)

    @pl.when(layer == 0)
    def _():
        y_scr[yrows, :] = x_ref[...].reshape(rows, D_MODEL)

    x = y_scr[yrows, :]
    z = _chunked_dot(x.astype(BF16), w_in_ref, IN_CHUNKS)
    lvl = lvl_ref[...]
    attn_g = v512_ref[0:1, :]
    log_lb = v512_ref[1:2, :]
    log1m_lb = v512_ref[2:3, :]
    hg_g = v512_ref[3:4, :]

    qcols = [_rope(z[:, C_Q + c * LANES:C_Q + (c + 1) * LANES], cq_ref[...], sq_ref[...]) for c in range(4)]
    k_rot = _rope(z[:, C_K:C_K + KV_WIDTH], ckt_ref[...], skt_ref[...])
    v_new = z[:, C_V:C_V + KV_WIDTH]
    k_new_b = k_rot.astype(BF16)
    v_new_b = v_new.astype(BF16)
    lo2 = _lane_iota(2 * rows) < HEAD_DIM
    for g in range(2):
        xg = jnp.concatenate([qcols[2 * g], qcols[2 * g + 1]], axis=0)
        xsw = pltpu.roll(xg, HEAD_DIM, 1)
        keep = lo2 if g == 0 else jnp.logical_not(lo2)
        for e in range(2):
            qm = jnp.where(keep, xg if e == g else xsw, 0.0)
            for cc in range(2):
                r0 = ((g * 2 + e) * 2 + cc) * 8
                for s in range(SAMPLE_SEQS):
                    qm_scr[s, r0:r0 + 8, :] = qm[cc * rows + s * 8:cc * rows + s * 8 + 8, :]

    lf = _log_decay(z[:, C_HF:C_HF + HG_KW], log_lb, log1m_lb)
    kin = 1.0 - jnp.exp(lf)
    lf_hi, lf_lo = _split_bf16(lf)
    eg = _dot(mlow_ref[...], lf_hi) + _dot(mlow_ref[...], lf_lo)
    dec_t = jnp.exp(_dot_tn(lf_hi, seg_ref[...]) + _dot_tn(lf_lo, seg_ref[...]))
    o_intra, qe, kd, vh = [], [], [], []
    for h in range(HG_HEADS):
        hs = slice(h * LANES, (h + 1) * LANES)
        q_h = z[:, C_HQ + h * LANES:C_HQ + (h + 1) * LANES]
        k_h = kin[:, hs]
        v_h = z[:, C_HI + h * LANES:C_HI + (h + 1) * LANES]
        e_low = [eg[l * rows:(l + 1) * rows, hs] for l in range(LOW_LEVELS)]
        g_h = eg[LOW_LEVELS * rows:(LOW_LEVELS + 1) * rows, hs]
        tot_h = eg[(LOW_LEVELS + 1) * rows:(LOW_LEVELS + 2) * rows, hs]
        a = _hgrn_scores(q_h, k_h, e_low, g_h, lvl, LOW_LEVELS)
        o_intra.append(_dot(a.astype(BF16), v_h.astype(BF16)))
        qe.append(q_h * jnp.exp(g_h))
        kd.append(k_h * jnp.exp(tot_h - g_h))
        vh.append(v_h)

    grp = 4
    grows = grp * 64
    rid = lax.broadcasted_iota(jnp.int32, (grows, 1), 0)
    r8 = (rid >> 3) & 7
    sink_col = jnp.zeros((grows, 1), F32)
    for idx in range(8):
        g, e, cc = idx // 4, (idx // 2) % 2, idx % 2
        sink_col = jnp.where(r8 == idx, sinks_ref[layer * ATTN_HEADS + 4 * g + 2 * cc + e], sink_col)
    row = lax.broadcasted_iota(jnp.int32, (grows, 2 * WINDOW), 0)
    col = lax.broadcasted_iota(jnp.int32, (grows, 2 * WINDOW), 1)
    tok = row & 7
    new_col = col - WINDOW
    mask_cache = jnp.logical_and(col < WINDOW, col > tok)
    lo8 = _lane_iota(8) < HEAD_DIM
    o_inter = [[None] * SAMPLE_SEQS for _ in range(HG_HEADS)]
    for gi in range(SAMPLE_SEQS // grp):
        seqs = range(gi * grp, (gi + 1) * grp)
        kcs = {s: ck_ref[s] for s in seqs}
        vcs = {s: cv_ref[s] for s in seqs}
        sc = [_dot_nt(qm_scr[s].astype(BF16),
                      jnp.concatenate([kcs[s].astype(BF16), k_new_b], axis=0)) for s in seqs]
        for s in seqs:
            nk_ref[s, 0:WINDOW - 8, :] = kcs[s][8:WINDOW, :]
            nk_ref[s, WINDOW - 8:WINDOW, :] = k_rot[s * 8:s * 8 + 8, :]
            nv_ref[s, 0:WINDOW - 8, :] = vcs[s][8:WINDOW, :]
            nv_ref[s, WINDOW - 8:WINDOW, :] = v_new[s * 8:s * 8 + 8, :]
        mask_new = jnp.logical_and(jnp.logical_and(col >= WINDOW, (new_col >> 3) == (row >> 6) + gi * grp),
                                   (new_col & 7) <= tok)
        s_all = jnp.where(jnp.logical_or(mask_cache, mask_new), jnp.concatenate(sc, axis=0), NEG_INF)
        m = jnp.maximum(jnp.max(s_all, axis=-1, keepdims=True), sink_col)
        p = jnp.exp(s_all - m)
        den = jnp.sum(p, axis=-1, keepdims=True) + jnp.exp(sink_col - m)
        pb = p.astype(BF16)
        o_all = jnp.concatenate(
            [_dot(pb[i * 64:(i + 1) * 64, :], jnp.concatenate([vcs[s].astype(BF16), v_new_b], axis=0))
             for i, s in enumerate(seqs)], axis=0) / den
        o_sw = pltpu.roll(o_all, HEAD_DIM, 1)
        for i, s in enumerate(seqs):
            for c in range(4):
                g, cc = c // 2, c % 2
                ra = i * 64 + ((g * 2 + 0) * 2 + cc) * 8
                rb = i * 64 + ((g * 2 + 1) * 2 + cc) * 8
                part0 = (o_all if g == 0 else o_sw)[ra:ra + 8, :]
                part1 = (o_all if g == 1 else o_sw)[rb:rb + 8, :]
                attn_scr[c, s * 8:s * 8 + 8, :] = jnp.where(lo8, part0, part1)
        s0s = {(s, h): s0_ref[s, h] for s in seqs for h in range(HG_HEADS)}
        for s in seqs:
            for h in range(HG_HEADS):
                o_inter[h][s] = _dot(qe[h][s * 8:s * 8 + 8, :].astype(BF16), s0s[(s, h)].astype(BF16))
        upd = {(s, h): _dot_tn(kd[h][s * 8:s * 8 + 8, :].astype(BF16), vh[h][s * 8:s * 8 + 8, :].astype(BF16))
               for s in seqs for h in range(HG_HEADS)}
        for s in seqs:
            for h in range(HG_HEADS):
                dcol = dec_t[h * LANES:(h + 1) * LANES, s * 8:s * 8 + 1]
                ns_ref[s, h] = dcol * s0s[(s, h)] + upd[(s, h)]

    acols = _gated_rmsnorm_cols([attn_scr[c] for c in range(4)], attn_g, z[:, C_GA:C_GA + ATTN_WIDTH],
                                ATTN_WIDTH)
    for c in range(4):
        mix_scr[:, c * LANES:(c + 1) * LANES] = acols[c].astype(BF16)
    for h in range(HG_HEADS):
        hs = slice(h * LANES, (h + 1) * LANES)
        o_h = o_intra[h] + jnp.concatenate(o_inter[h], axis=0)
        gt = z[:, C_GH + h * LANES:C_GH + (h + 1) * LANES]
        inv = lax.rsqrt(jnp.mean(o_h * o_h, axis=-1, keepdims=True) + NORM_EPS)
        mix_scr[:, ATTN_WIDTH + h * LANES:ATTN_WIDTH + (h + 1) * LANES] = (
            o_h * inv * hg_g[:, hs] * _silu(gt)).astype(BF16)

    hpre = DN_ALPHA * x + _chunked_dot(mix_scr[...], w_out_ref, OUT_CHUNKS)
    hn = _layernorm(hpre, v1024_ref[0:1, :], v1024_ref[1:2, :])
    gate = _sigmoid(_chunked_dot(hn.astype(BF16), w_pg_ref, OUT_CHUNKS))
    pp = _chunked_dot(p_ref[...].reshape(rows, PLE_DIM).astype(BF16), w_pp_ref, OUT_CHUNKS)
    y = hn + gate * pp
    y_scr[yrows, :] = y
    y_ref[...] = y.reshape(SAMPLE_SEQS, 8, D_MODEL)


def _level_matrix():
    t = np.arange(BLK)[:, None]
    s = np.arange(BLK)[None, :]
    x = t ^ s
    lv = np.floor(np.log2(np.maximum(x, 1))).astype(np.int32)
    return np.where(t > s, lv, np.where(t == s, -1, -2)).astype(np.int32)


def _level_exponent_matrix(l):
    m = np.zeros((BLK, BLK), np.float32)
    b, h = 2 << l, 1 << l
    for t in range(BLK):
        mid = t - t % b + h
        if t >= mid:
            m[t, mid:t + 1] = 1.0
        else:
            m[t, t + 1:mid] = 1.0
    return m


def _mask_matrices(seq_rows):
    blocks = [_level_exponent_matrix(l) for l in range(LOW_LEVELS)]
    t = np.arange(BLK)[:, None]
    s = np.arange(BLK)[None, :]
    same = (t // seq_rows) == (s // seq_rows)
    blocks.append((same & (s <= t)).astype(np.float32))
    if seq_rows < BLK:
        blocks.append(same.astype(np.float32))
    return np.concatenate(blocks, axis=0)


def _rope_tables(pos, scale):
    half = HEAD_DIM // 2
    inv = jnp.exp(-math.log(ROPE_THETA) * jnp.arange(half, dtype=F32) * 2.0 / HEAD_DIM)
    ang = pos.astype(F32)[:, None] * inv[None, :]
    cos = jnp.cos(ang) * scale
    sin = jnp.sin(ang) * scale
    return jnp.tile(cos, (1, 4)), jnp.concatenate([-sin, sin, -sin, sin], axis=1)


def _column_chunks(w):
    d, k, n = w.shape
    return w.astype(BF16).reshape(d, k, n // MXU_COLS, MXU_COLS).transpose(0, 2, 1, 3)


def _const_spec(shape, layer=None):
    if layer is None:
        return pl.BlockSpec(shape, lambda *_: (0,) * len(shape))
    return pl.BlockSpec((None,) + shape, lambda *_: (layer,) + (0,) * len(shape))


def _weight_specs(i):
    return [
        _const_spec((IN_CHUNKS, D_MODEL, MXU_COLS), i),
        _const_spec((OUT_CHUNKS, MIX_WIDTH, MXU_COLS), i),
        _const_spec((OUT_CHUNKS, D_MODEL, MXU_COLS), i),
        _const_spec((OUT_CHUNKS, PLE_DIM, MXU_COLS), i),
        _const_spec((4, ATTN_WIDTH), i),
        _const_spec((2, D_MODEL), i),
    ]


def _prompt_layer(i, x, p_all, tables, lvl, mlow, weights, sinks, v512, v1024):
    B, T, _ = x.shape
    nt = T // PROMPT_TILE
    tab_spec = pl.BlockSpec((PROMPT_TILE, LANES), lambda b, t: (t, 0))
    in_specs = [
        pl.BlockSpec(memory_space=pltpu.SMEM),
        pl.BlockSpec((None, PROMPT_TILE, D_MODEL), lambda b, t: (b, t, 0)),
        pl.BlockSpec((None, None, PROMPT_TILE, PLE_DIM), lambda b, t: (i, b, t, 0)),
        tab_spec, tab_spec, tab_spec, tab_spec,
        _const_spec((BLK, BLK)),
        _const_spec(mlow.shape),
    ] + _weight_specs(i)
    out_shape = [
        jax.ShapeDtypeStruct((B, T, D_MODEL), F32),
        jax.ShapeDtypeStruct((B, WINDOW, KV_WIDTH), F32),
        jax.ShapeDtypeStruct((B, WINDOW, KV_WIDTH), F32),
        jax.ShapeDtypeStruct((B, HG_HEADS, HG_DK, HG_DV), F32),
    ]
    out_specs = [
        pl.BlockSpec((None, PROMPT_TILE, D_MODEL), lambda b, t: (b, t, 0)),
        pl.BlockSpec((None, WINDOW, KV_WIDTH), lambda b, t: (b, 0, 0)),
        pl.BlockSpec((None, WINDOW, KV_WIDTH), lambda b, t: (b, 0, 0)),
        pl.BlockSpec((None, HG_HEADS, HG_DK, HG_DV), lambda b, t: (b, 0, 0, 0)),
    ]
    scratch = [
        pltpu.VMEM((HG_HEADS, HG_DV, HG_DK), F32),
        pltpu.VMEM((4, BLK, LANES), BF16),
        pltpu.VMEM((4, BLK, LANES), BF16),
        pltpu.VMEM((2, BLK, IN_COLS), F32),
        pltpu.VMEM((2, BLK, MIX_WIDTH), BF16),
    ]
    return pl.pallas_call(
        _prompt_kernel,
        grid=(B, nt),
        in_specs=in_specs,
        out_specs=out_specs,
        out_shape=out_shape,
        scratch_shapes=scratch,
        compiler_params=pltpu.CompilerParams(
            dimension_semantics=("arbitrary", "arbitrary"), vmem_limit_bytes=VMEM_LIMIT),
        name=f"prompt_layer{i}",
    )(sinks, x, p_all, *tables, lvl, mlow, *weights, v512, v1024)


def _sample_call(x, p_all, ck_all, cv_all, s0_all, tables, lvl, mlow, seg, weights, sinks, v512, v1024):
    B, T, _ = x.shape
    rows = SAMPLE_SEQS * T

    def wspec(shape):
        return pl.BlockSpec((None,) + shape, lambda l, b: (l,) + (0,) * len(shape),
                            pipeline_mode=pl.Buffered(1))
    tab_spec = _const_spec((rows, LANES))
    in_specs = [
        pl.BlockSpec(memory_space=pltpu.SMEM),
        pl.BlockSpec((SAMPLE_SEQS, T, D_MODEL), lambda l, b: (jnp.where(l == 0, b, 0), 0, 0)),
        pl.BlockSpec((None, SAMPLE_SEQS, T, PLE_DIM), lambda l, b: (l, b, 0, 0)),
        pl.BlockSpec((None, SAMPLE_SEQS, WINDOW, KV_WIDTH), lambda l, b: (l, b, 0, 0)),
        pl.BlockSpec((None, SAMPLE_SEQS, WINDOW, KV_WIDTH), lambda l, b: (l, b, 0, 0)),
        pl.BlockSpec((None, SAMPLE_SEQS, HG_HEADS, HG_DK, HG_DV), lambda l, b: (l, b, 0, 0, 0)),
        tab_spec, tab_spec, tab_spec, tab_spec,
        _const_spec((BLK, BLK)),
        _const_spec(mlow.shape),
        _const_spec((BLK, LANES)),
        wspec((IN_CHUNKS, D_MODEL, MXU_COLS)),
        wspec((OUT_CHUNKS, MIX_WIDTH, MXU_COLS)),
        wspec((OUT_CHUNKS, D_MODEL, MXU_COLS)),
        wspec((OUT_CHUNKS, PLE_DIM, MXU_COLS)),
        pl.BlockSpec((None, 4, ATTN_WIDTH), lambda l, b: (l, 0, 0)),
        pl.BlockSpec((None, 2, D_MODEL), lambda l, b: (l, 0, 0)),
    ]
    out_shape = [
        jax.ShapeDtypeStruct((B, T, D_MODEL), F32),
        jax.ShapeDtypeStruct((DEPTH, B, WINDOW, KV_WIDTH), F32),
        jax.ShapeDtypeStruct((DEPTH, B, WINDOW, KV_WIDTH), F32),
        jax.ShapeDtypeStruct((DEPTH, B, HG_HEADS, HG_DK, HG_DV), F32),
    ]
    out_specs = [
        pl.BlockSpec((SAMPLE_SEQS, T, D_MODEL), lambda l, b: (jnp.where(l == DEPTH - 1, b, 0), 0, 0)),
        pl.BlockSpec((None, SAMPLE_SEQS, WINDOW, KV_WIDTH), lambda l, b: (l, b, 0, 0)),
        pl.BlockSpec((None, SAMPLE_SEQS, WINDOW, KV_WIDTH), lambda l, b: (l, b, 0, 0)),
        pl.BlockSpec((None, SAMPLE_SEQS, HG_HEADS, HG_DK, HG_DV), lambda l, b: (l, b, 0, 0, 0)),
    ]
    scratch = [
        pltpu.VMEM((B * T, D_MODEL), F32),
        pltpu.VMEM((SAMPLE_SEQS, 64, LANES), F32),
        pltpu.VMEM((4, rows, LANES), F32),
        pltpu.VMEM((rows, MIX_WIDTH), BF16),
    ]
    return pl.pallas_call(
        _sample_kernel,
        grid=(DEPTH, B // SAMPLE_SEQS),
        in_specs=in_specs,
        out_specs=out_specs,
        out_shape=out_shape,
        scratch_shapes=scratch,
        compiler_params=pltpu.CompilerParams(
            dimension_semantics=("arbitrary", "arbitrary"), vmem_limit_bytes=VMEM_LIMIT),
        name="sample_layers",
    )(sinks, x, p_all, ck_all, cv_all, s0_all, *tables, lvl, mlow, seg, *weights, v512, v1024)


def kernel(x_prompt, x_sample, cache_k_win, cache_v_win, state_hgrn, p_prompt, p_sample, w_in, attn_sinks,
           attn_norm_g, hg_lb_logits, hg_norm_g, w_out, ln_g, ln_b, w_ple_proj, w_ple_gate):
    B, T, _ = x_prompt.shape
    SB, ST, _ = x_sample.shape
    assert T % PROMPT_TILE == 0 and SB % SAMPLE_SEQS == 0 and ST == 8 and SAMPLE_SEQS * ST == BLK
    assert cache_k_win.shape[2] == WINDOW

    cs = jnp.cumsum(jax.nn.softmax(hg_lb_logits.astype(F32), axis=0), axis=0)
    lbs = cs - cs[:1]
    v512 = jnp.stack([attn_norm_g.astype(F32), jnp.log(lbs), jnp.log1p(-lbs), hg_norm_g.astype(F32)], axis=1)
    v1024 = jnp.stack([ln_g.astype(F32), ln_b.astype(F32)], axis=1)
    weights = tuple(_column_chunks(w) for w in (w_in, w_out, w_ple_gate, w_ple_proj))
    sinks = attn_sinks.astype(F32)

    scale = HEAD_DIM ** -0.5
    pos_p = jnp.arange(T, dtype=jnp.int32)
    pos_s = jnp.tile(PAST_LEN + jnp.arange(ST, dtype=jnp.int32), SAMPLE_SEQS)
    tab_p = _rope_tables(pos_p, scale) + _rope_tables(pos_p, 1.0)
    tab_s = _rope_tables(pos_s, scale) + _rope_tables(pos_s, 1.0)
    lvl = jnp.asarray(_level_matrix())
    mlow_p = jnp.asarray(_mask_matrices(BLK), dtype=BF16)
    mlow_s = jnp.asarray(_mask_matrices(ST), dtype=BF16)
    seg = jnp.asarray((np.arange(BLK)[:, None] // ST) == (np.arange(LANES)[None, :] // ST), dtype=BF16)

    ck = cache_k_win.reshape(DEPTH, SB, WINDOW, KV_WIDTH)
    cv = cache_v_win.reshape(DEPTH, SB, WINDOW, KV_WIDTH)
    ys, ks, vs, ss = _sample_call(x_sample, p_sample, ck, cv, state_hgrn, tab_s, lvl, mlow_s, seg, weights,
                                  sinks.reshape(DEPTH * ATTN_HEADS), v512, v1024)

    yp = x_prompt
    kp_l, vp_l, sp_l = [], [], []
    for i in range(DEPTH):
        yp, kp, vp, sp = _prompt_layer(i, yp, p_prompt, tab_p, lvl, mlow_p, weights, sinks[i], v512, v1024)
        kp_l.append(kp); vp_l.append(vp); sp_l.append(sp)

    def kv5(a, b):
        return a.reshape(DEPTH, b, WINDOW, KV_HEADS, HEAD_DIM)

    return (yp, ys, kv5(jnp.stack(kp_l), B), kv5(jnp.stack(vp_l), B), jnp.stack(sp_l),
            kv5(ks, SB), kv5(vs, SB), ss)
```

```python
import math

import numpy as np
import jax
import jax.numpy as jnp
from jax import lax
from jax.experimental import pallas as pl
from jax.experimental.pallas import tpu as pltpu

D_MODEL = 1024
DEPTH = 4
PAST_LEN = 8192
ATTN_HEADS = 8
KV_HEADS = 2
HEAD_DIM = 64
ATTN_WIDTH = ATTN_HEADS * HEAD_DIM
KV_WIDTH = KV_HEADS * HEAD_DIM
WINDOW = 128
ROPE_THETA = 10000.0
HG_HEADS = 4
HG_DK = 128
HG_DV = 128
HG_KW = HG_HEADS * HG_DK
HG_VW = HG_HEADS * HG_DV
MIX_WIDTH = ATTN_WIDTH + HG_VW
IN_COLS = 2 * ATTN_WIDTH + 2 * KV_WIDTH + 2 * HG_KW + 2 * HG_VW
PLE_DIM = 256
DN_ALPHA = (2 * DEPTH) ** 0.25
NORM_EPS = 1e-5
NEG_INF = -1e30

C_Q = 0
C_K = C_Q + ATTN_WIDTH
C_V = C_K + KV_WIDTH
C_GA = C_V + KV_WIDTH
C_HQ = C_GA + ATTN_WIDTH
C_HF = C_HQ + HG_KW
C_HI = C_HF + HG_KW
C_GH = C_HI + HG_VW

LANES = 128
MXU_COLS = 256
CHUNK_COLS = 2 * MXU_COLS
BLK = 128
PROMPT_TILE = 512
SAMPLE_SEQS = 16
LOW_LEVELS = 3
PROMPT_LEVELS = 7
VMEM_LIMIT = 56 * 1024 * 1024
IN_CHUNKS = IN_COLS // CHUNK_COLS
IN_TAIL = IN_COLS - IN_CHUNKS * CHUNK_COLS
OUT_CHUNKS = D_MODEL // CHUNK_COLS

F32 = jnp.float32
BF16 = jnp.bfloat16


def _dot(a, b):
    return jnp.dot(a, b, preferred_element_type=F32)


def _dot_nt(a, b):
    return lax.dot_general(a, b, (((1,), (1,)), ((), ())), preferred_element_type=F32)


def _dot_tn(a, b):
    return lax.dot_general(a, b, (((0,), (0,)), ((), ())), preferred_element_type=F32)


def _split_bf16(x):
    hi = x.astype(BF16)
    lo = (x - hi.astype(F32)).astype(BF16)
    return hi, lo


def _sigmoid(x):
    return 0.5 * jnp.tanh(0.5 * x) + 0.5


def _silu(x):
    h = 0.5 * x
    return h * jnp.tanh(h) + h


def _lane_iota(rows):
    return lax.broadcasted_iota(jnp.int32, (rows, LANES), 1)


def _rope(x, cos_t, sin_t):
    lane = _lane_iota(x.shape[0])
    first_half = (lane & 32) == 0
    swapped = jnp.where(first_half, pltpu.roll(x, 96, 1), pltpu.roll(x, 32, 1))
    return x * cos_t + swapped * sin_t


def _kv_variants(a, fill):
    lane = _lane_iota(a.shape[0])
    lo = lane < HEAD_DIM
    sw = pltpu.roll(a, HEAD_DIM, 1)
    f = jnp.full_like(a, fill)
    out = [jnp.where(lo, a, f), jnp.where(lo, f, sw), jnp.where(lo, sw, f), jnp.where(lo, f, a)]
    return [o.astype(BF16) for o in out]


def _softmax_parts(s_cur, s_prev, sink_col):
    m = jnp.maximum(jnp.maximum(jnp.max(s_cur, axis=-1, keepdims=True),
                                jnp.max(s_prev, axis=-1, keepdims=True)), sink_col)
    return jnp.exp(s_cur - m), jnp.exp(s_prev - m), jnp.exp(sink_col - m)


def _sum_half(e, es, rows):
    lane = _lane_iota(rows)
    sum_half = (lane >= HEAD_DIM) if e == 0 else (lane < HEAD_DIM)
    return jnp.where(sum_half, es, 0.0)


def _assemble_attn(res, rows):
    lane = _lane_iota(rows)
    lo = lane < HEAD_DIM
    cols = []
    for c in range(4):
        g, cc = c // 2, c % 2
        r0 = res[g * 2 + 0][cc * rows:(cc + 1) * rows]
        r1 = res[g * 2 + 1][cc * rows:(cc + 1) * rows]
        num = jnp.where(lo, r0, r1)
        den = pltpu.roll(jnp.where(lo, r1, r0), HEAD_DIM, 1)
        cols.append(num / den)
    return cols


def _log_decay(hf, log_lb, log1m_lb):
    ls = jnp.minimum(hf, 0.0) - jnp.log(1.0 + jnp.exp(-jnp.abs(hf)))
    b = log1m_lb + ls
    return jnp.maximum(log_lb, b) + jnp.log(1.0 + jnp.exp(-jnp.abs(log_lb - b)))


def _level_exponent(l, e_low, g_h, upper):
    if l < LOW_LEVELS:
        return e_low[l]
    b, h = 2 << l, 1 << l
    pieces = []
    for i in range(BLK // b):
        r = i * b + h - 1
        pieces.append(jnp.broadcast_to(g_h[r:r + 1, :], (b, LANES)))
    gref = pieces[0] if len(pieces) == 1 else jnp.concatenate(pieces, axis=0)
    d = g_h - gref
    return jnp.where(upper, d, -d)


def _hgrn_scores_steps(q_h, k_h, e_low, g_h, lvl, levels, out):
    rows = lax.broadcasted_iota(jnp.int32, (BLK, LANES), 0)
    a = jnp.zeros((BLK, BLK), F32)
    for l in range(levels):
        upper = ((rows >> l) & 1) == 1
        e = _level_exponent(l, e_low, g_h, upper)
        u = (jnp.where(upper, q_h, k_h) * jnp.exp(e)).astype(BF16)
        a = jnp.where(lvl == l, _dot_nt(u, u), a)
        yield
    diag = jnp.sum(q_h * k_h, axis=-1, keepdims=True)
    out["a"] = jnp.where(lvl == -1, diag, a)


def _hgrn_scores(q_h, k_h, e_low, g_h, lvl, levels):
    out = {}
    for _ in _hgrn_scores_steps(q_h, k_h, e_low, g_h, lvl, levels, out):
        pass
    return out["a"]


def _gated_rmsnorm_cols(cols, gain_row, gate, width):
    ss = None
    for c in cols:
        s = jnp.sum(c * c, axis=-1, keepdims=True)
        ss = s if ss is None else ss + s
    inv = lax.rsqrt(ss * (1.0 / width) + NORM_EPS)
    out = []
    for i, c in enumerate(cols):
        gt = gate[:, i * LANES:(i + 1) * LANES]
        out.append(c * inv * gain_row[:, i * LANES:(i + 1) * LANES] * _silu(gt))
    return out


def _layernorm(hpre, ln_g, ln_b):
    mu = jnp.mean(hpre, axis=-1, keepdims=True)
    cen = hpre - mu
    var = jnp.mean(cen * cen, axis=-1, keepdims=True)
    return cen * lax.rsqrt(var + NORM_EPS) * ln_g + ln_b


def _chunked_dot(a, w_ref, n):
    return jnp.concatenate([_dot(a, w_ref[c]) for c in range(n)], axis=1)


def _interleave(gen, n_yields, thunks):
    thunks = list(thunks)
    total, done, seen = len(thunks), 0, 0
    for _ in gen:
        seen += 1
        want = min(total, (seen * total + n_yields - 1) // n_yields)
        while done < want:
            thunks[done]()
            done += 1
    while done < total:
        thunks[done]()
        done += 1


def _prompt_kernel(sinks_ref, x_ref, xn_ref, p_ref, cq_ref, sq_ref, ck_ref, sk_ref, lvl_ref, mlow_ref,
                   w_in_ref, w_tail_ref, w_out_ref, w_pg_ref, w_pp_ref, v512_ref, v1024_ref,
                   y_ref, kk_ref, vk_ref, sfin_ref,
                   st_scr, kprev_scr, vprev_scr, z_scr, mix_scr):
    t = pl.program_id(1)
    nblk = PROMPT_TILE // BLK

    @pl.when(t == 0)
    def _():
        st_scr[...] = jnp.zeros_like(st_scr)
        kprev_scr[...] = jnp.zeros_like(kprev_scr)
        vprev_scr[...] = jnp.zeros_like(vprev_scr)

    lvl = lvl_ref[...]
    attn_g = v512_ref[0:1, :]
    log_lb = v512_ref[1:2, :]
    log1m_lb = v512_ref[2:3, :]
    hg_g = v512_ref[3:4, :]

    row = lax.broadcasted_iota(jnp.int32, (2 * BLK, BLK), 0)
    col = lax.broadcasted_iota(jnp.int32, (2 * BLK, BLK), 1)
    qrow = row & (BLK - 1)
    mask_cur = col <= qrow
    mask_prev_any = col > qrow
    first_rows = lax.broadcasted_iota(jnp.int32, (2 * BLK, 1), 0) < BLK
    carry = {}

    def stage_in(j):
        slot, rows_j, cell = j % 2, slice(j * BLK, (j + 1) * BLK), {}

        def chunk(c):
            def run():
                if "xb" not in cell:
                    cell["xb"] = (xn_ref[...] if j == nblk else x_ref[rows_j, :]).astype(BF16)
                if c < IN_CHUNKS:
                    z_scr[slot, :, c * CHUNK_COLS:(c + 1) * CHUNK_COLS] = _dot(cell["xb"], w_in_ref[c])
                else:
                    z_scr[slot, :, IN_CHUNKS * CHUNK_COLS:IN_COLS] = _dot(cell["xb"], w_tail_ref[0])
            return run
        return [chunk(c) for c in range(IN_CHUNKS + 1)]

    def stage_out(j):
        slot, rows_j, cell = j % 2, slice(j * BLK, (j + 1) * BLK), {"m": [], "g": []}

        def mix_chunk(c):
            def run():
                cell["m"].append(_dot(mix_scr[slot], w_out_ref[c]))
                if c + 1 == OUT_CHUNKS:
                    hpre = DN_ALPHA * x_ref[rows_j, :] + jnp.concatenate(cell.pop("m"), axis=1)
                    h = _layernorm(hpre, v1024_ref[0:1, :], v1024_ref[1:2, :])
                    cell["h"] = h
                    cell["hb"] = h.astype(BF16)
            return run

        def gate_chunk(c):
            def run():
                cell["g"].append(_sigmoid(_dot(cell["hb"], w_pg_ref[c])))
            return run

        def finish():
            pb = p_ref[rows_j, :].astype(BF16)
            for c in range(OUT_CHUNKS):
                cs = slice(c * CHUNK_COLS, (c + 1) * CHUNK_COLS)
                y_ref[rows_j, cs] = cell["h"][:, cs] + cell["g"][c] * _dot(pb, w_pp_ref[c])
        return ([mix_chunk(c) for c in range(OUT_CHUNKS)] + [gate_chunk(c) for c in range(OUT_CHUNKS)]
                + [finish])

    mix_yields = 2 + 3 * 4 + 2 + HG_HEADS * (PROMPT_LEVELS + 2)

    def stage_mix(j):
        slot, rows_j = j % 2, slice(j * BLK, (j + 1) * BLK)

        def zc(c0, width):
            return z_scr[slot, :, c0:c0 + width]
        qcols = [_rope(zc(C_Q + c * LANES, LANES), cq_ref[rows_j, :], sq_ref[rows_j, :]).astype(BF16)
                 for c in range(4)]
        k_rot = _rope(zc(C_K, KV_WIDTH), ck_ref[rows_j, :], sk_ref[rows_j, :])
        v_new = zc(C_V, KV_WIDTH)
        yield
        if j == 0:
            k_prev = [kprev_scr[i] for i in range(4)]
            v_prev = [vprev_scr[i] for i in range(4)]
            mask_prev = jnp.logical_and(mask_prev_any, t > 0)
        else:
            k_prev, v_prev = carry["k_var"], carry["v_var"]
            mask_prev = mask_prev_any
        k_var = _kv_variants(k_rot, 0.0)
        v_var = _kv_variants(v_new, 1.0)
        carry["k_var"], carry["v_var"] = k_var, v_var
        if j == nblk - 1:
            for i in range(4):
                kprev_scr[i] = k_var[i]
                vprev_scr[i] = v_var[i]
            kk_ref[...] = k_rot
            vk_ref[...] = v_new
        yield
        res = []
        for g in range(2):
            qst = jnp.concatenate([qcols[2 * g], qcols[2 * g + 1]], axis=0)
            for e in range(2):
                i = g * 2 + e
                s_cur = jnp.where(mask_cur, _dot_nt(qst, k_var[i]), NEG_INF)
                s_prev = jnp.where(mask_prev, _dot_nt(qst, k_prev[i]), NEG_INF)
                yield
                sink_col = jnp.where(first_rows, sinks_ref[4 * g + e], sinks_ref[4 * g + 2 + e])
                p_cur, p_prev, es = _softmax_parts(s_cur, s_prev, sink_col)
                yield
                r = _dot(p_cur.astype(BF16), v_var[i]) + _dot(p_prev.astype(BF16), v_prev[i])
                res.append(r + _sum_half(e, es, 2 * BLK))
                yield
        acols = _assemble_attn(res, BLK)
        acols = _gated_rmsnorm_cols(acols, attn_g, zc(C_GA, ATTN_WIDTH), ATTN_WIDTH)
        for c in range(4):
            mix_scr[slot, :, c * LANES:(c + 1) * LANES] = acols[c].astype(BF16)
        yield
        lf = _log_decay(zc(C_HF, HG_KW), log_lb, log1m_lb)
        kin = 1.0 - jnp.exp(lf)
        lf_hi, lf_lo = _split_bf16(lf)
        m_low = mlow_ref[0:LOW_LEVELS * BLK, :]
        m_cum = mlow_ref[LOW_LEVELS * BLK:(LOW_LEVELS + 1) * BLK, :]
        eg = jnp.concatenate([_dot(m_low, lf_hi), _dot(m_cum, lf_hi) + _dot(m_cum, lf_lo)], axis=0)
        yield
        for h in range(HG_HEADS):
            hs = slice(h * LANES, (h + 1) * LANES)
            q_h = zc(C_HQ + h * LANES, LANES)
            k_h = kin[:, hs]
            v_h = zc(C_HI + h * LANES, LANES).astype(BF16)
            e_low = [eg[l * BLK:(l + 1) * BLK, hs] for l in range(LOW_LEVELS)]
            g_h = eg[LOW_LEVELS * BLK:(LOW_LEVELS + 1) * BLK, hs]
            out = {}
            yield from _hgrn_scores_steps(q_h, k_h, e_low, g_h, lvl, PROMPT_LEVELS, out)
            st = st_scr[h]
            o_h = (_dot(out["a"].astype(BF16), v_h)
                   + _dot_nt((q_h * jnp.exp(g_h)).astype(BF16), st.astype(BF16)))
            g_last = g_h[BLK - 1:BLK, :]
            kd = (k_h * jnp.exp(g_last - g_h)).astype(BF16)
            st_scr[h] = st * jnp.exp(g_last) + _dot_tn(v_h, kd)
            yield
            gt = zc(C_GH + h * LANES, LANES)
            inv = lax.rsqrt(jnp.mean(o_h * o_h, axis=-1, keepdims=True) + NORM_EPS)
            mix_scr[slot, :, ATTN_WIDTH + h * LANES:ATTN_WIDTH + (h + 1) * LANES] = (
                o_h * inv * hg_g[:, hs] * _silu(gt)).astype(BF16)
            yield

    @pl.when(jnp.logical_and(pl.program_id(0) == 0, t == 0))
    def _():
        for th in stage_in(0):
            th()

    for j in range(nblk):
        side = stage_in(j + 1)
        if j >= 1:
            side = side + stage_out(j - 1)
        _interleave(stage_mix(j), mix_yields, side)
    for th in stage_out(nblk - 1):
        th()

    @pl.when(t == pl.num_programs(1) - 1)
    def _():
        for h in range(HG_HEADS):
            sfin_ref[h] = st_scr[h].T


def _sample_kernel(sinks_ref, x_ref, p_ref, ck_ref, cv_ref, s0_ref, cq_ref, sq_ref, ckt_ref, skt_ref,
                   lvl_ref, mlow_ref, seg_ref, w_in_ref, w_tail_ref, w_out_ref, w_pg_ref, w_pp_ref, v512_ref,
                   v1024_ref,
                   y_ref, nk_ref, nv_ref, ns_ref,
                   y_scr, qm_scr, attn_scr, mix_scr):
    layer = pl.program_id(0)
    blk = pl.program_id(1)
    rows = SAMPLE_SEQS * 8
    yrows = pl.ds(pl.multiple_of(blk * rows, rows), rows)

    @pl.when(layer == 0)
    def _():
        y_scr[yrows, :] = x_ref[...].reshape(rows, D_MODEL)

    x = y_scr[yrows, :]
    xb = x.astype(BF16)
    z = jnp.concatenate([_chunked_dot(xb, w_in_ref, IN_CHUNKS), _dot(xb, w_tail_ref[0])], axis=1)
    lvl = lvl_ref[...]
    attn_g = v512_ref[0:1, :]
    log_lb = v512_ref[1:2, :]
    log1m_lb = v512_ref[2:3, :]
    hg_g = v512_ref[3:4, :]

    qcols = [_rope(z[:, C_Q + c * LANES:C_Q + (c + 1) * LANES], cq_ref[...], sq_ref[...]) for c in range(4)]
    k_rot = _rope(z[:, C_K:C_K + KV_WIDTH], ckt_ref[...], skt_ref[...])
    v_new = z[:, C_V:C_V + KV_WIDTH]
    k_new_b = k_rot.astype(BF16)
    v_new_b = v_new.astype(BF16)
    lo2 = _lane_iota(2 * rows) < HEAD_DIM
    for g in range(2):
        xg = jnp.concatenate([qcols[2 * g], qcols[2 * g + 1]], axis=0)
        xsw = pltpu.roll(xg, HEAD_DIM, 1)
        keep = lo2 if g == 0 else jnp.logical_not(lo2)
        for e in range(2):
            qm = jnp.where(keep, xg if e == g else xsw, 0.0)
            for cc in range(2):
                r0 = ((g * 2 + e) * 2 + cc) * 8
                for s in range(SAMPLE_SEQS):
                    qm_scr[s, r0:r0 + 8, :] = qm[cc * rows + s * 8:cc * rows + s * 8 + 8, :]

    lf = _log_decay(z[:, C_HF:C_HF + HG_KW], log_lb, log1m_lb)
    kin = 1.0 - jnp.exp(lf)
    lf_hi, lf_lo = _split_bf16(lf)
    eg = _dot(mlow_ref[...], lf_hi) + _dot(mlow_ref[...], lf_lo)
    dec_t = jnp.exp(_dot_tn(lf_hi, seg_ref[...]) + _dot_tn(lf_lo, seg_ref[...]))
    o_intra, qe, kd, vh = [], [], [], []
    for h in range(HG_HEADS):
        hs = slice(h * LANES, (h + 1) * LANES)
        q_h = z[:, C_HQ + h * LANES:C_HQ + (h + 1) * LANES]
        k_h = kin[:, hs]
        v_h = z[:, C_HI + h * LANES:C_HI + (h + 1) * LANES]
        e_low = [eg[l * rows:(l + 1) * rows, hs] for l in range(LOW_LEVELS)]
        g_h = eg[LOW_LEVELS * rows:(LOW_LEVELS + 1) * rows, hs]
        tot_h = eg[(LOW_LEVELS + 1) * rows:(LOW_LEVELS + 2) * rows, hs]
        a = _hgrn_scores(q_h, k_h, e_low, g_h, lvl, LOW_LEVELS)
        o_intra.append(_dot(a.astype(BF16), v_h.astype(BF16)))
        qe.append(q_h * jnp.exp(g_h))
        kd.append(k_h * jnp.exp(tot_h - g_h))
        vh.append(v_h)

    grp = 4
    grows = grp * 64
    rid = lax.broadcasted_iota(jnp.int32, (grows, 1), 0)
    r8 = (rid >> 3) & 7
    sink_col = jnp.zeros((grows, 1), F32)
    for idx in range(8):
        g, e, cc = idx // 4, (idx // 2) % 2, idx % 2
        sink_col = jnp.where(r8 == idx, sinks_ref[layer * ATTN_HEADS + 4 * g + 2 * cc + e], sink_col)
    row = lax.broadcasted_iota(jnp.int32, (grows, 2 * WINDOW), 0)
    col = lax.broadcasted_iota(jnp.int32, (grows, 2 * WINDOW), 1)
    tok = row & 7
    new_col = col - WINDOW
    mask_cache = jnp.logical_and(col < WINDOW, col > tok)
    lo8 = _lane_iota(8) < HEAD_DIM
    o_inter = [[None] * SAMPLE_SEQS for _ in range(HG_HEADS)]
    for gi in range(SAMPLE_SEQS // grp):
        seqs = range(gi * grp, (gi + 1) * grp)
        kcs = {s: ck_ref[s] for s in seqs}
        vcs = {s: cv_ref[s] for s in seqs}
        sc = [_dot_nt(qm_scr[s].astype(BF16),
                      jnp.concatenate([kcs[s].astype(BF16), k_new_b], axis=0)) for s in seqs]
        for s in seqs:
            nk_ref[s, 0:WINDOW - 8, :] = kcs[s][8:WINDOW, :]
            nk_ref[s, WINDOW - 8:WINDOW, :] = k_rot[s * 8:s * 8 + 8, :]
            nv_ref[s, 0:WINDOW - 8, :] = vcs[s][8:WINDOW, :]
            nv_ref[s, WINDOW - 8:WINDOW, :] = v_new[s * 8:s * 8 + 8, :]
        mask_new = jnp.logical_and(jnp.logical_and(col >= WINDOW, (new_col >> 3) == (row >> 6) + gi * grp),
                                   (new_col & 7) <= tok)
        s_all = jnp.where(jnp.logical_or(mask_cache, mask_new), jnp.concatenate(sc, axis=0), NEG_INF)
        m = jnp.maximum(jnp.max(s_all, axis=-1, keepdims=True), sink_col)
        p = jnp.exp(s_all - m)
        den = jnp.sum(p, axis=-1, keepdims=True) + jnp.exp(sink_col - m)
        pb = p.astype(BF16)
        o_all = jnp.concatenate(
            [_dot(pb[i * 64:(i + 1) * 64, :], jnp.concatenate([vcs[s].astype(BF16), v_new_b], axis=0))
             for i, s in enumerate(seqs)], axis=0) / den
        o_sw = pltpu.roll(o_all, HEAD_DIM, 1)
        for i, s in enumerate(seqs):
            for c in range(4):
                g, cc = c // 2, c % 2
                ra = i * 64 + ((g * 2 + 0) * 2 + cc) * 8
                rb = i * 64 + ((g * 2 + 1) * 2 + cc) * 8
                part0 = (o_all if g == 0 else o_sw)[ra:ra + 8, :]
                part1 = (o_all if g == 1 else o_sw)[rb:rb + 8, :]
                attn_scr[c, s * 8:s * 8 + 8, :] = jnp.where(lo8, part0, part1)
        s0s = {(s, h): s0_ref[s, h] for s in seqs for h in range(HG_HEADS)}
        for s in seqs:
            for h in range(HG_HEADS):
                o_inter[h][s] = _dot(qe[h][s * 8:s * 8 + 8, :].astype(BF16), s0s[(s, h)].astype(BF16))
        upd = {(s, h): _dot_tn(kd[h][s * 8:s * 8 + 8, :].astype(BF16), vh[h][s * 8:s * 8 + 8, :].astype(BF16))
               for s in seqs for h in range(HG_HEADS)}
        for s in seqs:
            for h in range(HG_HEADS):
                dcol = dec_t[h * LANES:(h + 1) * LANES, s * 8:s * 8 + 1]
                ns_ref[s, h] = dcol * s0s[(s, h)] + upd[(s, h)]

    acols = _gated_rmsnorm_cols([attn_scr[c] for c in range(4)], attn_g, z[:, C_GA:C_GA + ATTN_WIDTH],
                                ATTN_WIDTH)
    for c in range(4):
        mix_scr[:, c * LANES:(c + 1) * LANES] = acols[c].astype(BF16)
    for h in range(HG_HEADS):
        hs = slice(h * LANES, (h + 1) * LANES)
        o_h = o_intra[h] + jnp.concatenate(o_inter[h], axis=0)
        gt = z[:, C_GH + h * LANES:C_GH + (h + 1) * LANES]
        inv = lax.rsqrt(jnp.mean(o_h * o_h, axis=-1, keepdims=True) + NORM_EPS)
        mix_scr[:, ATTN_WIDTH + h * LANES:ATTN_WIDTH + (h + 1) * LANES] = (
            o_h * inv * hg_g[:, hs] * _silu(gt)).astype(BF16)

    hpre = DN_ALPHA * x + _chunked_dot(mix_scr[...], w_out_ref, OUT_CHUNKS)
    hn = _layernorm(hpre, v1024_ref[0:1, :], v1024_ref[1:2, :])
    gate = _sigmoid(_chunked_dot(hn.astype(BF16), w_pg_ref, OUT_CHUNKS))
    pp = _chunked_dot(p_ref[...].reshape(rows, PLE_DIM).astype(BF16), w_pp_ref, OUT_CHUNKS)
    y = hn + gate * pp
    y_scr[yrows, :] = y
    y_ref[...] = y.reshape(SAMPLE_SEQS, 8, D_MODEL)


def _level_matrix():
    t = np.arange(BLK)[:, None]
    s = np.arange(BLK)[None, :]
    x = t ^ s
    lv = np.floor(np.log2(np.maximum(x, 1))).astype(np.int32)
    return np.where(t > s, lv, np.where(t == s, -1, -2)).astype(np.int32)


def _level_exponent_matrix(l):
    m = np.zeros((BLK, BLK), np.float32)
    b, h = 2 << l, 1 << l
    for t in range(BLK):
        mid = t - t % b + h
        if t >= mid:
            m[t, mid:t + 1] = 1.0
        else:
            m[t, t + 1:mid] = 1.0
    return m


def _mask_matrices(seq_rows):
    blocks = [_level_exponent_matrix(l) for l in range(LOW_LEVELS)]
    t = np.arange(BLK)[:, None]
    s = np.arange(BLK)[None, :]
    same = (t // seq_rows) == (s // seq_rows)
    blocks.append((same & (s <= t)).astype(np.float32))
    if seq_rows < BLK:
        blocks.append(same.astype(np.float32))
    return np.concatenate(blocks, axis=0)


def _rope_tables(pos, scale):
    half = HEAD_DIM // 2
    inv = jnp.exp(-math.log(ROPE_THETA) * jnp.arange(half, dtype=F32) * 2.0 / HEAD_DIM)
    ang = pos.astype(F32)[:, None] * inv[None, :]
    cos = jnp.cos(ang) * scale
    sin = jnp.sin(ang) * scale
    return jnp.tile(cos, (1, 4)), jnp.concatenate([-sin, sin, -sin, sin], axis=1)


def _column_chunks(w, width):
    d, k, n = w.shape
    return w.astype(BF16).reshape(d, k, n // width, width).transpose(0, 2, 1, 3)


def _const_spec(shape, layer=None):
    if layer is None:
        return pl.BlockSpec(shape, lambda *_: (0,) * len(shape))
    return pl.BlockSpec((None,) + shape, lambda *_: (layer,) + (0,) * len(shape))


def _weight_specs(i):
    return [
        _const_spec((IN_CHUNKS, D_MODEL, CHUNK_COLS), i),
        _const_spec((1, D_MODEL, IN_TAIL), i),
        _const_spec((OUT_CHUNKS, MIX_WIDTH, CHUNK_COLS), i),
        _const_spec((OUT_CHUNKS, D_MODEL, CHUNK_COLS), i),
        _const_spec((OUT_CHUNKS, PLE_DIM, CHUNK_COLS), i),
        _const_spec((4, ATTN_WIDTH), i),
        _const_spec((2, D_MODEL), i),
    ]


def _prompt_layer(i, x, p_all, tables, lvl, mlow, weights, sinks, v512, v1024):
    B, T, _ = x.shape
    nt = T // PROMPT_TILE
    tab_spec = pl.BlockSpec((PROMPT_TILE, LANES), lambda b, t: (t, 0))
    nblk = PROMPT_TILE // BLK
    assert nblk % 2 == 0

    def next_first_block(b, t):
        flat = jnp.minimum(b * nt + t + 1, B * nt - 1)
        return (flat // nt, (flat % nt) * nblk, 0)
    in_specs = [
        pl.BlockSpec(memory_space=pltpu.SMEM),
        pl.BlockSpec((None, PROMPT_TILE, D_MODEL), lambda b, t: (b, t, 0)),
        pl.BlockSpec((None, BLK, D_MODEL), next_first_block),
        pl.BlockSpec((None, None, PROMPT_TILE, PLE_DIM), lambda b, t: (i, b, t, 0)),
        tab_spec, tab_spec, tab_spec, tab_spec,
        _const_spec((BLK, BLK)),
        _const_spec(mlow.shape),
    ] + _weight_specs(i)
    out_shape = [
        jax.ShapeDtypeStruct((B, T, D_MODEL), F32),
        jax.ShapeDtypeStruct((B, WINDOW, KV_WIDTH), F32),
        jax.ShapeDtypeStruct((B, WINDOW, KV_WIDTH), F32),
        jax.ShapeDtypeStruct((B, HG_HEADS, HG_DK, HG_DV), F32),
    ]
    out_specs = [
        pl.BlockSpec((None, PROMPT_TILE, D_MODEL), lambda b, t: (b, t, 0)),
        pl.BlockSpec((None, WINDOW, KV_WIDTH), lambda b, t: (b, 0, 0)),
        pl.BlockSpec((None, WINDOW, KV_WIDTH), lambda b, t: (b, 0, 0)),
        pl.BlockSpec((None, HG_HEADS, HG_DK, HG_DV), lambda b, t: (b, 0, 0, 0)),
    ]
    scratch = [
        pltpu.VMEM((HG_HEADS, HG_DV, HG_DK), F32),
        pltpu.VMEM((4, BLK, LANES), BF16),
        pltpu.VMEM((4, BLK, LANES), BF16),
        pltpu.VMEM((2, BLK, IN_COLS), F32),
        pltpu.VMEM((2, BLK, MIX_WIDTH), BF16),
    ]
    return pl.pallas_call(
        _prompt_kernel,
        grid=(B, nt),
        in_specs=in_specs,
        out_specs=out_specs,
        out_shape=out_shape,
        scratch_shapes=scratch,
        compiler_params=pltpu.CompilerParams(
            dimension_semantics=("arbitrary", "arbitrary"), vmem_limit_bytes=VMEM_LIMIT),
        name=f"prompt_layer{i}",
    )(sinks, x, x, p_all, *tables, lvl, mlow, *weights, v512, v1024)


def _sample_call(x, p_all, ck_all, cv_all, s0_all, tables, lvl, mlow, seg, weights, sinks, v512, v1024):
    B, T, _ = x.shape
    rows = SAMPLE_SEQS * T

    def wspec(shape):
        return pl.BlockSpec((None,) + shape, lambda l, b: (l,) + (0,) * len(shape),
                            pipeline_mode=pl.Buffered(1))
    tab_spec = _const_spec((rows, LANES))
    in_specs = [
        pl.BlockSpec(memory_space=pltpu.SMEM),
        pl.BlockSpec((SAMPLE_SEQS, T, D_MODEL), lambda l, b: (jnp.where(l == 0, b, 0), 0, 0)),
        pl.BlockSpec((None, SAMPLE_SEQS, T, PLE_DIM), lambda l, b: (l, b, 0, 0)),
        pl.BlockSpec((None, SAMPLE_SEQS, WINDOW, KV_WIDTH), lambda l, b: (l, b, 0, 0)),
        pl.BlockSpec((None, SAMPLE_SEQS, WINDOW, KV_WIDTH), lambda l, b: (l, b, 0, 0)),
        pl.BlockSpec((None, SAMPLE_SEQS, HG_HEADS, HG_DK, HG_DV), lambda l, b: (l, b, 0, 0, 0)),
        tab_spec, tab_spec, tab_spec, tab_spec,
        _const_spec((BLK, BLK)),
        _const_spec(mlow.shape),
        _const_spec((BLK, LANES)),
        wspec((IN_CHUNKS, D_MODEL, CHUNK_COLS)),
        wspec((1, D_MODEL, IN_TAIL)),
        wspec((OUT_CHUNKS, MIX_WIDTH, CHUNK_COLS)),
        wspec((OUT_CHUNKS, D_MODEL, CHUNK_COLS)),
        wspec((OUT_CHUNKS, PLE_DIM, CHUNK_COLS)),
        pl.BlockSpec((None, 4, ATTN_WIDTH), lambda l, b: (l, 0, 0)),
        pl.BlockSpec((None, 2, D_MODEL), lambda l, b: (l, 0, 0)),
    ]
    out_shape = [
        jax.ShapeDtypeStruct((B, T, D_MODEL), F32),
        jax.ShapeDtypeStruct((DEPTH, B, WINDOW, KV_WIDTH), F32),
        jax.ShapeDtypeStruct((DEPTH, B, WINDOW, KV_WIDTH), F32),
        jax.ShapeDtypeStruct((DEPTH, B, HG_HEADS, HG_DK, HG_DV), F32),
    ]
    out_specs = [
        pl.BlockSpec((SAMPLE_SEQS, T, D_MODEL), lambda l, b: (jnp.where(l == DEPTH - 1, b, 0), 0, 0)),
        pl.BlockSpec((None, SAMPLE_SEQS, WINDOW, KV_WIDTH), lambda l, b: (l, b, 0, 0)),
        pl.BlockSpec((None, SAMPLE_SEQS, WINDOW, KV_WIDTH), lambda l, b: (l, b, 0, 0)),
        pl.BlockSpec((None, SAMPLE_SEQS, HG_HEADS, HG_DK, HG_DV), lambda l, b: (l, b, 0, 0, 0)),
    ]
    scratch = [
        pltpu.VMEM((B * T, D_MODEL), F32),
        pltpu.VMEM((SAMPLE_SEQS, 64, LANES), F32),
        pltpu.VMEM((4, rows, LANES), F32),
        pltpu.VMEM((rows, MIX_WIDTH), BF16),
    ]
    return pl.pallas_call(
        _sample_kernel,
        grid=(DEPTH, B // SAMPLE_SEQS),
        in_specs=in_specs,
        out_specs=out_specs,
        out_shape=out_shape,
        scratch_shapes=scratch,
        compiler_params=pltpu.CompilerParams(
            dimension_semantics=("arbitrary", "arbitrary"), vmem_limit_bytes=VMEM_LIMIT),
        name="sample_layers",
    )(sinks, x, p_all, ck_all, cv_all, s0_all, *tables, lvl, mlow, seg, *weights, v512, v1024)


def kernel(x_prompt, x_sample, cache_k_win, cache_v_win, state_hgrn, p_prompt, p_sample, w_in, attn_sinks,
           attn_norm_g, hg_lb_logits, hg_norm_g, w_out, ln_g, ln_b, w_ple_proj, w_ple_gate):
    B, T, _ = x_prompt.shape
    SB, ST, _ = x_sample.shape
    assert T % PROMPT_TILE == 0 and SB % SAMPLE_SEQS == 0 and ST == 8 and SAMPLE_SEQS * ST == BLK
    assert cache_k_win.shape[2] == WINDOW

    cs = jnp.cumsum(jax.nn.softmax(hg_lb_logits.astype(F32), axis=0), axis=0)
    lbs = cs - cs[:1]
    v512 = jnp.stack([attn_norm_g.astype(F32), jnp.log(lbs), jnp.log1p(-lbs), hg_norm_g.astype(F32)], axis=1)
    v1024 = jnp.stack([ln_g.astype(F32), ln_b.astype(F32)], axis=1)
    n_main = IN_CHUNKS * CHUNK_COLS
    weights = (_column_chunks(w_in[:, :, :n_main], CHUNK_COLS), _column_chunks(w_in[:, :, n_main:], IN_TAIL),
               _column_chunks(w_out, CHUNK_COLS), _column_chunks(w_ple_gate, CHUNK_COLS),
               _column_chunks(w_ple_proj, CHUNK_COLS))
    sinks = attn_sinks.astype(F32)

    scale = HEAD_DIM ** -0.5
    pos_p = jnp.arange(T, dtype=jnp.int32)
    pos_s = jnp.tile(PAST_LEN + jnp.arange(ST, dtype=jnp.int32), SAMPLE_SEQS)
    tab_p = _rope_tables(pos_p, scale) + _rope_tables(pos_p, 1.0)
    tab_s = _rope_tables(pos_s, scale) + _rope_tables(pos_s, 1.0)
    lvl = jnp.asarray(_level_matrix())
    mlow_p = jnp.asarray(_mask_matrices(BLK), dtype=BF16)
    mlow_s = jnp.asarray(_mask_matrices(ST), dtype=BF16)
    seg = jnp.asarray((np.arange(BLK)[:, None] // ST) == (np.arange(LANES)[None, :] // ST), dtype=BF16)

    ck = cache_k_win.reshape(DEPTH, SB, WINDOW, KV_WIDTH)
    cv = cache_v_win.reshape(DEPTH, SB, WINDOW, KV_WIDTH)
    ys, ks, vs, ss = _sample_call(x_sample, p_sample, ck, cv, state_hgrn, tab_s, lvl, mlow_s, seg, weights,
                                  sinks.reshape(DEPTH * ATTN_HEADS), v512, v1024)

    yp = x_prompt
    kp_l, vp_l, sp_l = [], [], []
    for i in range(DEPTH):
        yp, kp, vp, sp = _prompt_layer(i, yp, p_prompt, tab_p, lvl, mlow_p, weights, sinks[i], v512, v1024)
        kp_l.append(kp); vp_l.append(vp); sp_l.append(sp)

    def kv5(a, b):
        return a.reshape(DEPTH, b, WINDOW, KV_HEADS, HEAD_DIM)

    return (yp, ys, kv5(jnp.stack(kp_l), B), kv5(jnp.stack(vp_l), B), jnp.stack(sp_l),
            kv5(ks, SB), kv5(vs, SB), ss)
```

```python
import math

import numpy as np
import jax
import jax.numpy as jnp
from jax import lax
from jax.experimental import pallas as pl
from jax.experimental.pallas import tpu as pltpu

D_MODEL = 1024
DEPTH = 4
PAST_LEN = 8192
ATTN_HEADS = 8
KV_HEADS = 2
HEAD_DIM = 64
ATTN_WIDTH = ATTN_HEADS * HEAD_DIM
KV_WIDTH = KV_HEADS * HEAD_DIM
WINDOW = 128
ROPE_THETA = 10000.0
HG_HEADS = 4
HG_DK = 128
HG_DV = 128
HG_KW = HG_HEADS * HG_DK
HG_VW = HG_HEADS * HG_DV
MIX_WIDTH = ATTN_WIDTH + HG_VW
IN_COLS = 2 * ATTN_WIDTH + 2 * KV_WIDTH + 2 * HG_KW + 2 * HG_VW
PLE_DIM = 256
DN_ALPHA = (2 * DEPTH) ** 0.25
NORM_EPS = 1e-5
NEG_INF = -1e30

C_Q = 0
C_K = C_Q + ATTN_WIDTH
C_V = C_K + KV_WIDTH
C_GA = C_V + KV_WIDTH
C_HQ = C_GA + ATTN_WIDTH
C_HF = C_HQ + HG_KW
C_HI = C_HF + HG_KW
C_GH = C_HI + HG_VW

LANES = 128
MXU_COLS = 256
CHUNK_COLS = 2 * MXU_COLS
BLK = 128
PROMPT_TILE = 512
PROMPT_SEQS = 2
SAMPLE_SEQS = 16
LOW_LEVELS = 3
PROMPT_LEVELS = 7
VMEM_LIMIT = 56 * 1024 * 1024
IN_CHUNKS = IN_COLS // CHUNK_COLS
IN_TAIL = IN_COLS - IN_CHUNKS * CHUNK_COLS
OUT_CHUNKS = D_MODEL // CHUNK_COLS

F32 = jnp.float32
BF16 = jnp.bfloat16


def _dot(a, b):
    return jnp.dot(a, b, preferred_element_type=F32)


def _dot_nt(a, b):
    return lax.dot_general(a, b, (((1,), (1,)), ((), ())), preferred_element_type=F32)


def _dot_tn(a, b):
    return lax.dot_general(a, b, (((0,), (0,)), ((), ())), preferred_element_type=F32)


def _split_bf16(x):
    hi = x.astype(BF16)
    lo = (x - hi.astype(F32)).astype(BF16)
    return hi, lo


def _sigmoid(x):
    return 0.5 * jnp.tanh(0.5 * x) + 0.5


def _silu(x):
    h = 0.5 * x
    return h * jnp.tanh(h) + h


def _lane_iota(rows):
    return lax.broadcasted_iota(jnp.int32, (rows, LANES), 1)


def _rope(x, cos_t, sin_t):
    lane = _lane_iota(x.shape[0])
    first_half = (lane & 32) == 0
    swapped = jnp.where(first_half, pltpu.roll(x, 96, 1), pltpu.roll(x, 32, 1))
    return x * cos_t + swapped * sin_t


def _kv_variants(a, fill):
    lane = _lane_iota(a.shape[0])
    lo = lane < HEAD_DIM
    sw = pltpu.roll(a, HEAD_DIM, 1)
    f = jnp.full_like(a, fill)
    out = [jnp.where(lo, a, f), jnp.where(lo, f, sw), jnp.where(lo, sw, f), jnp.where(lo, f, a)]
    return [o.astype(BF16) for o in out]


def _softmax_parts(s_cur, s_prev, sink_col):
    m = jnp.maximum(jnp.maximum(jnp.max(s_cur, axis=-1, keepdims=True),
                                jnp.max(s_prev, axis=-1, keepdims=True)), sink_col)
    return jnp.exp(s_cur - m), jnp.exp(s_prev - m), jnp.exp(sink_col - m)


def _sum_half(e, es, rows):
    lane = _lane_iota(rows)
    sum_half = (lane >= HEAD_DIM) if e == 0 else (lane < HEAD_DIM)
    return jnp.where(sum_half, es, 0.0)


def _assemble_attn(res, rows):
    lane = _lane_iota(rows)
    lo = lane < HEAD_DIM
    cols = []
    for c in range(4):
        g, cc = c // 2, c % 2
        r0 = res[g * 2 + 0][cc * rows:(cc + 1) * rows]
        r1 = res[g * 2 + 1][cc * rows:(cc + 1) * rows]
        num = jnp.where(lo, r0, r1)
        den = pltpu.roll(jnp.where(lo, r1, r0), HEAD_DIM, 1)
        cols.append(num / den)
    return cols


def _log_decay(hf, log_lb, log1m_lb):
    ls = jnp.minimum(hf, 0.0) - jnp.log(1.0 + jnp.exp(-jnp.abs(hf)))
    b = log1m_lb + ls
    return jnp.maximum(log_lb, b) + jnp.log(1.0 + jnp.exp(-jnp.abs(log_lb - b)))


def _level_exponent(l, e_low, g_h, upper):
    if l < LOW_LEVELS:
        return e_low[l]
    b, h = 2 << l, 1 << l
    pieces = []
    for i in range(BLK // b):
        r = i * b + h - 1
        pieces.append(jnp.broadcast_to(g_h[r:r + 1, :], (b, LANES)))
    gref = pieces[0] if len(pieces) == 1 else jnp.concatenate(pieces, axis=0)
    d = g_h - gref
    return jnp.where(upper, d, -d)


def _hgrn_scores_steps(q_h, k_h, e_low, g_h, lvl, levels, out):
    rows = lax.broadcasted_iota(jnp.int32, (BLK, LANES), 0)
    a = jnp.zeros((BLK, BLK), F32)
    for l in range(levels):
        upper = ((rows >> l) & 1) == 1
        e = _level_exponent(l, e_low, g_h, upper)
        u = (jnp.where(upper, q_h, k_h) * jnp.exp(e)).astype(BF16)
        a = jnp.where(lvl == l, _dot_nt(u, u), a)
        yield
    diag = jnp.sum(q_h * k_h, axis=-1, keepdims=True)
    out["a"] = jnp.where(lvl == -1, diag, a)


def _hgrn_scores(q_h, k_h, e_low, g_h, lvl, levels):
    out = {}
    for _ in _hgrn_scores_steps(q_h, k_h, e_low, g_h, lvl, levels, out):
        pass
    return out["a"]


def _gated_rmsnorm_cols(cols, gain_row, gate, width):
    ss = None
    for c in cols:
        s = jnp.sum(c * c, axis=-1, keepdims=True)
        ss = s if ss is None else ss + s
    inv = lax.rsqrt(ss * (1.0 / width) + NORM_EPS)
    out = []
    for i, c in enumerate(cols):
        gt = gate[:, i * LANES:(i + 1) * LANES]
        out.append(c * inv * gain_row[:, i * LANES:(i + 1) * LANES] * _silu(gt))
    return out


def _layernorm(hpre, ln_g, ln_b):
    mu = jnp.mean(hpre, axis=-1, keepdims=True)
    cen = hpre - mu
    var = jnp.mean(cen * cen, axis=-1, keepdims=True)
    return cen * lax.rsqrt(var + NORM_EPS) * ln_g + ln_b


def _chunked_dot(a, w_ref, n):
    return jnp.concatenate([_dot(a, w_ref[c]) for c in range(n)], axis=1)


def _interleave(gen, n_yields, thunks):
    thunks = list(thunks)
    total, done, seen = len(thunks), 0, 0
    for _ in gen:
        seen += 1
        want = min(total, (seen * total + n_yields - 1) // n_yields)
        while done < want:
            thunks[done]()
            done += 1
    while done < total:
        thunks[done]()
        done += 1


def _prompt_kernel(sinks_ref, x_ref, xn_ref, p_ref, cq_ref, sq_ref, ck_ref, sk_ref, lvl_ref, mlow_ref,
                   w_in_ref, w_tail_ref, w_out_ref, w_pg_ref, w_pp_ref, v512_ref, v1024_ref,
                   y_ref, kk_ref, vk_ref, sfin_ref,
                   st_scr, kprev_scr, vprev_scr, z_scr, mix_scr):
    t = pl.program_id(1)
    nblk = PROMPT_TILE // BLK

    @pl.when(t == 0)
    def _():
        st_scr[...] = jnp.zeros_like(st_scr)
        kprev_scr[...] = jnp.zeros_like(kprev_scr)
        vprev_scr[...] = jnp.zeros_like(vprev_scr)

    lvl = lvl_ref[...]
    attn_g = v512_ref[0:1, :]
    log_lb = v512_ref[1:2, :]
    log1m_lb = v512_ref[2:3, :]
    hg_g = v512_ref[3:4, :]

    row = lax.broadcasted_iota(jnp.int32, (2 * BLK, BLK), 0)
    col = lax.broadcasted_iota(jnp.int32, (2 * BLK, BLK), 1)
    qrow = row & (BLK - 1)
    mask_cur = col <= qrow
    mask_prev_any = col > qrow
    first_rows = lax.broadcasted_iota(jnp.int32, (2 * BLK, 1), 0) < BLK
    carry = [{} for _ in range(PROMPT_SEQS)]

    def stage_in(q, j):
        slot, rows_j, cell = j % 2, slice(j * BLK, (j + 1) * BLK), {}

        def chunk(c):
            def run():
                if "xb" not in cell:
                    cell["xb"] = (xn_ref[q] if j == nblk else x_ref[q, rows_j, :]).astype(BF16)
                if c < IN_CHUNKS:
                    z_scr[q, slot, :, c * CHUNK_COLS:(c + 1) * CHUNK_COLS] = _dot(cell["xb"], w_in_ref[c])
                else:
                    z_scr[q, slot, :, IN_CHUNKS * CHUNK_COLS:IN_COLS] = _dot(cell["xb"], w_tail_ref[0])
            return run
        return [chunk(c) for c in range(IN_CHUNKS + 1)]

    def stage_out(q, j):
        slot, rows_j, cell = j % 2, slice(j * BLK, (j + 1) * BLK), {"m": [], "g": []}

        def mix_chunk(c):
            def run():
                cell["m"].append(_dot(mix_scr[q, slot], w_out_ref[c]))
                if c + 1 == OUT_CHUNKS:
                    hpre = DN_ALPHA * x_ref[q, rows_j, :] + jnp.concatenate(cell.pop("m"), axis=1)
                    h = _layernorm(hpre, v1024_ref[0:1, :], v1024_ref[1:2, :])
                    cell["h"] = h
                    cell["hb"] = h.astype(BF16)
            return run

        def gate_chunk(c):
            def run():
                cell["g"].append(_sigmoid(_dot(cell["hb"], w_pg_ref[c])))
            return run

        def finish():
            pb = p_ref[q, rows_j, :].astype(BF16)
            for c in range(OUT_CHUNKS):
                cs = slice(c * CHUNK_COLS, (c + 1) * CHUNK_COLS)
                y_ref[q, rows_j, cs] = cell["h"][:, cs] + cell["g"][c] * _dot(pb, w_pp_ref[c])
        return ([mix_chunk(c) for c in range(OUT_CHUNKS)] + [gate_chunk(c) for c in range(OUT_CHUNKS)]
                + [finish])

    def attn_part(q, j):
        slot, rows_j = j % 2, slice(j * BLK, (j + 1) * BLK)

        def zc(c0, width):
            return z_scr[q, slot, :, c0:c0 + width]
        qcols = [_rope(zc(C_Q + c * LANES, LANES), cq_ref[rows_j, :], sq_ref[rows_j, :]).astype(BF16)
                 for c in range(4)]
        k_rot = _rope(zc(C_K, KV_WIDTH), ck_ref[rows_j, :], sk_ref[rows_j, :])
        v_new = zc(C_V, KV_WIDTH)
        yield
        if j == 0:
            k_prev = [kprev_scr[q, i] for i in range(4)]
            v_prev = [vprev_scr[q, i] for i in range(4)]
            mask_prev = jnp.logical_and(mask_prev_any, t > 0)
        else:
            k_prev, v_prev = carry[q]["k_var"], carry[q]["v_var"]
            mask_prev = mask_prev_any
        k_var = _kv_variants(k_rot, 0.0)
        v_var = _kv_variants(v_new, 1.0)
        carry[q]["k_var"], carry[q]["v_var"] = k_var, v_var
        if j == nblk - 1:
            for i in range(4):
                kprev_scr[q, i] = k_var[i]
                vprev_scr[q, i] = v_var[i]
            kk_ref[q] = k_rot
            vk_ref[q] = v_new
        yield
        res = []
        for g in range(2):
            qst = jnp.concatenate([qcols[2 * g], qcols[2 * g + 1]], axis=0)
            for e in range(2):
                i = g * 2 + e
                s_cur = jnp.where(mask_cur, _dot_nt(qst, k_var[i]), NEG_INF)
                s_prev = jnp.where(mask_prev, _dot_nt(qst, k_prev[i]), NEG_INF)
                yield
                sink_col = jnp.where(first_rows, sinks_ref[4 * g + e], sinks_ref[4 * g + 2 + e])
                p_cur, p_prev, es = _softmax_parts(s_cur, s_prev, sink_col)
                yield
                r = _dot(p_cur.astype(BF16), v_var[i]) + _dot(p_prev.astype(BF16), v_prev[i])
                res.append(r + _sum_half(e, es, 2 * BLK))
                yield
        acols = _assemble_attn(res, BLK)
        acols = _gated_rmsnorm_cols(acols, attn_g, zc(C_GA, ATTN_WIDTH), ATTN_WIDTH)
        for c in range(4):
            mix_scr[q, slot, :, c * LANES:(c + 1) * LANES] = acols[c].astype(BF16)
        yield

    def hgrn_part(q, j):
        slot = j % 2

        def zc(c0, width):
            return z_scr[q, slot, :, c0:c0 + width]
        lf = _log_decay(zc(C_HF, HG_KW), log_lb, log1m_lb)
        kin = 1.0 - jnp.exp(lf)
        lf_hi, lf_lo = _split_bf16(lf)
        m_low = mlow_ref[0:LOW_LEVELS * BLK, :]
        m_cum = mlow_ref[LOW_LEVELS * BLK:(LOW_LEVELS + 1) * BLK, :]
        eg = jnp.concatenate([_dot(m_low, lf_hi), _dot(m_cum, lf_hi) + _dot(m_cum, lf_lo)], axis=0)
        yield
        for h in range(HG_HEADS):
            hs = slice(h * LANES, (h + 1) * LANES)
            q_h = zc(C_HQ + h * LANES, LANES)
            k_h = kin[:, hs]
            v_h = zc(C_HI + h * LANES, LANES).astype(BF16)
            e_low = [eg[l * BLK:(l + 1) * BLK, hs] for l in range(LOW_LEVELS)]
            g_h = eg[LOW_LEVELS * BLK:(LOW_LEVELS + 1) * BLK, hs]
            out = {}
            yield from _hgrn_scores_steps(q_h, k_h, e_low, g_h, lvl, PROMPT_LEVELS, out)
            st = st_scr[q, h]
            o_h = (_dot(out["a"].astype(BF16), v_h)
                   + _dot_nt((q_h * jnp.exp(g_h)).astype(BF16), st.astype(BF16)))
            g_last = g_h[BLK - 1:BLK, :]
            kd = (k_h * jnp.exp(g_last - g_h)).astype(BF16)
            st_scr[q, h] = st * jnp.exp(g_last) + _dot_tn(v_h, kd)
            yield
            gt = zc(C_GH + h * LANES, LANES)
            inv = lax.rsqrt(jnp.mean(o_h * o_h, axis=-1, keepdims=True) + NORM_EPS)
            mix_scr[q, slot, :, ATTN_WIDTH + h * LANES:ATTN_WIDTH + (h + 1) * LANES] = (
                o_h * inv * hg_g[:, hs] * _silu(gt)).astype(BF16)
            yield

    def stage_mix(j):
        def one(q):
            parts = (attn_part, hgrn_part) if q % 2 == 0 else (hgrn_part, attn_part)
            for part in parts:
                yield from part(q, j)
        gens = [one(q) for q in range(PROMPT_SEQS)]
        live = list(gens)
        while live:
            for g in list(live):
                try:
                    next(g)
                    yield
                except StopIteration:
                    live.remove(g)

    mix_yields = PROMPT_SEQS * (2 + 3 * 4 + 2 + HG_HEADS * (PROMPT_LEVELS + 2))

    @pl.when(jnp.logical_and(pl.program_id(0) == 0, t == 0))
    def _():
        for q in range(PROMPT_SEQS):
            for th in stage_in(q, 0):
                th()

    def both(stage, j):
        lists = [stage(q, j) for q in range(PROMPT_SEQS)]
        return [th for group in zip(*lists) for th in group]

    for j in range(nblk):
        side = both(stage_in, j + 1)
        if j >= 1:
            side = side + both(stage_out, j - 1)
        _interleave(stage_mix(j), mix_yields, side)
    for th in both(stage_out, nblk - 1):
        th()

    @pl.when(t == pl.num_programs(1) - 1)
    def _():
        for q in range(PROMPT_SEQS):
            for h in range(HG_HEADS):
                sfin_ref[q, h] = st_scr[q, h].T


def _sample_kernel(sinks_ref, x_ref, p_ref, ck_ref, cv_ref, s0_ref, cq_ref, sq_ref, ckt_ref, skt_ref,
                   lvl_ref, mlow_ref, seg_ref, w_in_ref, w_tail_ref, w_out_ref, w_pg_ref, w_pp_ref, v512_ref,
                   v1024_ref,
                   y_ref, nk_ref, nv_ref, ns_ref,
                   y_scr, qm_scr, attn_scr, mix_scr):
    layer = pl.program_id(0)
    blk = pl.program_id(1)
    rows = SAMPLE_SEQS * 8
    yrows = pl.ds(pl.multiple_of(blk * rows, rows), rows)

    @pl.when(layer == 0)
    def _():
        y_scr[yrows, :] = x_ref[...].reshape(rows, D_MODEL)

    x = y_scr[yrows, :]
    xb = x.astype(BF16)
    z = jnp.concatenate([_chunked_dot(xb, w_in_ref, IN_CHUNKS), _dot(xb, w_tail_ref[0])], axis=1)
    lvl = lvl_ref[...]
    attn_g = v512_ref[0:1, :]
    log_lb = v512_ref[1:2, :]
    log1m_lb = v512_ref[2:3, :]
    hg_g = v512_ref[3:4, :]

    qcols = [_rope(z[:, C_Q + c * LANES:C_Q + (c + 1) * LANES], cq_ref[...], sq_ref[...]) for c in range(4)]
    k_rot = _rope(z[:, C_K:C_K + KV_WIDTH], ckt_ref[...], skt_ref[...])
    v_new = z[:, C_V:C_V + KV_WIDTH]
    k_new_b = k_rot.astype(BF16)
    v_new_b = v_new.astype(BF16)
    lo2 = _lane_iota(2 * rows) < HEAD_DIM
    for g in range(2):
        xg = jnp.concatenate([qcols[2 * g], qcols[2 * g + 1]], axis=0)
        xsw = pltpu.roll(xg, HEAD_DIM, 1)
        keep = lo2 if g == 0 else jnp.logical_not(lo2)
        for e in range(2):
            qm = jnp.where(keep, xg if e == g else xsw, 0.0)
            for cc in range(2):
                r0 = ((g * 2 + e) * 2 + cc) * 8
                for s in range(SAMPLE_SEQS):
                    qm_scr[s, r0:r0 + 8, :] = qm[cc * rows + s * 8:cc * rows + s * 8 + 8, :]

    lf = _log_decay(z[:, C_HF:C_HF + HG_KW], log_lb, log1m_lb)
    kin = 1.0 - jnp.exp(lf)
    lf_hi, lf_lo = _split_bf16(lf)
    eg = _dot(mlow_ref[...], lf_hi) + _dot(mlow_ref[...], lf_lo)
    dec_t = jnp.exp(_dot_tn(lf_hi, seg_ref[...]) + _dot_tn(lf_lo, seg_ref[...]))
    o_intra, qe, kd, vh = [], [], [], []
    for h in range(HG_HEADS):
        hs = slice(h * LANES, (h + 1) * LANES)
        q_h = z[:, C_HQ + h * LANES:C_HQ + (h + 1) * LANES]
        k_h = kin[:, hs]
        v_h = z[:, C_HI + h * LANES:C_HI + (h + 1) * LANES]
        e_low = [eg[l * rows:(l + 1) * rows, hs] for l in range(LOW_LEVELS)]
        g_h = eg[LOW_LEVELS * rows:(LOW_LEVELS + 1) * rows, hs]
        tot_h = eg[(LOW_LEVELS + 1) * rows:(LOW_LEVELS + 2) * rows, hs]
        a = _hgrn_scores(q_h, k_h, e_low, g_h, lvl, LOW_LEVELS)
        o_intra.append(_dot(a.astype(BF16), v_h.astype(BF16)))
        qe.append(q_h * jnp.exp(g_h))
        kd.append(k_h * jnp.exp(tot_h - g_h))
        vh.append(v_h)

    grp = 4
    grows = grp * 64
    rid = lax.broadcasted_iota(jnp.int32, (grows, 1), 0)
    r8 = (rid >> 3) & 7
    sink_col = jnp.zeros((grows, 1), F32)
    for idx in range(8):
        g, e, cc = idx // 4, (idx // 2) % 2, idx % 2
        sink_col = jnp.where(r8 == idx, sinks_ref[layer * ATTN_HEADS + 4 * g + 2 * cc + e], sink_col)
    row = lax.broadcasted_iota(jnp.int32, (grows, 2 * WINDOW), 0)
    col = lax.broadcasted_iota(jnp.int32, (grows, 2 * WINDOW), 1)
    tok = row & 7
    new_col = col - WINDOW
    mask_cache = jnp.logical_and(col < WINDOW, col > tok)
    lo8 = _lane_iota(8) < HEAD_DIM
    o_inter = [[None] * SAMPLE_SEQS for _ in range(HG_HEADS)]
    for gi in range(SAMPLE_SEQS // grp):
        seqs = range(gi * grp, (gi + 1) * grp)
        kcs = {s: ck_ref[s] for s in seqs}
        vcs = {s: cv_ref[s] for s in seqs}
        sc = [_dot_nt(qm_scr[s].astype(BF16),
                      jnp.concatenate([kcs[s].astype(BF16), k_new_b], axis=0)) for s in seqs]
        for s in seqs:
            nk_ref[s, 0:WINDOW - 8, :] = kcs[s][8:WINDOW, :]
            nk_ref[s, WINDOW - 8:WINDOW, :] = k_rot[s * 8:s * 8 + 8, :]
            nv_ref[s, 0:WINDOW - 8, :] = vcs[s][8:WINDOW, :]
            nv_ref[s, WINDOW - 8:WINDOW, :] = v_new[s * 8:s * 8 + 8, :]
        mask_new = jnp.logical_and(jnp.logical_and(col >= WINDOW, (new_col >> 3) == (row >> 6) + gi * grp),
                                   (new_col & 7) <= tok)
        s_all = jnp.where(jnp.logical_or(mask_cache, mask_new), jnp.concatenate(sc, axis=0), NEG_INF)
        m = jnp.maximum(jnp.max(s_all, axis=-1, keepdims=True), sink_col)
        p = jnp.exp(s_all - m)
        den = jnp.sum(p, axis=-1, keepdims=True) + jnp.exp(sink_col - m)
        pb = p.astype(BF16)
        o_all = jnp.concatenate(
            [_dot(pb[i * 64:(i + 1) * 64, :], jnp.concatenate([vcs[s].astype(BF16), v_new_b], axis=0))
             for i, s in enumerate(seqs)], axis=0) / den
        o_sw = pltpu.roll(o_all, HEAD_DIM, 1)
        for i, s in enumerate(seqs):
            for c in range(4):
                g, cc = c // 2, c % 2
                ra = i * 64 + ((g * 2 + 0) * 2 + cc) * 8
                rb = i * 64 + ((g * 2 + 1) * 2 + cc) * 8
                part0 = (o_all if g == 0 else o_sw)[ra:ra + 8, :]
                part1 = (o_all if g == 1 else o_sw)[rb:rb + 8, :]
                attn_scr[c, s * 8:s * 8 + 8, :] = jnp.where(lo8, part0, part1)
        s0s = {(s, h): s0_ref[s, h] for s in seqs for h in range(HG_HEADS)}
        for s in seqs:
            for h in range(HG_HEADS):
                o_inter[h][s] = _dot(qe[h][s * 8:s * 8 + 8, :].astype(BF16), s0s[(s, h)].astype(BF16))
        upd = {(s, h): _dot_tn(kd[h][s * 8:s * 8 + 8, :].astype(BF16), vh[h][s * 8:s * 8 + 8, :].astype(BF16))
               for s in seqs for h in range(HG_HEADS)}
        for s in seqs:
            for h in range(HG_HEADS):
                dcol = dec_t[h * LANES:(h + 1) * LANES, s * 8:s * 8 + 1]
                ns_ref[s, h] = dcol * s0s[(s, h)] + upd[(s, h)]

    acols = _gated_rmsnorm_cols([attn_scr[c] for c in range(4)], attn_g, z[:, C_GA:C_GA + ATTN_WIDTH],
                                ATTN_WIDTH)
    for c in range(4):
        mix_scr[:, c * LANES:(c + 1) * LANES] = acols[c].astype(BF16)
    for h in range(HG_HEADS):
        hs = slice(h * LANES, (h + 1) * LANES)
        o_h = o_intra[h] + jnp.concatenate(o_inter[h], axis=0)
        gt = z[:, C_GH + h * LANES:C_GH + (h + 1) * LANES]
        inv = lax.rsqrt(jnp.mean(o_h * o_h, axis=-1, keepdims=True) + NORM_EPS)
        mix_scr[:, ATTN_WIDTH + h * LANES:ATTN_WIDTH + (h + 1) * LANES] = (
            o_h * inv * hg_g[:, hs] * _silu(gt)).astype(BF16)

    hpre = DN_ALPHA * x + _chunked_dot(mix_scr[...], w_out_ref, OUT_CHUNKS)
    hn = _layernorm(hpre, v1024_ref[0:1, :], v1024_ref[1:2, :])
    gate = _sigmoid(_chunked_dot(hn.astype(BF16), w_pg_ref, OUT_CHUNKS))
    pp = _chunked_dot(p_ref[...].reshape(rows, PLE_DIM).astype(BF16), w_pp_ref, OUT_CHUNKS)
    y = hn + gate * pp
    y_scr[yrows, :] = y
    y_ref[...] = y.reshape(SAMPLE_SEQS, 8, D_MODEL)


def _level_matrix():
    t = np.arange(BLK)[:, None]
    s = np.arange(BLK)[None, :]
    x = t ^ s
    lv = np.floor(np.log2(np.maximum(x, 1))).astype(np.int32)
    return np.where(t > s, lv, np.where(t == s, -1, -2)).astype(np.int32)


def _level_exponent_matrix(l):
    m = np.zeros((BLK, BLK), np.float32)
    b, h = 2 << l, 1 << l
    for t in range(BLK):
        mid = t - t % b + h
        if t >= mid:
            m[t, mid:t + 1] = 1.0
        else:
            m[t, t + 1:mid] = 1.0
    return m


def _mask_matrices(seq_rows):
    blocks = [_level_exponent_matrix(l) for l in range(LOW_LEVELS)]
    t = np.arange(BLK)[:, None]
    s = np.arange(BLK)[None, :]
    same = (t // seq_rows) == (s // seq_rows)
    blocks.append((same & (s <= t)).astype(np.float32))
    if seq_rows < BLK:
        blocks.append(same.astype(np.float32))
    return np.concatenate(blocks, axis=0)


def _rope_tables(pos, scale):
    half = HEAD_DIM // 2
    inv = jnp.exp(-math.log(ROPE_THETA) * jnp.arange(half, dtype=F32) * 2.0 / HEAD_DIM)
    ang = pos.astype(F32)[:, None] * inv[None, :]
    cos = jnp.cos(ang) * scale
    sin = jnp.sin(ang) * scale
    return jnp.tile(cos, (1, 4)), jnp.concatenate([-sin, sin, -sin, sin], axis=1)


def _column_chunks(w, width):
    d, k, n = w.shape
    return w.astype(BF16).reshape(d, k, n // width, width).transpose(0, 2, 1, 3)


def _const_spec(shape, layer=None):
    if layer is None:
        return pl.BlockSpec(shape, lambda *_: (0,) * len(shape))
    return pl.BlockSpec((None,) + shape, lambda *_: (layer,) + (0,) * len(shape))


def _prompt_layer(i, x, p_all, tables, lvl, mlow, weights, sinks, v512, v1024):
    B, T, _ = x.shape
    nt = T // PROMPT_TILE
    nb = B // PROMPT_SEQS
    tab_spec = pl.BlockSpec((PROMPT_TILE, LANES), lambda b, t: (t, 0))
    nblk = PROMPT_TILE // BLK
    assert nblk % 2 == 0

    def next_first_block(b, t):
        flat = jnp.minimum(b * nt + t + 1, nb * nt - 1)
        return (flat // nt, (flat % nt) * nblk, 0)

    def wspec(shape):
        return pl.BlockSpec((None,) + shape, lambda b, t: (i,) + (0,) * len(shape),
                            pipeline_mode=pl.Buffered(1))
    in_specs = [
        pl.BlockSpec(memory_space=pltpu.SMEM),
        pl.BlockSpec((PROMPT_SEQS, PROMPT_TILE, D_MODEL), lambda b, t: (b, t, 0)),
        pl.BlockSpec((PROMPT_SEQS, BLK, D_MODEL), next_first_block),
        pl.BlockSpec((None, PROMPT_SEQS, PROMPT_TILE, PLE_DIM), lambda b, t: (i, b, t, 0)),
        tab_spec, tab_spec, tab_spec, tab_spec,
        _const_spec((BLK, BLK)),
        _const_spec(mlow.shape),
        wspec((IN_CHUNKS, D_MODEL, CHUNK_COLS)),
        wspec((1, D_MODEL, IN_TAIL)),
        wspec((OUT_CHUNKS, MIX_WIDTH, CHUNK_COLS)),
        wspec((OUT_CHUNKS, D_MODEL, CHUNK_COLS)),
        wspec((OUT_CHUNKS, PLE_DIM, CHUNK_COLS)),
        _const_spec((4, ATTN_WIDTH), i),
        _const_spec((2, D_MODEL), i),
    ]
    out_shape = [
        jax.ShapeDtypeStruct((B, T, D_MODEL), F32),
        jax.ShapeDtypeStruct((B, WINDOW, KV_WIDTH), F32),
        jax.ShapeDtypeStruct((B, WINDOW, KV_WIDTH), F32),
        jax.ShapeDtypeStruct((B, HG_HEADS, HG_DK, HG_DV), F32),
    ]
    out_specs = [
        pl.BlockSpec((PROMPT_SEQS, PROMPT_TILE, D_MODEL), lambda b, t: (b, t, 0)),
        pl.BlockSpec((PROMPT_SEQS, WINDOW, KV_WIDTH), lambda b, t: (b, 0, 0)),
        pl.BlockSpec((PROMPT_SEQS, WINDOW, KV_WIDTH), lambda b, t: (b, 0, 0)),
        pl.BlockSpec((PROMPT_SEQS, HG_HEADS, HG_DK, HG_DV), lambda b, t: (b, 0, 0, 0)),
    ]
    scratch = [
        pltpu.VMEM((PROMPT_SEQS, HG_HEADS, HG_DV, HG_DK), F32),
        pltpu.VMEM((PROMPT_SEQS, 4, BLK, LANES), BF16),
        pltpu.VMEM((PROMPT_SEQS, 4, BLK, LANES), BF16),
        pltpu.VMEM((PROMPT_SEQS, 2, BLK, IN_COLS), F32),
        pltpu.VMEM((PROMPT_SEQS, 2, BLK, MIX_WIDTH), BF16),
    ]
    return pl.pallas_call(
        _prompt_kernel,
        grid=(nb, nt),
        in_specs=in_specs,
        out_specs=out_specs,
        out_shape=out_shape,
        scratch_shapes=scratch,
        compiler_params=pltpu.CompilerParams(
            dimension_semantics=("arbitrary", "arbitrary"), vmem_limit_bytes=VMEM_LIMIT),
        name=f"prompt_layer{i}",
    )(sinks, x, x, p_all, *tables, lvl, mlow, *weights, v512, v1024)


def _sample_call(x, p_all, ck_all, cv_all, s0_all, tables, lvl, mlow, seg, weights, sinks, v512, v1024):
    B, T, _ = x.shape
    rows = SAMPLE_SEQS * T

    def wspec(shape):
        return pl.BlockSpec((None,) + shape, lambda l, b: (l,) + (0,) * len(shape),
                            pipeline_mode=pl.Buffered(1))
    tab_spec = _const_spec((rows, LANES))
    in_specs = [
        pl.BlockSpec(memory_space=pltpu.SMEM),
        pl.BlockSpec((SAMPLE_SEQS, T, D_MODEL), lambda l, b: (jnp.where(l == 0, b, 0), 0, 0)),
        pl.BlockSpec((None, SAMPLE_SEQS, T, PLE_DIM), lambda l, b: (l, b, 0, 0)),
        pl.BlockSpec((None, SAMPLE_SEQS, WINDOW, KV_WIDTH), lambda l, b: (l, b, 0, 0)),
        pl.BlockSpec((None, SAMPLE_SEQS, WINDOW, KV_WIDTH), lambda l, b: (l, b, 0, 0)),
        pl.BlockSpec((None, SAMPLE_SEQS, HG_HEADS, HG_DK, HG_DV), lambda l, b: (l, b, 0, 0, 0)),
        tab_spec, tab_spec, tab_spec, tab_spec,
        _const_spec((BLK, BLK)),
        _const_spec(mlow.shape),
        _const_spec((BLK, LANES)),
        wspec((IN_CHUNKS, D_MODEL, CHUNK_COLS)),
        wspec((1, D_MODEL, IN_TAIL)),
        wspec((OUT_CHUNKS, MIX_WIDTH, CHUNK_COLS)),
        wspec((OUT_CHUNKS, D_MODEL, CHUNK_COLS)),
        wspec((OUT_CHUNKS, PLE_DIM, CHUNK_COLS)),
        pl.BlockSpec((None, 4, ATTN_WIDTH), lambda l, b: (l, 0, 0)),
        pl.BlockSpec((None, 2, D_MODEL), lambda l, b: (l, 0, 0)),
    ]
    out_shape = [
        jax.ShapeDtypeStruct((B, T, D_MODEL), F32),
        jax.ShapeDtypeStruct((DEPTH, B, WINDOW, KV_WIDTH), F32),
        jax.ShapeDtypeStruct((DEPTH, B, WINDOW, KV_WIDTH), F32),
        jax.ShapeDtypeStruct((DEPTH, B, HG_HEADS, HG_DK, HG_DV), F32),
    ]
    out_specs = [
        pl.BlockSpec((SAMPLE_SEQS, T, D_MODEL), lambda l, b: (jnp.where(l == DEPTH - 1, b, 0), 0, 0)),
        pl.BlockSpec((None, SAMPLE_SEQS, WINDOW, KV_WIDTH), lambda l, b: (l, b, 0, 0)),
        pl.BlockSpec((None, SAMPLE_SEQS, WINDOW, KV_WIDTH), lambda l, b: (l, b, 0, 0)),
        pl.BlockSpec((None, SAMPLE_SEQS, HG_HEADS, HG_DK, HG_DV), lambda l, b: (l, b, 0, 0, 0)),
    ]
    scratch = [
        pltpu.VMEM((B * T, D_MODEL), F32),
        pltpu.VMEM((SAMPLE_SEQS, 64, LANES), F32),
        pltpu.VMEM((4, rows, LANES), F32),
        pltpu.VMEM((rows, MIX_WIDTH), BF16),
    ]
    return pl.pallas_call(
        _sample_kernel,
        grid=(DEPTH, B // SAMPLE_SEQS),
        in_specs=in_specs,
        out_specs=out_specs,
        out_shape=out_shape,
        scratch_shapes=scratch,
        compiler_params=pltpu.CompilerParams(
            dimension_semantics=("arbitrary", "arbitrary"), vmem_limit_bytes=VMEM_LIMIT),
        name="sample_layers",
    )(sinks, x, p_all, ck_all, cv_all, s0_all, *tables, lvl, mlow, seg, *weights, v512, v1024)


def kernel(x_prompt, x_sample, cache_k_win, cache_v_win, state_hgrn, p_prompt, p_sample, w_in, attn_sinks,
           attn_norm_g, hg_lb_logits, hg_norm_g, w_out, ln_g, ln_b, w_ple_proj, w_ple_gate):
    B, T, _ = x_prompt.shape
    SB, ST, _ = x_sample.shape
    assert T % PROMPT_TILE == 0 and B % PROMPT_SEQS == 0
    assert SB % SAMPLE_SEQS == 0 and ST == 8 and SAMPLE_SEQS * ST == BLK
    assert cache_k_win.shape[2] == WINDOW

    cs = jnp.cumsum(jax.nn.softmax(hg_lb_logits.astype(F32), axis=0), axis=0)
    lbs = cs - cs[:1]
    v512 = jnp.stack([attn_norm_g.astype(F32), jnp.log(lbs), jnp.log1p(-lbs), hg_norm_g.astype(F32)], axis=1)
    v1024 = jnp.stack([ln_g.astype(F32), ln_b.astype(F32)], axis=1)
    n_main = IN_CHUNKS * CHUNK_COLS
    weights = (_column_chunks(w_in[:, :, :n_main], CHUNK_COLS), _column_chunks(w_in[:, :, n_main:], IN_TAIL),
               _column_chunks(w_out, CHUNK_COLS), _column_chunks(w_ple_gate, CHUNK_COLS),
               _column_chunks(w_ple_proj, CHUNK_COLS))
    sinks = attn_sinks.astype(F32)

    scale = HEAD_DIM ** -0.5
    pos_p = jnp.arange(T, dtype=jnp.int32)
    pos_s = jnp.tile(PAST_LEN + jnp.arange(ST, dtype=jnp.int32), SAMPLE_SEQS)
    tab_p = _rope_tables(pos_p, scale) + _rope_tables(pos_p, 1.0)
    tab_s = _rope_tables(pos_s, scale) + _rope_tables(pos_s, 1.0)
    lvl = jnp.asarray(_level_matrix())
    mlow_p = jnp.asarray(_mask_matrices(BLK), dtype=BF16)
    mlow_s = jnp.asarray(_mask_matrices(ST), dtype=BF16)
    seg = jnp.asarray((np.arange(BLK)[:, None] // ST) == (np.arange(LANES)[None, :] // ST), dtype=BF16)

    ck = cache_k_win.reshape(DEPTH, SB, WINDOW, KV_WIDTH)
    cv = cache_v_win.reshape(DEPTH, SB, WINDOW, KV_WIDTH)
    ys, ks, vs, ss = _sample_call(x_sample, p_sample, ck, cv, state_hgrn, tab_s, lvl, mlow_s, seg, weights,
                                  sinks.reshape(DEPTH * ATTN_HEADS), v512, v1024)

    yp = x_prompt
    kp_l, vp_l, sp_l = [], [], []
    for i in range(DEPTH):
        yp, kp, vp, sp = _prompt_layer(i, yp, p_prompt, tab_p, lvl, mlow_p, weights, sinks[i], v512, v1024)
        kp_l.append(kp); vp_l.append(vp); sp_l.append(sp)

    def kv5(a, b):
        return a.reshape(DEPTH, b, WINDOW, KV_HEADS, HEAD_DIM)

    return (yp, ys, kv5(jnp.stack(kp_l), B), kv5(jnp.stack(vp_l), B), jnp.stack(sp_l),
            kv5(ks, SB), kv5(vs, SB), ss)
```

```python
import math

import numpy as np
import jax
import jax.numpy as jnp
from jax import lax
from jax.experimental import pallas as pl
from jax.experimental.pallas import tpu as pltpu

D_MODEL = 1024
DEPTH = 4
PAST_LEN = 8192
ATTN_HEADS = 8
KV_HEADS = 2
HEAD_DIM = 64
ATTN_WIDTH = ATTN_HEADS * HEAD_DIM
KV_WIDTH = KV_HEADS * HEAD_DIM
WINDOW = 128
ROPE_THETA = 10000.0
HG_HEADS = 4
HG_DK = 128
HG_DV = 128
HG_KW = HG_HEADS * HG_DK
HG_VW = HG_HEADS * HG_DV
MIX_WIDTH = ATTN_WIDTH + HG_VW
IN_COLS = 2 * ATTN_WIDTH + 2 * KV_WIDTH + 2 * HG_KW + 2 * HG_VW
PLE_DIM = 256
DN_ALPHA = (2 * DEPTH) ** 0.25
NORM_EPS = 1e-5
NEG_INF = -1e30

C_Q = 0
C_K = C_Q + ATTN_WIDTH
C_V = C_K + KV_WIDTH
C_GA = C_V + KV_WIDTH
C_HQ = C_GA + ATTN_WIDTH
C_HF = C_HQ + HG_KW
C_HI = C_HF + HG_KW
C_GH = C_HI + HG_VW

LANES = 128
MXU_COLS = 256
CHUNK_COLS = 2 * MXU_COLS
BLK = 128
PROMPT_TILE = 512
PROMPT_SEQS = 1
IN_GROUP = 2
SAMPLE_SEQS = 16
LOW_LEVELS = 3
PROMPT_LEVELS = 7
VMEM_LIMIT = 56 * 1024 * 1024
IN_CHUNKS = IN_COLS // CHUNK_COLS
IN_TAIL = IN_COLS - IN_CHUNKS * CHUNK_COLS
OUT_CHUNKS = D_MODEL // CHUNK_COLS

F32 = jnp.float32
BF16 = jnp.bfloat16


def _dot(a, b):
    return jnp.dot(a, b, preferred_element_type=F32)


def _dot_nt(a, b):
    return lax.dot_general(a, b, (((1,), (1,)), ((), ())), preferred_element_type=F32)


def _dot_tn(a, b):
    return lax.dot_general(a, b, (((0,), (0,)), ((), ())), preferred_element_type=F32)


def _split_bf16(x):
    hi = x.astype(BF16)
    lo = (x - hi.astype(F32)).astype(BF16)
    return hi, lo


def _sigmoid(x):
    return 0.5 * jnp.tanh(0.5 * x) + 0.5


def _silu(x):
    h = 0.5 * x
    return h * jnp.tanh(h) + h


def _lane_iota(rows):
    return lax.broadcasted_iota(jnp.int32, (rows, LANES), 1)


def _rope(x, cos_t, sin_t):
    lane = _lane_iota(x.shape[0])
    first_half = (lane & 32) == 0
    swapped = jnp.where(first_half, pltpu.roll(x, 96, 1), pltpu.roll(x, 32, 1))
    return x * cos_t + swapped * sin_t


def _kv_variants(a, fill):
    lane = _lane_iota(a.shape[0])
    lo = lane < HEAD_DIM
    sw = pltpu.roll(a, HEAD_DIM, 1)
    f = jnp.full_like(a, fill)
    out = [jnp.where(lo, a, f), jnp.where(lo, f, sw), jnp.where(lo, sw, f), jnp.where(lo, f, a)]
    return [o.astype(BF16) for o in out]


def _softmax_parts(s_cur, s_prev, sink_col):
    m = jnp.maximum(jnp.maximum(jnp.max(s_cur, axis=-1, keepdims=True),
                                jnp.max(s_prev, axis=-1, keepdims=True)), sink_col)
    return jnp.exp(s_cur - m), jnp.exp(s_prev - m), jnp.exp(sink_col - m)


def _sum_half(e, es, rows):
    lane = _lane_iota(rows)
    sum_half = (lane >= HEAD_DIM) if e == 0 else (lane < HEAD_DIM)
    return jnp.where(sum_half, es, 0.0)


def _assemble_attn(res, rows):
    lane = _lane_iota(rows)
    lo = lane < HEAD_DIM
    cols = []
    for c in range(4):
        g, cc = c // 2, c % 2
        r0 = res[g * 2 + 0][cc * rows:(cc + 1) * rows]
        r1 = res[g * 2 + 1][cc * rows:(cc + 1) * rows]
        num = jnp.where(lo, r0, r1)
        den = pltpu.roll(jnp.where(lo, r1, r0), HEAD_DIM, 1)
        cols.append(num / den)
    return cols


def _log_decay(hf, log_lb, log1m_lb):
    ls = jnp.minimum(hf, 0.0) - jnp.log(1.0 + jnp.exp(-jnp.abs(hf)))
    b = log1m_lb + ls
    return jnp.maximum(log_lb, b) + jnp.log(1.0 + jnp.exp(-jnp.abs(log_lb - b)))


def _level_exponent(l, e_low, g_h, upper):
    if l < LOW_LEVELS:
        return e_low[l]
    b, h = 2 << l, 1 << l
    pieces = []
    for i in range(BLK // b):
        r = i * b + h - 1
        pieces.append(jnp.broadcast_to(g_h[r:r + 1, :], (b, LANES)))
    gref = pieces[0] if len(pieces) == 1 else jnp.concatenate(pieces, axis=0)
    d = g_h - gref
    return jnp.where(upper, d, -d)


def _hgrn_scores_steps(q_h, k_h, e_low, g_h, lvl, levels, out):
    rows = lax.broadcasted_iota(jnp.int32, (BLK, LANES), 0)
    a = jnp.zeros((BLK, BLK), F32)
    for l in range(levels):
        upper = ((rows >> l) & 1) == 1
        e = _level_exponent(l, e_low, g_h, upper)
        u = (jnp.where(upper, q_h, k_h) * jnp.exp(e)).astype(BF16)
        a = jnp.where(lvl == l, _dot_nt(u, u), a)
        yield
    diag = jnp.sum(q_h * k_h, axis=-1, keepdims=True)
    out["a"] = jnp.where(lvl == -1, diag, a)


def _hgrn_scores(q_h, k_h, e_low, g_h, lvl, levels):
    out = {}
    for _ in _hgrn_scores_steps(q_h, k_h, e_low, g_h, lvl, levels, out):
        pass
    return out["a"]


def _gated_rmsnorm_cols(cols, gain_row, gate, width):
    ss = None
    for c in cols:
        s = jnp.sum(c * c, axis=-1, keepdims=True)
        ss = s if ss is None else ss + s
    inv = lax.rsqrt(ss * (1.0 / width) + NORM_EPS)
    out = []
    for i, c in enumerate(cols):
        gt = gate[:, i * LANES:(i + 1) * LANES]
        out.append(c * inv * gain_row[:, i * LANES:(i + 1) * LANES] * _silu(gt))
    return out


def _layernorm(hpre, ln_g, ln_b):
    mu = jnp.mean(hpre, axis=-1, keepdims=True)
    cen = hpre - mu
    var = jnp.mean(cen * cen, axis=-1, keepdims=True)
    return cen * lax.rsqrt(var + NORM_EPS) * ln_g + ln_b


def _chunked_dot(a, w_ref, n):
    return jnp.concatenate([_dot(a, w_ref[c]) for c in range(n)], axis=1)


def _interleave(gen, n_yields, thunks):
    thunks = list(thunks)
    total, done, seen = len(thunks), 0, 0
    for _ in gen:
        seen += 1
        want = min(total, (seen * total + n_yields - 1) // n_yields)
        while done < want:
            thunks[done]()
            done += 1
    while done < total:
        thunks[done]()
        done += 1


def _prompt_kernel(sinks_ref, x_ref, xn_ref, p_ref, cq_ref, sq_ref, ck_ref, sk_ref, lvl_ref, mlow_ref,
                   w_in_ref, w_tail_ref, w_out_ref, w_pg_ref, w_pp_ref, v512_ref, v1024_ref,
                   y_ref, kk_ref, vk_ref, sfin_ref,
                   st_scr, kprev_scr, vprev_scr, z_scr, mix_scr):
    t = pl.program_id(1)
    nblk = PROMPT_TILE // BLK

    @pl.when(t == 0)
    def _():
        st_scr[...] = jnp.zeros_like(st_scr)
        kprev_scr[...] = jnp.zeros_like(kprev_scr)
        vprev_scr[...] = jnp.zeros_like(vprev_scr)

    lvl = lvl_ref[...]
    attn_g = v512_ref[0:1, :]
    log_lb = v512_ref[1:2, :]
    log1m_lb = v512_ref[2:3, :]
    hg_g = v512_ref[3:4, :]

    row = lax.broadcasted_iota(jnp.int32, (2 * BLK, BLK), 0)
    col = lax.broadcasted_iota(jnp.int32, (2 * BLK, BLK), 1)
    qrow = row & (BLK - 1)
    mask_cur = col <= qrow
    mask_prev_any = col > qrow
    first_rows = lax.broadcasted_iota(jnp.int32, (2 * BLK, 1), 0) < BLK
    carry = [{} for _ in range(PROMPT_SEQS)]

    ngrp = nblk // IN_GROUP
    grp_rows = IN_GROUP * BLK

    def stage_in(q, g):
        cell = {}
        first = (g % ngrp) * IN_GROUP

        def chunk(c):
            def run():
                if "xb" not in cell:
                    src = xn_ref[q] if g == ngrp else x_ref[q, g * grp_rows:(g + 1) * grp_rows, :]
                    cell["xb"] = src.astype(BF16)
                if c < IN_CHUNKS:
                    cols, zz = slice(c * CHUNK_COLS, (c + 1) * CHUNK_COLS), _dot(cell["xb"], w_in_ref[c])
                else:
                    cols, zz = slice(IN_CHUNKS * CHUNK_COLS, IN_COLS), _dot(cell["xb"], w_tail_ref[0])
                for i in range(IN_GROUP):
                    z_scr[q, first + i, :, cols] = zz[i * BLK:(i + 1) * BLK, :]
            return run
        return [chunk(c) for c in range(IN_CHUNKS + 1)]

    def stage_out(q, j):
        slot, rows_j, cell = j % 2, slice(j * BLK, (j + 1) * BLK), {"m": [], "g": []}

        def mix_chunk(c):
            def run():
                cell["m"].append(_dot(mix_scr[q, slot], w_out_ref[c]))
                if c + 1 == OUT_CHUNKS:
                    hpre = DN_ALPHA * x_ref[q, rows_j, :] + jnp.concatenate(cell.pop("m"), axis=1)
                    h = _layernorm(hpre, v1024_ref[0:1, :], v1024_ref[1:2, :])
                    cell["h"] = h
                    cell["hb"] = h.astype(BF16)
            return run

        def gate_chunk(c):
            def run():
                cell["g"].append(_sigmoid(_dot(cell["hb"], w_pg_ref[c])))
            return run

        def finish():
            pb = p_ref[q, rows_j, :].astype(BF16)
            for c in range(OUT_CHUNKS):
                cs = slice(c * CHUNK_COLS, (c + 1) * CHUNK_COLS)
                y_ref[q, rows_j, cs] = cell["h"][:, cs] + cell["g"][c] * _dot(pb, w_pp_ref[c])
        return ([mix_chunk(c) for c in range(OUT_CHUNKS)] + [gate_chunk(c) for c in range(OUT_CHUNKS)]
                + [finish])

    def attn_part(q, j):
        slot, rows_j = j % 2, slice(j * BLK, (j + 1) * BLK)

        def zc(c0, width):
            return z_scr[q, j, :, c0:c0 + width]
        qcols = [_rope(zc(C_Q + c * LANES, LANES), cq_ref[rows_j, :], sq_ref[rows_j, :]).astype(BF16)
                 for c in range(4)]
        k_rot = _rope(zc(C_K, KV_WIDTH), ck_ref[rows_j, :], sk_ref[rows_j, :])
        v_new = zc(C_V, KV_WIDTH)
        yield
        if j == 0:
            k_prev = [kprev_scr[q, i] for i in range(4)]
            v_prev = [vprev_scr[q, i] for i in range(4)]
            mask_prev = jnp.logical_and(mask_prev_any, t > 0)
        else:
            k_prev, v_prev = carry[q]["k_var"], carry[q]["v_var"]
            mask_prev = mask_prev_any
        k_var = _kv_variants(k_rot, 0.0)
        v_var = _kv_variants(v_new, 1.0)
        carry[q]["k_var"], carry[q]["v_var"] = k_var, v_var
        if j == nblk - 1:
            for i in range(4):
                kprev_scr[q, i] = k_var[i]
                vprev_scr[q, i] = v_var[i]
            kk_ref[q] = k_rot
            vk_ref[q] = v_new
        yield
        res = []
        for g in range(2):
            qst = jnp.concatenate([qcols[2 * g], qcols[2 * g + 1]], axis=0)
            for e in range(2):
                i = g * 2 + e
                s_cur = jnp.where(mask_cur, _dot_nt(qst, k_var[i]), NEG_INF)
                s_prev = jnp.where(mask_prev, _dot_nt(qst, k_prev[i]), NEG_INF)
                yield
                sink_col = jnp.where(first_rows, sinks_ref[4 * g + e], sinks_ref[4 * g + 2 + e])
                p_cur, p_prev, es = _softmax_parts(s_cur, s_prev, sink_col)
                yield
                r = _dot(p_cur.astype(BF16), v_var[i]) + _dot(p_prev.astype(BF16), v_prev[i])
                res.append(r + _sum_half(e, es, 2 * BLK))
                yield
        acols = _assemble_attn(res, BLK)
        acols = _gated_rmsnorm_cols(acols, attn_g, zc(C_GA, ATTN_WIDTH), ATTN_WIDTH)
        for c in range(4):
            mix_scr[q, slot, :, c * LANES:(c + 1) * LANES] = acols[c].astype(BF16)
        yield

    def hgrn_part(q, j):
        slot = j % 2

        def zc(c0, width):
            return z_scr[q, j, :, c0:c0 + width]
        lf = _log_decay(zc(C_HF, HG_KW), log_lb, log1m_lb)
        kin = 1.0 - jnp.exp(lf)
        lf_hi, lf_lo = _split_bf16(lf)
        m_low = mlow_ref[0:LOW_LEVELS * BLK, :]
        m_cum = mlow_ref[LOW_LEVELS * BLK:(LOW_LEVELS + 1) * BLK, :]
        eg = jnp.concatenate([_dot(m_low, lf_hi), _dot(m_cum, lf_hi) + _dot(m_cum, lf_lo)], axis=0)
        yield
        for h in range(HG_HEADS):
            hs = slice(h * LANES, (h + 1) * LANES)
            q_h = zc(C_HQ + h * LANES, LANES)
            k_h = kin[:, hs]
            v_h = zc(C_HI + h * LANES, LANES).astype(BF16)
            e_low = [eg[l * BLK:(l + 1) * BLK, hs] for l in range(LOW_LEVELS)]
            g_h = eg[LOW_LEVELS * BLK:(LOW_LEVELS + 1) * BLK, hs]
            out = {}
            yield from _hgrn_scores_steps(q_h, k_h, e_low, g_h, lvl, PROMPT_LEVELS, out)
            st = st_scr[q, h]
            o_h = (_dot(out["a"].astype(BF16), v_h)
                   + _dot_nt((q_h * jnp.exp(g_h)).astype(BF16), st.astype(BF16)))
            g_last = g_h[BLK - 1:BLK, :]
            kd = (k_h * jnp.exp(g_last - g_h)).astype(BF16)
            st_scr[q, h] = st * jnp.exp(g_last) + _dot_tn(v_h, kd)
            yield
            gt = zc(C_GH + h * LANES, LANES)
            inv = lax.rsqrt(jnp.mean(o_h * o_h, axis=-1, keepdims=True) + NORM_EPS)
            mix_scr[q, slot, :, ATTN_WIDTH + h * LANES:ATTN_WIDTH + (h + 1) * LANES] = (
                o_h * inv * hg_g[:, hs] * _silu(gt)).astype(BF16)
            yield

    def stage_mix(j):
        def one(q):
            parts = (attn_part, hgrn_part) if q % 2 == 0 else (hgrn_part, attn_part)
            for part in parts:
                yield from part(q, j)
        gens = [one(q) for q in range(PROMPT_SEQS)]
        live = list(gens)
        while live:
            for g in list(live):
                try:
                    next(g)
                    yield
                except StopIteration:
                    live.remove(g)

    mix_yields = PROMPT_SEQS * (2 + 3 * 4 + 2 + HG_HEADS * (PROMPT_LEVELS + 2))

    @pl.when(jnp.logical_and(pl.program_id(0) == 0, t == 0))
    def _():
        for q in range(PROMPT_SEQS):
            for th in stage_in(q, 0):
                th()

    def both(stage, j):
        lists = [stage(q, j) for q in range(PROMPT_SEQS)]
        return [th for group in zip(*lists) for th in group]

    for j in range(nblk):
        nxt = both(stage_in, j // IN_GROUP + 1)
        part = j % IN_GROUP
        per = -(-len(nxt) // IN_GROUP)
        side = nxt[part * per:(part + 1) * per]
        if j >= 1:
            side = side + both(stage_out, j - 1)
        _interleave(stage_mix(j), mix_yields, side)
    for th in both(stage_out, nblk - 1):
        th()

    @pl.when(t == pl.num_programs(1) - 1)
    def _():
        for q in range(PROMPT_SEQS):
            for h in range(HG_HEADS):
                sfin_ref[q, h] = st_scr[q, h].T


def _sample_kernel(sinks_ref, x_ref, p_ref, ck_ref, cv_ref, s0_ref, cq_ref, sq_ref, ckt_ref, skt_ref,
                   lvl_ref, mlow_ref, seg_ref, w_in_ref, w_tail_ref, w_out_ref, w_pg_ref, w_pp_ref, v512_ref,
                   v1024_ref,
                   y_ref, nk_ref, nv_ref, ns_ref,
                   y_scr, qm_scr, attn_scr, mix_scr):
    layer = pl.program_id(0)
    blk = pl.program_id(1)
    rows = SAMPLE_SEQS * 8
    yrows = pl.ds(pl.multiple_of(blk * rows, rows), rows)

    @pl.when(layer == 0)
    def _():
        y_scr[yrows, :] = x_ref[...].reshape(rows, D_MODEL)

    x = y_scr[yrows, :]
    xb = x.astype(BF16)
    z = jnp.concatenate([_chunked_dot(xb, w_in_ref, IN_CHUNKS), _dot(xb, w_tail_ref[0])], axis=1)
    lvl = lvl_ref[...]
    attn_g = v512_ref[0:1, :]
    log_lb = v512_ref[1:2, :]
    log1m_lb = v512_ref[2:3, :]
    hg_g = v512_ref[3:4, :]

    qcols = [_rope(z[:, C_Q + c * LANES:C_Q + (c + 1) * LANES], cq_ref[...], sq_ref[...]) for c in range(4)]
    k_rot = _rope(z[:, C_K:C_K + KV_WIDTH], ckt_ref[...], skt_ref[...])
    v_new = z[:, C_V:C_V + KV_WIDTH]
    k_new_b = k_rot.astype(BF16)
    v_new_b = v_new.astype(BF16)
    lo2 = _lane_iota(2 * rows) < HEAD_DIM
    for g in range(2):
        xg = jnp.concatenate([qcols[2 * g], qcols[2 * g + 1]], axis=0)
        xsw = pltpu.roll(xg, HEAD_DIM, 1)
        keep = lo2 if g == 0 else jnp.logical_not(lo2)
        for e in range(2):
            qm = jnp.where(keep, xg if e == g else xsw, 0.0)
            for cc in range(2):
                r0 = ((g * 2 + e) * 2 + cc) * 8
                for s in range(SAMPLE_SEQS):
                    qm_scr[s, r0:r0 + 8, :] = qm[cc * rows + s * 8:cc * rows + s * 8 + 8, :]

    lf = _log_decay(z[:, C_HF:C_HF + HG_KW], log_lb, log1m_lb)
    kin = 1.0 - jnp.exp(lf)
    lf_hi, lf_lo = _split_bf16(lf)
    eg = _dot(mlow_ref[...], lf_hi) + _dot(mlow_ref[...], lf_lo)
    dec_t = jnp.exp(_dot_tn(lf_hi, seg_ref[...]) + _dot_tn(lf_lo, seg_ref[...]))
    o_intra, qe, kd, vh = [], [], [], []
    for h in range(HG_HEADS):
        hs = slice(h * LANES, (h + 1) * LANES)
        q_h = z[:, C_HQ + h * LANES:C_HQ + (h + 1) * LANES]
        k_h = kin[:, hs]
        v_h = z[:, C_HI + h * LANES:C_HI + (h + 1) * LANES]
        e_low = [eg[l * rows:(l + 1) * rows, hs] for l in range(LOW_LEVELS)]
        g_h = eg[LOW_LEVELS * rows:(LOW_LEVELS + 1) * rows, hs]
        tot_h = eg[(LOW_LEVELS + 1) * rows:(LOW_LEVELS + 2) * rows, hs]
        a = _hgrn_scores(q_h, k_h, e_low, g_h, lvl, LOW_LEVELS)
        o_intra.append(_dot(a.astype(BF16), v_h.astype(BF16)))
        qe.append(q_h * jnp.exp(g_h))
        kd.append(k_h * jnp.exp(tot_h - g_h))
        vh.append(v_h)

    grp = 4
    grows = grp * 64
    rid = lax.broadcasted_iota(jnp.int32, (grows, 1), 0)
    r8 = (rid >> 3) & 7
    sink_col = jnp.zeros((grows, 1), F32)
    for idx in range(8):
        g, e, cc = idx // 4, (idx // 2) % 2, idx % 2
        sink_col = jnp.where(r8 == idx, sinks_ref[layer * ATTN_HEADS + 4 * g + 2 * cc + e], sink_col)
    row = lax.broadcasted_iota(jnp.int32, (grows, 2 * WINDOW), 0)
    col = lax.broadcasted_iota(jnp.int32, (grows, 2 * WINDOW), 1)
    tok = row & 7
    new_col = col - WINDOW
    mask_cache = jnp.logical_and(col < WINDOW, col > tok)
    lo8 = _lane_iota(8) < HEAD_DIM
    o_inter = [[None] * SAMPLE_SEQS for _ in range(HG_HEADS)]
    for gi in range(SAMPLE_SEQS // grp):
        seqs = range(gi * grp, (gi + 1) * grp)
        kcs = {s: ck_ref[s] for s in seqs}
        vcs = {s: cv_ref[s] for s in seqs}
        sc = [_dot_nt(qm_scr[s].astype(BF16),
                      jnp.concatenate([kcs[s].astype(BF16), k_new_b], axis=0)) for s in seqs]
        for s in seqs:
            nk_ref[s, 0:WINDOW - 8, :] = kcs[s][8:WINDOW, :]
            nk_ref[s, WINDOW - 8:WINDOW, :] = k_rot[s * 8:s * 8 + 8, :]
            nv_ref[s, 0:WINDOW - 8, :] = vcs[s][8:WINDOW, :]
            nv_ref[s, WINDOW - 8:WINDOW, :] = v_new[s * 8:s * 8 + 8, :]
        mask_new = jnp.logical_and(jnp.logical_and(col >= WINDOW, (new_col >> 3) == (row >> 6) + gi * grp),
                                   (new_col & 7) <= tok)
        s_all = jnp.where(jnp.logical_or(mask_cache, mask_new), jnp.concatenate(sc, axis=0), NEG_INF)
        m = jnp.maximum(jnp.max(s_all, axis=-1, keepdims=True), sink_col)
        p = jnp.exp(s_all - m)
        den = jnp.sum(p, axis=-1, keepdims=True) + jnp.exp(sink_col - m)
        pb = p.astype(BF16)
        o_all = jnp.concatenate(
            [_dot(pb[i * 64:(i + 1) * 64, :], jnp.concatenate([vcs[s].astype(BF16), v_new_b], axis=0))
             for i, s in enumerate(seqs)], axis=0) / den
        o_sw = pltpu.roll(o_all, HEAD_DIM, 1)
        for i, s in enumerate(seqs):
            for c in range(4):
                g, cc = c // 2, c % 2
                ra = i * 64 + ((g * 2 + 0) * 2 + cc) * 8
                rb = i * 64 + ((g * 2 + 1) * 2 + cc) * 8
                part0 = (o_all if g == 0 else o_sw)[ra:ra + 8, :]
                part1 = (o_all if g == 1 else o_sw)[rb:rb + 8, :]
                attn_scr[c, s * 8:s * 8 + 8, :] = jnp.where(lo8, part0, part1)
        s0s = {(s, h): s0_ref[s, h] for s in seqs for h in range(HG_HEADS)}
        for s in seqs:
            for h in range(HG_HEADS):
                o_inter[h][s] = _dot(qe[h][s * 8:s * 8 + 8, :].astype(BF16), s0s[(s, h)].astype(BF16))
        upd = {(s, h): _dot_tn(kd[h][s * 8:s * 8 + 8, :].astype(BF16), vh[h][s * 8:s * 8 + 8, :].astype(BF16))
               for s in seqs for h in range(HG_HEADS)}
        for s in seqs:
            for h in range(HG_HEADS):
                dcol = dec_t[h * LANES:(h + 1) * LANES, s * 8:s * 8 + 1]
                ns_ref[s, h] = dcol * s0s[(s, h)] + upd[(s, h)]

    acols = _gated_rmsnorm_cols([attn_scr[c] for c in range(4)], attn_g, z[:, C_GA:C_GA + ATTN_WIDTH],
                                ATTN_WIDTH)
    for c in range(4):
        mix_scr[:, c * LANES:(c + 1) * LANES] = acols[c].astype(BF16)
    for h in range(HG_HEADS):
        hs = slice(h * LANES, (h + 1) * LANES)
        o_h = o_intra[h] + jnp.concatenate(o_inter[h], axis=0)
        gt = z[:, C_GH + h * LANES:C_GH + (h + 1) * LANES]
        inv = lax.rsqrt(jnp.mean(o_h * o_h, axis=-1, keepdims=True) + NORM_EPS)
        mix_scr[:, ATTN_WIDTH + h * LANES:ATTN_WIDTH + (h + 1) * LANES] = (
            o_h * inv * hg_g[:, hs] * _silu(gt)).astype(BF16)

    hpre = DN_ALPHA * x + _chunked_dot(mix_scr[...], w_out_ref, OUT_CHUNKS)
    hn = _layernorm(hpre, v1024_ref[0:1, :], v1024_ref[1:2, :])
    gate = _sigmoid(_chunked_dot(hn.astype(BF16), w_pg_ref, OUT_CHUNKS))
    pp = _chunked_dot(p_ref[...].reshape(rows, PLE_DIM).astype(BF16), w_pp_ref, OUT_CHUNKS)
    y = hn + gate * pp
    y_scr[yrows, :] = y
    y_ref[...] = y.reshape(SAMPLE_SEQS, 8, D_MODEL)


def _level_matrix():
    t = np.arange(BLK)[:, None]
    s = np.arange(BLK)[None, :]
    x = t ^ s
    lv = np.floor(np.log2(np.maximum(x, 1))).astype(np.int32)
    return np.where(t > s, lv, np.where(t == s, -1, -2)).astype(np.int32)


def _level_exponent_matrix(l):
    m = np.zeros((BLK, BLK), np.float32)
    b, h = 2 << l, 1 << l
    for t in range(BLK):
        mid = t - t % b + h
        if t >= mid:
            m[t, mid:t + 1] = 1.0
        else:
            m[t, t + 1:mid] = 1.0
    return m


def _mask_matrices(seq_rows):
    blocks = [_level_exponent_matrix(l) for l in range(LOW_LEVELS)]
    t = np.arange(BLK)[:, None]
    s = np.arange(BLK)[None, :]
    same = (t // seq_rows) == (s // seq_rows)
    blocks.append((same & (s <= t)).astype(np.float32))
    if seq_rows < BLK:
        blocks.append(same.astype(np.float32))
    return np.concatenate(blocks, axis=0)


def _rope_tables(pos, scale):
    half = HEAD_DIM // 2
    inv = jnp.exp(-math.log(ROPE_THETA) * jnp.arange(half, dtype=F32) * 2.0 / HEAD_DIM)
    ang = pos.astype(F32)[:, None] * inv[None, :]
    cos = jnp.cos(ang) * scale
    sin = jnp.sin(ang) * scale
    return jnp.tile(cos, (1, 4)), jnp.concatenate([-sin, sin, -sin, sin], axis=1)


def _column_chunks(w, width):
    d, k, n = w.shape
    return w.astype(BF16).reshape(d, k, n // width, width).transpose(0, 2, 1, 3)


def _const_spec(shape, layer=None):
    if layer is None:
        return pl.BlockSpec(shape, lambda *_: (0,) * len(shape))
    return pl.BlockSpec((None,) + shape, lambda *_: (layer,) + (0,) * len(shape))


def _prompt_layer(i, x, p_all, tables, lvl, mlow, weights, sinks, v512, v1024):
    B, T, _ = x.shape
    nt = T // PROMPT_TILE
    nb = B // PROMPT_SEQS
    tab_spec = pl.BlockSpec((PROMPT_TILE, LANES), lambda b, t: (t, 0))
    nblk = PROMPT_TILE // BLK
    assert nblk % IN_GROUP == 0 and nblk % 2 == 0

    def next_first_group(b, t):
        flat = jnp.minimum(b * nt + t + 1, nb * nt - 1)
        return (flat // nt, (flat % nt) * (nblk // IN_GROUP), 0)

    def wspec(shape):
        return pl.BlockSpec((None,) + shape, lambda b, t: (i,) + (0,) * len(shape),
                            pipeline_mode=pl.Buffered(1))
    in_specs = [
        pl.BlockSpec(memory_space=pltpu.SMEM),
        pl.BlockSpec((PROMPT_SEQS, PROMPT_TILE, D_MODEL), lambda b, t: (b, t, 0)),
        pl.BlockSpec((PROMPT_SEQS, IN_GROUP * BLK, D_MODEL), next_first_group),
        pl.BlockSpec((None, PROMPT_SEQS, PROMPT_TILE, PLE_DIM), lambda b, t: (i, b, t, 0)),
        tab_spec, tab_spec, tab_spec, tab_spec,
        _const_spec((BLK, BLK)),
        _const_spec(mlow.shape),
        wspec((IN_CHUNKS, D_MODEL, CHUNK_COLS)),
        wspec((1, D_MODEL, IN_TAIL)),
        wspec((OUT_CHUNKS, MIX_WIDTH, CHUNK_COLS)),
        wspec((OUT_CHUNKS, D_MODEL, CHUNK_COLS)),
        wspec((OUT_CHUNKS, PLE_DIM, CHUNK_COLS)),
        _const_spec((4, ATTN_WIDTH), i),
        _const_spec((2, D_MODEL), i),
    ]
    out_shape = [
        jax.ShapeDtypeStruct((B, T, D_MODEL), F32),
        jax.ShapeDtypeStruct((B, WINDOW, KV_WIDTH), F32),
        jax.ShapeDtypeStruct((B, WINDOW, KV_WIDTH), F32),
        jax.ShapeDtypeStruct((B, HG_HEADS, HG_DK, HG_DV), F32),
    ]
    out_specs = [
        pl.BlockSpec((PROMPT_SEQS, PROMPT_TILE, D_MODEL), lambda b, t: (b, t, 0)),
        pl.BlockSpec((PROMPT_SEQS, WINDOW, KV_WIDTH), lambda b, t: (b, 0, 0)),
        pl.BlockSpec((PROMPT_SEQS, WINDOW, KV_WIDTH), lambda b, t: (b, 0, 0)),
        pl.BlockSpec((PROMPT_SEQS, HG_HEADS, HG_DK, HG_DV), lambda b, t: (b, 0, 0, 0)),
    ]
    scratch = [
        pltpu.VMEM((PROMPT_SEQS, HG_HEADS, HG_DV, HG_DK), F32),
        pltpu.VMEM((PROMPT_SEQS, 4, BLK, LANES), BF16),
        pltpu.VMEM((PROMPT_SEQS, 4, BLK, LANES), BF16),
        pltpu.VMEM((PROMPT_SEQS, nblk, BLK, IN_COLS), F32),
        pltpu.VMEM((PROMPT_SEQS, 2, BLK, MIX_WIDTH), BF16),
    ]
    return pl.pallas_call(
        _prompt_kernel,
        grid=(nb, nt),
        in_specs=in_specs,
        out_specs=out_specs,
        out_shape=out_shape,
        scratch_shapes=scratch,
        compiler_params=pltpu.CompilerParams(
            dimension_semantics=("arbitrary", "arbitrary"), vmem_limit_bytes=VMEM_LIMIT),
        name=f"prompt_layer{i}",
    )(sinks, x, x, p_all, *tables, lvl, mlow, *weights, v512, v1024)


def _sample_call(x, p_all, ck_all, cv_all, s0_all, tables, lvl, mlow, seg, weights, sinks, v512, v1024):
    B, T, _ = x.shape
    rows = SAMPLE_SEQS * T

    def wspec(shape):
        return pl.BlockSpec((None,) + shape, lambda l, b: (l,) + (0,) * len(shape),
                            pipeline_mode=pl.Buffered(1))
    tab_spec = _const_spec((rows, LANES))
    in_specs = [
        pl.BlockSpec(memory_space=pltpu.SMEM),
        pl.BlockSpec((SAMPLE_SEQS, T, D_MODEL), lambda l, b: (jnp.where(l == 0, b, 0), 0, 0)),
        pl.BlockSpec((None, SAMPLE_SEQS, T, PLE_DIM), lambda l, b: (l, b, 0, 0)),
        pl.BlockSpec((None, SAMPLE_SEQS, WINDOW, KV_WIDTH), lambda l, b: (l, b, 0, 0)),
        pl.BlockSpec((None, SAMPLE_SEQS, WINDOW, KV_WIDTH), lambda l, b: (l, b, 0, 0)),
        pl.BlockSpec((None, SAMPLE_SEQS, HG_HEADS, HG_DK, HG_DV), lambda l, b: (l, b, 0, 0, 0)),
        tab_spec, tab_spec, tab_spec, tab_spec,
        _const_spec((BLK, BLK)),
        _const_spec(mlow.shape),
        _const_spec((BLK, LANES)),
        wspec((IN_CHUNKS, D_MODEL, CHUNK_COLS)),
        wspec((1, D_MODEL, IN_TAIL)),
        wspec((OUT_CHUNKS, MIX_WIDTH, CHUNK_COLS)),
        wspec((OUT_CHUNKS, D_MODEL, CHUNK_COLS)),
        wspec((OUT_CHUNKS, PLE_DIM, CHUNK_COLS)),
        pl.BlockSpec((None, 4, ATTN_WIDTH), lambda l, b: (l, 0, 0)),
        pl.BlockSpec((None, 2, D_MODEL), lambda l, b: (l, 0, 0)),
    ]
    out_shape = [
        jax.ShapeDtypeStruct((B, T, D_MODEL), F32),
        jax.ShapeDtypeStruct((DEPTH, B, WINDOW, KV_WIDTH), F32),
        jax.ShapeDtypeStruct((DEPTH, B, WINDOW, KV_WIDTH), F32),
        jax.ShapeDtypeStruct((DEPTH, B, HG_HEADS, HG_DK, HG_DV), F32),
    ]
    out_specs = [
        pl.BlockSpec((SAMPLE_SEQS, T, D_MODEL), lambda l, b: (jnp.where(l == DEPTH - 1, b, 0), 0, 0)),
        pl.BlockSpec((None, SAMPLE_SEQS, WINDOW, KV_WIDTH), lambda l, b: (l, b, 0, 0)),
        pl.BlockSpec((None, SAMPLE_SEQS, WINDOW, KV_WIDTH), lambda l, b: (l, b, 0, 0)),
        pl.BlockSpec((None, SAMPLE_SEQS, HG_HEADS, HG_DK, HG_DV), lambda l, b: (l, b, 0, 0, 0)),
    ]
    scratch = [
        pltpu.VMEM((B * T, D_MODEL), F32),
        pltpu.VMEM((SAMPLE_SEQS, 64, LANES), F32),
        pltpu.VMEM((4, rows, LANES), F32),
        pltpu.VMEM((rows, MIX_WIDTH), BF16),
    ]
    return pl.pallas_call(
        _sample_kernel,
        grid=(DEPTH, B // SAMPLE_SEQS),
        in_specs=in_specs,
        out_specs=out_specs,
        out_shape=out_shape,
        scratch_shapes=scratch,
        compiler_params=pltpu.CompilerParams(
            dimension_semantics=("arbitrary", "arbitrary"), vmem_limit_bytes=VMEM_LIMIT),
        name="sample_layers",
    )(sinks, x, p_all, ck_all, cv_all, s0_all, *tables, lvl, mlow, seg, *weights, v512, v1024)


def kernel(x_prompt, x_sample, cache_k_win, cache_v_win, state_hgrn, p_prompt, p_sample, w_in, attn_sinks,
           attn_norm_g, hg_lb_logits, hg_norm_g, w_out, ln_g, ln_b, w_ple_proj, w_ple_gate):
    B, T, _ = x_prompt.shape
    SB, ST, _ = x_sample.shape
    assert T % PROMPT_TILE == 0 and B % PROMPT_SEQS == 0
    assert SB % SAMPLE_SEQS == 0 and ST == 8 and SAMPLE_SEQS * ST == BLK
    assert cache_k_win.shape[2] == WINDOW

    cs = jnp.cumsum(jax.nn.softmax(hg_lb_logits.astype(F32), axis=0), axis=0)
    lbs = cs - cs[:1]
    v512 = jnp.stack([attn_norm_g.astype(F32), jnp.log(lbs), jnp.log1p(-lbs), hg_norm_g.astype(F32)], axis=1)
    v1024 = jnp.stack([ln_g.astype(F32), ln_b.astype(F32)], axis=1)
    n_main = IN_CHUNKS * CHUNK_COLS
    weights = (_column_chunks(w_in[:, :, :n_main], CHUNK_COLS), _column_chunks(w_in[:, :, n_main:], IN_TAIL),
               _column_chunks(w_out, CHUNK_COLS), _column_chunks(w_ple_gate, CHUNK_COLS),
               _column_chunks(w_ple_proj, CHUNK_COLS))
    sinks = attn_sinks.astype(F32)

    scale = HEAD_DIM ** -0.5
    pos_p = jnp.arange(T, dtype=jnp.int32)
    pos_s = jnp.tile(PAST_LEN + jnp.arange(ST, dtype=jnp.int32), SAMPLE_SEQS)
    tab_p = _rope_tables(pos_p, scale) + _rope_tables(pos_p, 1.0)
    tab_s = _rope_tables(pos_s, scale) + _rope_tables(pos_s, 1.0)
    lvl = jnp.asarray(_level_matrix())
    mlow_p = jnp.asarray(_mask_matrices(BLK), dtype=BF16)
    mlow_s = jnp.asarray(_mask_matrices(ST), dtype=BF16)
    seg = jnp.asarray((np.arange(BLK)[:, None] // ST) == (np.arange(LANES)[None, :] // ST), dtype=BF16)

    ck = cache_k_win.reshape(DEPTH, SB, WINDOW, KV_WIDTH)
    cv = cache_v_win.reshape(DEPTH, SB, WINDOW, KV_WIDTH)
    ys, ks, vs, ss = _sample_call(x_sample, p_sample, ck, cv, state_hgrn, tab_s, lvl, mlow_s, seg, weights,
                                  sinks.reshape(DEPTH * ATTN_HEADS), v512, v1024)

    yp = x_prompt
    kp_l, vp_l, sp_l = [], [], []
    for i in range(DEPTH):
        yp, kp, vp, sp = _prompt_layer(i, yp, p_prompt, tab_p, lvl, mlow_p, weights, sinks[i], v512, v1024)
        kp_l.append(kp); vp_l.append(vp); sp_l.append(sp)

    def kv5(a, b):
        return a.reshape(DEPTH, b, WINDOW, KV_HEADS, HEAD_DIM)

    return (yp, ys, kv5(jnp.stack(kp_l), B), kv5(jnp.stack(vp_l), B), jnp.stack(sp_l),
            kv5(ks, SB), kv5(vs, SB), ss)
```

```python
import math

import numpy as np
import jax
import jax.numpy as jnp
from jax import lax
from jax.experimental import pallas as pl
from jax.experimental.pallas import tpu as pltpu

D_MODEL = 1024
DEPTH = 4
PAST_LEN = 8192
ATTN_HEADS = 8
KV_HEADS = 2
HEAD_DIM = 64
ATTN_WIDTH = ATTN_HEADS * HEAD_DIM
KV_WIDTH = KV_HEADS * HEAD_DIM
WINDOW = 128
ROPE_THETA = 10000.0
HG_HEADS = 4
HG_DK = 128
HG_DV = 128
HG_KW = HG_HEADS * HG_DK
HG_VW = HG_HEADS * HG_DV
MIX_WIDTH = ATTN_WIDTH + HG_VW
IN_COLS = 2 * ATTN_WIDTH + 2 * KV_WIDTH + 2 * HG_KW + 2 * HG_VW
PLE_DIM = 256
DN_ALPHA = (2 * DEPTH) ** 0.25
NORM_EPS = 1e-5
NEG_INF = -1e30
LOG2E = 1.4426950408889634

C_Q = 0
C_K = C_Q + ATTN_WIDTH
C_V = C_K + KV_WIDTH
C_GA = C_V + KV_WIDTH
C_HQ = C_GA + ATTN_WIDTH
C_HF = C_HQ + HG_KW
C_HI = C_HF + HG_KW
C_GH = C_HI + HG_VW

LANES = 128
MXU_COLS = 256
CHUNK_COLS = 2 * MXU_COLS
BLK = 128
PROMPT_TILE = 512
PROMPT_SEQS = 1
IN_GROUP = 2
SAMPLE_SEQS = 16
LOW_LEVELS = 3
PROMPT_LEVELS = 7
VMEM_LIMIT = 56 * 1024 * 1024
IN_CHUNKS = IN_COLS // CHUNK_COLS
IN_TAIL = IN_COLS - IN_CHUNKS * CHUNK_COLS
OUT_CHUNKS = D_MODEL // CHUNK_COLS

F32 = jnp.float32
BF16 = jnp.bfloat16


def _dot(a, b):
    return jnp.dot(a, b, preferred_element_type=F32)


def _dot_nt(a, b):
    return lax.dot_general(a, b, (((1,), (1,)), ((), ())), preferred_element_type=F32)


def _dot_tn(a, b):
    return lax.dot_general(a, b, (((0,), (0,)), ((), ())), preferred_element_type=F32)


def _split_bf16(x):
    hi = x.astype(BF16)
    lo = (x - hi.astype(F32)).astype(BF16)
    return hi, lo


def _sigmoid(x):
    return 0.5 * jnp.tanh(0.5 * x) + 0.5


def _silu(x):
    h = 0.5 * x
    return h * jnp.tanh(h) + h


def _lane_iota(rows):
    return lax.broadcasted_iota(jnp.int32, (rows, LANES), 1)


def _rope(x, cos_t, sin_t):
    lane = _lane_iota(x.shape[0])
    first_half = (lane & 32) == 0
    swapped = jnp.where(first_half, pltpu.roll(x, 96, 1), pltpu.roll(x, 32, 1))
    return x * cos_t + swapped * sin_t


def _kv_variants(a, fill):
    lane = _lane_iota(a.shape[0])
    lo = lane < HEAD_DIM
    sw = pltpu.roll(a, HEAD_DIM, 1)
    f = jnp.full_like(a, fill)
    out = [jnp.where(lo, a, f), jnp.where(lo, f, sw), jnp.where(lo, sw, f), jnp.where(lo, f, a)]
    return [o.astype(BF16) for o in out]


def _sum_half(e, es, rows):
    lane = _lane_iota(rows)
    sum_half = (lane >= HEAD_DIM) if e == 0 else (lane < HEAD_DIM)
    return jnp.where(sum_half, es, 0.0)


def _assemble_attn(res, rows):
    lane = _lane_iota(rows)
    lo = lane < HEAD_DIM
    cols = []
    for c in range(4):
        g, cc = c // 2, c % 2
        r0 = res[g * 2 + 0][cc * rows:(cc + 1) * rows]
        r1 = res[g * 2 + 1][cc * rows:(cc + 1) * rows]
        num = jnp.where(lo, r0, r1)
        den = pltpu.roll(jnp.where(lo, r1, r0), HEAD_DIM, 1)
        cols.append(num / den)
    return cols


def _log_decay(hf, log_lb, log1m_lb):
    ls = jnp.minimum(hf, 0.0) - jnp.log(1.0 + jnp.exp(-jnp.abs(hf)))
    b = log1m_lb + ls
    return jnp.maximum(log_lb, b) + jnp.log(1.0 + jnp.exp(-jnp.abs(log_lb - b)))


def _level_operand(l, q_h, k_h, e_low, g_h):
    if l < LOW_LEVELS:
        rows = lax.broadcasted_iota(jnp.int32, (BLK, LANES), 0)
        upper = ((rows >> l) & 1) == 1
        return jnp.where(upper, q_h, k_h) * jnp.exp2(e_low[l])
    b, h = 2 << l, 1 << l
    pieces = []
    for i in range(BLK // b):
        lower, upper = slice(i * b, i * b + h), slice(i * b + h, (i + 1) * b)
        g_mid = g_h[i * b + h - 1:i * b + h, :]
        pieces.append(k_h[lower, :] * jnp.exp2(g_mid - g_h[lower, :]))
        pieces.append(q_h[upper, :] * jnp.exp2(g_h[upper, :] - g_mid))
    return jnp.concatenate(pieces, axis=0)


def _hgrn_scores_steps(q_h, k_h, e_low, g_h, lvl, levels, out):
    a = jnp.zeros((BLK, BLK), F32)
    for l in range(levels):
        u = _level_operand(l, q_h, k_h, e_low, g_h).astype(BF16)
        a = jnp.where(lvl == l, _dot_nt(u, u), a)
        yield
    diag = jnp.sum(q_h * k_h, axis=-1, keepdims=True)
    out["a"] = jnp.where(lvl == -1, diag, a)


def _hgrn_scores(q_h, k_h, e_low, g_h, lvl, levels):
    out = {}
    for _ in _hgrn_scores_steps(q_h, k_h, e_low, g_h, lvl, levels, out):
        pass
    return out["a"]


def _gated_rmsnorm_cols(cols, gain_row, gate, width):
    ss = None
    for c in cols:
        s = jnp.sum(c * c, axis=-1, keepdims=True)
        ss = s if ss is None else ss + s
    inv = lax.rsqrt(ss * (1.0 / width) + NORM_EPS)
    out = []
    for i, c in enumerate(cols):
        gt = gate[:, i * LANES:(i + 1) * LANES]
        out.append(c * inv * gain_row[:, i * LANES:(i + 1) * LANES] * _silu(gt))
    return out


def _layernorm(hpre, ln_g, ln_b):
    mu = jnp.mean(hpre, axis=-1, keepdims=True)
    cen = hpre - mu
    var = jnp.mean(cen * cen, axis=-1, keepdims=True)
    return cen * lax.rsqrt(var + NORM_EPS) * ln_g + ln_b


def _chunked_dot(a, w_ref, n):
    return jnp.concatenate([_dot(a, w_ref[c]) for c in range(n)], axis=1)


def _interleave(gen, n_yields, thunks):
    thunks = list(thunks)
    total, done, seen = len(thunks), 0, 0
    for _ in gen:
        seen += 1
        want = min(total, (seen * total + n_yields - 1) // n_yields)
        while done < want:
            thunks[done]()
            done += 1
    while done < total:
        thunks[done]()
        done += 1


def _prompt_kernel(sinks_ref, x_ref, xn_ref, p_ref, cq_ref, sq_ref, ck_ref, sk_ref, lvl_ref, mlow_ref,
                   w_in_ref, w_tail_ref, w_out_ref, w_pg_ref, w_pp_ref, v512_ref, v1024_ref,
                   y_ref, kk_ref, vk_ref, sfin_ref,
                   st_scr, kprev_scr, vprev_scr, z_scr, mix_scr):
    t = pl.program_id(1)
    nblk = PROMPT_TILE // BLK

    @pl.when(t == 0)
    def _():
        st_scr[...] = jnp.zeros_like(st_scr)
        kprev_scr[...] = jnp.zeros_like(kprev_scr)
        vprev_scr[...] = jnp.zeros_like(vprev_scr)

    lvl = lvl_ref[...]
    attn_g = v512_ref[0:1, :]
    log_lb = v512_ref[1:2, :]
    log1m_lb = v512_ref[2:3, :]
    hg_g = v512_ref[3:4, :]

    row = lax.broadcasted_iota(jnp.int32, (2 * BLK, BLK), 0)
    col = lax.broadcasted_iota(jnp.int32, (2 * BLK, BLK), 1)
    from_prev = col > (row & (BLK - 1))
    first_rows = lax.broadcasted_iota(jnp.int32, (2 * BLK, 1), 0) < BLK
    carry = [{} for _ in range(PROMPT_SEQS)]

    ngrp = nblk // IN_GROUP
    grp_rows = IN_GROUP * BLK

    def stage_in(q, g):
        cell = {}
        first = (g % ngrp) * IN_GROUP

        def chunk(c):
            def run():
                if "xb" not in cell:
                    src = xn_ref[q] if g == ngrp else x_ref[q, g * grp_rows:(g + 1) * grp_rows, :]
                    cell["xb"] = src.astype(BF16)
                if c < IN_CHUNKS:
                    cols, zz = slice(c * CHUNK_COLS, (c + 1) * CHUNK_COLS), _dot(cell["xb"], w_in_ref[c])
                else:
                    cols, zz = slice(IN_CHUNKS * CHUNK_COLS, IN_COLS), _dot(cell["xb"], w_tail_ref[0])
                for i in range(IN_GROUP):
                    z_scr[q, first + i, :, cols] = zz[i * BLK:(i + 1) * BLK, :]
            return run
        return [chunk(c) for c in range(IN_CHUNKS + 1)]

    def stage_out(q, j):
        slot, rows_j, cell = j % 2, slice(j * BLK, (j + 1) * BLK), {"m": [], "g": []}

        def mix_chunk(c):
            def run():
                cell["m"].append(_dot(mix_scr[q, slot], w_out_ref[c]))
                if c + 1 == OUT_CHUNKS:
                    hpre = DN_ALPHA * x_ref[q, rows_j, :] + jnp.concatenate(cell.pop("m"), axis=1)
                    h = _layernorm(hpre, v1024_ref[0:1, :], v1024_ref[1:2, :])
                    cell["h"] = h
                    cell["hb"] = h.astype(BF16)
            return run

        def gate_chunk(c):
            def run():
                cell["g"].append(_sigmoid(_dot(cell["hb"], w_pg_ref[c])))
            return run

        def finish():
            pb = p_ref[q, rows_j, :].astype(BF16)
            for c in range(OUT_CHUNKS):
                cs = slice(c * CHUNK_COLS, (c + 1) * CHUNK_COLS)
                y_ref[q, rows_j, cs] = cell["h"][:, cs] + cell["g"][c] * _dot(pb, w_pp_ref[c])
        return ([mix_chunk(c) for c in range(OUT_CHUNKS)] + [gate_chunk(c) for c in range(OUT_CHUNKS)]
                + [finish])

    def attn_part(q, j):
        slot, rows_j = j % 2, slice(j * BLK, (j + 1) * BLK)

        def zc(c0, width):
            return z_scr[q, j, :, c0:c0 + width]
        qcols = [_rope(zc(C_Q + c * LANES, LANES), cq_ref[rows_j, :], sq_ref[rows_j, :]).astype(BF16)
                 for c in range(4)]
        k_rot = _rope(zc(C_K, KV_WIDTH), ck_ref[rows_j, :], sk_ref[rows_j, :])
        v_new = zc(C_V, KV_WIDTH)
        yield
        if j == 0:
            k_prev = [kprev_scr[q, i] for i in range(4)]
            v_prev = [vprev_scr[q, i] for i in range(4)]
        else:
            k_prev, v_prev = carry[q]["k_var"], carry[q]["v_var"]
        k_var = _kv_variants(k_rot, 0.0)
        v_var = _kv_variants(v_new, 1.0)
        carry[q]["k_var"], carry[q]["v_var"] = k_var, v_var
        if j == nblk - 1:
            for i in range(4):
                kprev_scr[q, i] = k_var[i]
                vprev_scr[q, i] = v_var[i]
            kk_ref[q] = k_rot
            vk_ref[q] = v_new
        yield
        res = []
        for g in range(2):
            qst = jnp.concatenate([qcols[2 * g], qcols[2 * g + 1]], axis=0)
            for e in range(2):
                i = g * 2 + e
                s_prev = _dot_nt(qst, k_prev[i])
                if j == 0:
                    s_prev = jnp.where(t > 0, s_prev, NEG_INF)
                s = jnp.where(from_prev, s_prev, _dot_nt(qst, k_var[i]))
                yield
                sink_col = jnp.where(first_rows, sinks_ref[4 * g + e], sinks_ref[4 * g + 2 + e])
                m = jnp.maximum(jnp.max(s, axis=-1, keepdims=True), sink_col)
                p = jnp.exp(s - m)
                es = jnp.exp(sink_col - m)
                yield
                p_both = jnp.concatenate([jnp.where(from_prev, p, 0.0), jnp.where(from_prev, 0.0, p)], axis=1)
                r = _dot(p_both.astype(BF16), jnp.concatenate([v_prev[i], v_var[i]], axis=0))
                res.append(r + _sum_half(e, es, 2 * BLK))
                yield
        acols = _assemble_attn(res, BLK)
        acols = _gated_rmsnorm_cols(acols, attn_g, zc(C_GA, ATTN_WIDTH), ATTN_WIDTH)
        for c in range(4):
            mix_scr[q, slot, :, c * LANES:(c + 1) * LANES] = acols[c].astype(BF16)
        yield

    def hgrn_part(q, j):
        slot = j % 2

        def zc(c0, width):
            return z_scr[q, j, :, c0:c0 + width]
        lf = _log_decay(zc(C_HF, HG_KW), log_lb, log1m_lb) * LOG2E
        kin = 1.0 - jnp.exp2(lf)
        lf_hi, lf_lo = _split_bf16(lf)
        m_low = mlow_ref[0:LOW_LEVELS * BLK, :]
        m_cum = mlow_ref[LOW_LEVELS * BLK:(LOW_LEVELS + 1) * BLK, :]
        eg = jnp.concatenate([_dot(m_low, lf_hi), _dot(m_cum, lf_hi) + _dot(m_cum, lf_lo)], axis=0)
        yield
        for h in range(HG_HEADS):
            hs = slice(h * LANES, (h + 1) * LANES)
            q_h = zc(C_HQ + h * LANES, LANES)
            k_h = kin[:, hs]
            v_h = zc(C_HI + h * LANES, LANES).astype(BF16)
            e_low = [eg[l * BLK:(l + 1) * BLK, hs] for l in range(LOW_LEVELS)]
            g_h = eg[LOW_LEVELS * BLK:(LOW_LEVELS + 1) * BLK, hs]
            out = {}
            yield from _hgrn_scores_steps(q_h, k_h, e_low, g_h, lvl, PROMPT_LEVELS, out)
            st = st_scr[q, h]
            o_h = (_dot(out["a"].astype(BF16), v_h)
                   + _dot_nt((q_h * jnp.exp2(g_h)).astype(BF16), st.astype(BF16)))
            g_last = g_h[BLK - 1:BLK, :]
            kd = (k_h * jnp.exp2(g_last - g_h)).astype(BF16)
            st_scr[q, h] = st * jnp.exp2(g_last) + _dot_tn(v_h, kd)
            yield
            gt = zc(C_GH + h * LANES, LANES)
            inv = lax.rsqrt(jnp.mean(o_h * o_h, axis=-1, keepdims=True) + NORM_EPS)
            mix_scr[q, slot, :, ATTN_WIDTH + h * LANES:ATTN_WIDTH + (h + 1) * LANES] = (
                o_h * inv * hg_g[:, hs] * _silu(gt)).astype(BF16)
            yield

    def stage_mix(j):
        def one(q):
            parts = (attn_part, hgrn_part) if q % 2 == 0 else (hgrn_part, attn_part)
            for part in parts:
                yield from part(q, j)
        gens = [one(q) for q in range(PROMPT_SEQS)]
        live = list(gens)
        while live:
            for g in list(live):
                try:
                    next(g)
                    yield
                except StopIteration:
                    live.remove(g)

    mix_yields = PROMPT_SEQS * (2 + 3 * 4 + 2 + HG_HEADS * (PROMPT_LEVELS + 2))

    @pl.when(jnp.logical_and(pl.program_id(0) == 0, t == 0))
    def _():
        for q in range(PROMPT_SEQS):
            for th in stage_in(q, 0):
                th()

    def both(stage, j):
        lists = [stage(q, j) for q in range(PROMPT_SEQS)]
        return [th for group in zip(*lists) for th in group]

    for j in range(nblk):
        nxt = both(stage_in, j // IN_GROUP + 1)
        part = j % IN_GROUP
        per = -(-len(nxt) // IN_GROUP)
        side = nxt[part * per:(part + 1) * per]
        if j >= 1:
            side = side + both(stage_out, j - 1)
        _interleave(stage_mix(j), mix_yields, side)
    for th in both(stage_out, nblk - 1):
        th()

    @pl.when(t == pl.num_programs(1) - 1)
    def _():
        for q in range(PROMPT_SEQS):
            for h in range(HG_HEADS):
                sfin_ref[q, h] = st_scr[q, h].T


def _sample_kernel(sinks_ref, x_ref, p_ref, ck_ref, cv_ref, s0_ref, cq_ref, sq_ref, ckt_ref, skt_ref,
                   lvl_ref, mlow_ref, seg_ref, w_in_ref, w_tail_ref, w_out_ref, w_pg_ref, w_pp_ref, v512_ref,
                   v1024_ref,
                   y_ref, nk_ref, nv_ref, ns_ref,
                   y_scr, qm_scr, attn_scr, mix_scr):
    layer = pl.program_id(0)
    blk = pl.program_id(1)
    rows = SAMPLE_SEQS * 8
    yrows = pl.ds(pl.multiple_of(blk * rows, rows), rows)

    @pl.when(layer == 0)
    def _():
        y_scr[yrows, :] = x_ref[...].reshape(rows, D_MODEL)

    x = y_scr[yrows, :]
    xb = x.astype(BF16)
    z = jnp.concatenate([_chunked_dot(xb, w_in_ref, IN_CHUNKS), _dot(xb, w_tail_ref[0])], axis=1)
    lvl = lvl_ref[...]
    attn_g = v512_ref[0:1, :]
    log_lb = v512_ref[1:2, :]
    log1m_lb = v512_ref[2:3, :]
    hg_g = v512_ref[3:4, :]

    qcols = [_rope(z[:, C_Q + c * LANES:C_Q + (c + 1) * LANES], cq_ref[...], sq_ref[...]) for c in range(4)]
    k_rot = _rope(z[:, C_K:C_K + KV_WIDTH], ckt_ref[...], skt_ref[...])
    v_new = z[:, C_V:C_V + KV_WIDTH]
    k_new_b = k_rot.astype(BF16)
    v_new_b = v_new.astype(BF16)
    lo2 = _lane_iota(2 * rows) < HEAD_DIM
    for g in range(2):
        xg = jnp.concatenate([qcols[2 * g], qcols[2 * g + 1]], axis=0)
        xsw = pltpu.roll(xg, HEAD_DIM, 1)
        keep = lo2 if g == 0 else jnp.logical_not(lo2)
        for e in range(2):
            qm = jnp.where(keep, xg if e == g else xsw, 0.0)
            for cc in range(2):
                r0 = ((g * 2 + e) * 2 + cc) * 8
                for s in range(SAMPLE_SEQS):
                    qm_scr[s, r0:r0 + 8, :] = qm[cc * rows + s * 8:cc * rows + s * 8 + 8, :]

    lf = _log_decay(z[:, C_HF:C_HF + HG_KW], log_lb, log1m_lb) * LOG2E
    kin = 1.0 - jnp.exp2(lf)
    lf_hi, lf_lo = _split_bf16(lf)
    eg = _dot(mlow_ref[...], lf_hi) + _dot(mlow_ref[...], lf_lo)
    dec_t = jnp.exp2(_dot_tn(lf_hi, seg_ref[...]) + _dot_tn(lf_lo, seg_ref[...]))
    o_intra, qe, kd, vh = [], [], [], []
    for h in range(HG_HEADS):
        hs = slice(h * LANES, (h + 1) * LANES)
        q_h = z[:, C_HQ + h * LANES:C_HQ + (h + 1) * LANES]
        k_h = kin[:, hs]
        v_h = z[:, C_HI + h * LANES:C_HI + (h + 1) * LANES]
        e_low = [eg[l * rows:(l + 1) * rows, hs] for l in range(LOW_LEVELS)]
        g_h = eg[LOW_LEVELS * rows:(LOW_LEVELS + 1) * rows, hs]
        tot_h = eg[(LOW_LEVELS + 1) * rows:(LOW_LEVELS + 2) * rows, hs]
        a = _hgrn_scores(q_h, k_h, e_low, g_h, lvl, LOW_LEVELS)
        o_intra.append(_dot(a.astype(BF16), v_h.astype(BF16)))
        qe.append(q_h * jnp.exp2(g_h))
        kd.append(k_h * jnp.exp2(tot_h - g_h))
        vh.append(v_h)

    grp = 4
    grows = grp * 64
    rid = lax.broadcasted_iota(jnp.int32, (grows, 1), 0)
    r8 = (rid >> 3) & 7
    sink_col = jnp.zeros((grows, 1), F32)
    for idx in range(8):
        g, e, cc = idx // 4, (idx // 2) % 2, idx % 2
        sink_col = jnp.where(r8 == idx, sinks_ref[layer * ATTN_HEADS + 4 * g + 2 * cc + e], sink_col)
    row = lax.broadcasted_iota(jnp.int32, (grows, 2 * WINDOW), 0)
    col = lax.broadcasted_iota(jnp.int32, (grows, 2 * WINDOW), 1)
    tok = row & 7
    new_col = col - WINDOW
    mask_cache = jnp.logical_and(col < WINDOW, col > tok)
    lo8 = _lane_iota(8) < HEAD_DIM
    o_inter = [[None] * SAMPLE_SEQS for _ in range(HG_HEADS)]
    for gi in range(SAMPLE_SEQS // grp):
        seqs = range(gi * grp, (gi + 1) * grp)
        kcs = {s: ck_ref[s] for s in seqs}
        vcs = {s: cv_ref[s] for s in seqs}
        sc = [_dot_nt(qm_scr[s].astype(BF16),
                      jnp.concatenate([kcs[s].astype(BF16), k_new_b], axis=0)) for s in seqs]
        for s in seqs:
            nk_ref[s, 0:WINDOW - 8, :] = kcs[s][8:WINDOW, :]
            nk_ref[s, WINDOW - 8:WINDOW, :] = k_rot[s * 8:s * 8 + 8, :]
            nv_ref[s, 0:WINDOW - 8, :] = vcs[s][8:WINDOW, :]
            nv_ref[s, WINDOW - 8:WINDOW, :] = v_new[s * 8:s * 8 + 8, :]
        mask_new = jnp.logical_and(jnp.logical_and(col >= WINDOW, (new_col >> 3) == (row >> 6) + gi * grp),
                                   (new_col & 7) <= tok)
        s_all = jnp.where(jnp.logical_or(mask_cache, mask_new), jnp.concatenate(sc, axis=0), NEG_INF)
        m = jnp.maximum(jnp.max(s_all, axis=-1, keepdims=True), sink_col)
        p = jnp.exp(s_all - m)
        den = jnp.sum(p, axis=-1, keepdims=True) + jnp.exp(sink_col - m)
        pb = p.astype(BF16)
        o_all = jnp.concatenate(
            [_dot(pb[i * 64:(i + 1) * 64, :], jnp.concatenate([vcs[s].astype(BF16), v_new_b], axis=0))
             for i, s in enumerate(seqs)], axis=0) / den
        o_sw = pltpu.roll(o_all, HEAD_DIM, 1)
        for i, s in enumerate(seqs):
            for c in range(4):
                g, cc = c // 2, c % 2
                ra = i * 64 + ((g * 2 + 0) * 2 + cc) * 8
                rb = i * 64 + ((g * 2 + 1) * 2 + cc) * 8
                part0 = (o_all if g == 0 else o_sw)[ra:ra + 8, :]
                part1 = (o_all if g == 1 else o_sw)[rb:rb + 8, :]
                attn_scr[c, s * 8:s * 8 + 8, :] = jnp.where(lo8, part0, part1)
        s0s = {(s, h): s0_ref[s, h] for s in seqs for h in range(HG_HEADS)}
        for s in seqs:
            for h in range(HG_HEADS):
                o_inter[h][s] = _dot(qe[h][s * 8:s * 8 + 8, :].astype(BF16), s0s[(s, h)].astype(BF16))
        upd = {(s, h): _dot_tn(kd[h][s * 8:s * 8 + 8, :].astype(BF16), vh[h][s * 8:s * 8 + 8, :].astype(BF16))
               for s in seqs for h in range(HG_HEADS)}
        for s in seqs:
            for h in range(HG_HEADS):
                dcol = dec_t[h * LANES:(h + 1) * LANES, s * 8:s * 8 + 1]
                ns_ref[s, h] = dcol * s0s[(s, h)] + upd[(s, h)]

    acols = _gated_rmsnorm_cols([attn_scr[c] for c in range(4)], attn_g, z[:, C_GA:C_GA + ATTN_WIDTH],
                                ATTN_WIDTH)
    for c in range(4):
        mix_scr[:, c * LANES:(c + 1) * LANES] = acols[c].astype(BF16)
    for h in range(HG_HEADS):
        hs = slice(h * LANES, (h + 1) * LANES)
        o_h = o_intra[h] + jnp.concatenate(o_inter[h], axis=0)
        gt = z[:, C_GH + h * LANES:C_GH + (h + 1) * LANES]
        inv = lax.rsqrt(jnp.mean(o_h * o_h, axis=-1, keepdims=True) + NORM_EPS)
        mix_scr[:, ATTN_WIDTH + h * LANES:ATTN_WIDTH + (h + 1) * LANES] = (
            o_h * inv * hg_g[:, hs] * _silu(gt)).astype(BF16)

    hpre = DN_ALPHA * x + _chunked_dot(mix_scr[...], w_out_ref, OUT_CHUNKS)
    hn = _layernorm(hpre, v1024_ref[0:1, :], v1024_ref[1:2, :])
    gate = _sigmoid(_chunked_dot(hn.astype(BF16), w_pg_ref, OUT_CHUNKS))
    pp = _chunked_dot(p_ref[...].reshape(rows, PLE_DIM).astype(BF16), w_pp_ref, OUT_CHUNKS)
    y = hn + gate * pp
    y_scr[yrows, :] = y
    y_ref[...] = y.reshape(SAMPLE_SEQS, 8, D_MODEL)


def _level_matrix():
    t = np.arange(BLK)[:, None]
    s = np.arange(BLK)[None, :]
    x = t ^ s
    lv = np.floor(np.log2(np.maximum(x, 1))).astype(np.int32)
    return np.where(t > s, lv, np.where(t == s, -1, -2)).astype(np.int32)


def _level_exponent_matrix(l):
    m = np.zeros((BLK, BLK), np.float32)
    b, h = 2 << l, 1 << l
    for t in range(BLK):
        mid = t - t % b + h
        if t >= mid:
            m[t, mid:t + 1] = 1.0
        else:
            m[t, t + 1:mid] = 1.0
    return m


def _mask_matrices(seq_rows):
    blocks = [_level_exponent_matrix(l) for l in range(LOW_LEVELS)]
    t = np.arange(BLK)[:, None]
    s = np.arange(BLK)[None, :]
    same = (t // seq_rows) == (s // seq_rows)
    blocks.append((same & (s <= t)).astype(np.float32))
    if seq_rows < BLK:
        blocks.append(same.astype(np.float32))
    return np.concatenate(blocks, axis=0)


def _rope_tables(pos, scale):
    half = HEAD_DIM // 2
    inv = jnp.exp(-math.log(ROPE_THETA) * jnp.arange(half, dtype=F32) * 2.0 / HEAD_DIM)
    ang = pos.astype(F32)[:, None] * inv[None, :]
    cos = jnp.cos(ang) * scale
    sin = jnp.sin(ang) * scale
    return jnp.tile(cos, (1, 4)), jnp.concatenate([-sin, sin, -sin, sin], axis=1)


def _column_chunks(w, width):
    d, k, n = w.shape
    return w.astype(BF16).reshape(d, k, n // width, width).transpose(0, 2, 1, 3)


def _const_spec(shape, layer=None):
    if layer is None:
        return pl.BlockSpec(shape, lambda *_: (0,) * len(shape))
    return pl.BlockSpec((None,) + shape, lambda *_: (layer,) + (0,) * len(shape))


def _prompt_layer(i, x, p_all, tables, lvl, mlow, weights, sinks, v512, v1024):
    B, T, _ = x.shape
    nt = T // PROMPT_TILE
    nb = B // PROMPT_SEQS
    tab_spec = pl.BlockSpec((PROMPT_TILE, LANES), lambda b, t: (t, 0))
    nblk = PROMPT_TILE // BLK
    assert nblk % IN_GROUP == 0 and nblk % 2 == 0

    def next_first_group(b, t):
        flat = jnp.minimum(b * nt + t + 1, nb * nt - 1)
        return (flat // nt, (flat % nt) * (nblk // IN_GROUP), 0)

    def wspec(shape):
        return pl.BlockSpec((None,) + shape, lambda b, t: (i,) + (0,) * len(shape),
                            pipeline_mode=pl.Buffered(1))
    in_specs = [
        pl.BlockSpec(memory_space=pltpu.SMEM),
        pl.BlockSpec((PROMPT_SEQS, PROMPT_TILE, D_MODEL), lambda b, t: (b, t, 0)),
        pl.BlockSpec((PROMPT_SEQS, IN_GROUP * BLK, D_MODEL), next_first_group),
        pl.BlockSpec((None, PROMPT_SEQS, PROMPT_TILE, PLE_DIM), lambda b, t: (i, b, t, 0)),
        tab_spec, tab_spec, tab_spec, tab_spec,
        _const_spec((BLK, BLK)),
        _const_spec(mlow.shape),
        wspec((IN_CHUNKS, D_MODEL, CHUNK_COLS)),
        wspec((1, D_MODEL, IN_TAIL)),
        wspec((OUT_CHUNKS, MIX_WIDTH, CHUNK_COLS)),
        wspec((OUT_CHUNKS, D_MODEL, CHUNK_COLS)),
        wspec((OUT_CHUNKS, PLE_DIM, CHUNK_COLS)),
        _const_spec((4, ATTN_WIDTH), i),
        _const_spec((2, D_MODEL), i),
    ]
    out_shape = [
        jax.ShapeDtypeStruct((B, T, D_MODEL), F32),
        jax.ShapeDtypeStruct((B, WINDOW, KV_WIDTH), F32),
        jax.ShapeDtypeStruct((B, WINDOW, KV_WIDTH), F32),
        jax.ShapeDtypeStruct((B, HG_HEADS, HG_DK, HG_DV), F32),
    ]
    out_specs = [
        pl.BlockSpec((PROMPT_SEQS, PROMPT_TILE, D_MODEL), lambda b, t: (b, t, 0)),
        pl.BlockSpec((PROMPT_SEQS, WINDOW, KV_WIDTH), lambda b, t: (b, 0, 0)),
        pl.BlockSpec((PROMPT_SEQS, WINDOW, KV_WIDTH), lambda b, t: (b, 0, 0)),
        pl.BlockSpec((PROMPT_SEQS, HG_HEADS, HG_DK, HG_DV), lambda b, t: (b, 0, 0, 0)),
    ]
    scratch = [
        pltpu.VMEM((PROMPT_SEQS, HG_HEADS, HG_DV, HG_DK), F32),
        pltpu.VMEM((PROMPT_SEQS, 4, BLK, LANES), BF16),
        pltpu.VMEM((PROMPT_SEQS, 4, BLK, LANES), BF16),
        pltpu.VMEM((PROMPT_SEQS, nblk, BLK, IN_COLS), F32),
        pltpu.VMEM((PROMPT_SEQS, 2, BLK, MIX_WIDTH), BF16),
    ]
    return pl.pallas_call(
        _prompt_kernel,
        grid=(nb, nt),
        in_specs=in_specs,
        out_specs=out_specs,
        out_shape=out_shape,
        scratch_shapes=scratch,
        compiler_params=pltpu.CompilerParams(
            dimension_semantics=("arbitrary", "arbitrary"), vmem_limit_bytes=VMEM_LIMIT),
        name=f"prompt_layer{i}",
    )(sinks, x, x, p_all, *tables, lvl, mlow, *weights, v512, v1024)


def _sample_call(x, p_all, ck_all, cv_all, s0_all, tables, lvl, mlow, seg, weights, sinks, v512, v1024):
    B, T, _ = x.shape
    rows = SAMPLE_SEQS * T

    def wspec(shape):
        return pl.BlockSpec((None,) + shape, lambda l, b: (l,) + (0,) * len(shape),
                            pipeline_mode=pl.Buffered(1))
    tab_spec = _const_spec((rows, LANES))
    in_specs = [
        pl.BlockSpec(memory_space=pltpu.SMEM),
        pl.BlockSpec((SAMPLE_SEQS, T, D_MODEL), lambda l, b: (jnp.where(l == 0, b, 0), 0, 0)),
        pl.BlockSpec((None, SAMPLE_SEQS, T, PLE_DIM), lambda l, b: (l, b, 0, 0)),
        pl.BlockSpec((None, SAMPLE_SEQS, WINDOW, KV_WIDTH), lambda l, b: (l, b, 0, 0)),
        pl.BlockSpec((None, SAMPLE_SEQS, WINDOW, KV_WIDTH), lambda l, b: (l, b, 0, 0)),
        pl.BlockSpec((None, SAMPLE_SEQS, HG_HEADS, HG_DK, HG_DV), lambda l, b: (l, b, 0, 0, 0)),
        tab_spec, tab_spec, tab_spec, tab_spec,
        _const_spec((BLK, BLK)),
        _const_spec(mlow.shape),
        _const_spec((BLK, LANES)),
        wspec((IN_CHUNKS, D_MODEL, CHUNK_COLS)),
        wspec((1, D_MODEL, IN_TAIL)),
        wspec((OUT_CHUNKS, MIX_WIDTH, CHUNK_COLS)),
        wspec((OUT_CHUNKS, D_MODEL, CHUNK_COLS)),
        wspec((OUT_CHUNKS, PLE_DIM, CHUNK_COLS)),
        pl.BlockSpec((None, 4, ATTN_WIDTH), lambda l, b: (l, 0, 0)),
        pl.BlockSpec((None, 2, D_MODEL), lambda l, b: (l, 0, 0)),
    ]
    out_shape = [
        jax.ShapeDtypeStruct((B, T, D_MODEL), F32),
        jax.ShapeDtypeStruct((DEPTH, B, WINDOW, KV_WIDTH), F32),
        jax.ShapeDtypeStruct((DEPTH, B, WINDOW, KV_WIDTH), F32),
        jax.ShapeDtypeStruct((DEPTH, B, HG_HEADS, HG_DK, HG_DV), F32),
    ]
    out_specs = [
        pl.BlockSpec((SAMPLE_SEQS, T, D_MODEL), lambda l, b: (jnp.where(l == DEPTH - 1, b, 0), 0, 0)),
        pl.BlockSpec((None, SAMPLE_SEQS, WINDOW, KV_WIDTH), lambda l, b: (l, b, 0, 0)),
        pl.BlockSpec((None, SAMPLE_SEQS, WINDOW, KV_WIDTH), lambda l, b: (l, b, 0, 0)),
        pl.BlockSpec((None, SAMPLE_SEQS, HG_HEADS, HG_DK, HG_DV), lambda l, b: (l, b, 0, 0, 0)),
    ]
    scratch = [
        pltpu.VMEM((B * T, D_MODEL), F32),
        pltpu.VMEM((SAMPLE_SEQS, 64, LANES), F32),
        pltpu.VMEM((4, rows, LANES), F32),
        pltpu.VMEM((rows, MIX_WIDTH), BF16),
    ]
    return pl.pallas_call(
        _sample_kernel,
        grid=(DEPTH, B // SAMPLE_SEQS),
        in_specs=in_specs,
        out_specs=out_specs,
        out_shape=out_shape,
        scratch_shapes=scratch,
        compiler_params=pltpu.CompilerParams(
            dimension_semantics=("arbitrary", "arbitrary"), vmem_limit_bytes=VMEM_LIMIT),
        name="sample_layers",
    )(sinks, x, p_all, ck_all, cv_all, s0_all, *tables, lvl, mlow, seg, *weights, v512, v1024)


def kernel(x_prompt, x_sample, cache_k_win, cache_v_win, state_hgrn, p_prompt, p_sample, w_in, attn_sinks,
           attn_norm_g, hg_lb_logits, hg_norm_g, w_out, ln_g, ln_b, w_ple_proj, w_ple_gate):
    B, T, _ = x_prompt.shape
    SB, ST, _ = x_sample.shape
    assert T % PROMPT_TILE == 0 and B % PROMPT_SEQS == 0
    assert SB % SAMPLE_SEQS == 0 and ST == 8 and SAMPLE_SEQS * ST == BLK
    assert cache_k_win.shape[2] == WINDOW

    cs = jnp.cumsum(jax.nn.softmax(hg_lb_logits.astype(F32), axis=0), axis=0)
    lbs = cs - cs[:1]
    v512 = jnp.stack([attn_norm_g.astype(F32), jnp.log(lbs), jnp.log1p(-lbs), hg_norm_g.astype(F32)], axis=1)
    v1024 = jnp.stack([ln_g.astype(F32), ln_b.astype(F32)], axis=1)
    n_main = IN_CHUNKS * CHUNK_COLS
    weights = (_column_chunks(w_in[:, :, :n_main], CHUNK_COLS), _column_chunks(w_in[:, :, n_main:], IN_TAIL),
               _column_chunks(w_out, CHUNK_COLS), _column_chunks(w_ple_gate, CHUNK_COLS),
               _column_chunks(w_ple_proj, CHUNK_COLS))
    sinks = attn_sinks.astype(F32)

    scale = HEAD_DIM ** -0.5
    pos_p = jnp.arange(T, dtype=jnp.int32)
    pos_s = jnp.tile(PAST_LEN + jnp.arange(ST, dtype=jnp.int32), SAMPLE_SEQS)
    tab_p = _rope_tables(pos_p, scale) + _rope_tables(pos_p, 1.0)
    tab_s = _rope_tables(pos_s, scale) + _rope_tables(pos_s, 1.0)
    lvl = jnp.asarray(_level_matrix())
    mlow_p = jnp.asarray(_mask_matrices(BLK), dtype=BF16)
    mlow_s = jnp.asarray(_mask_matrices(ST), dtype=BF16)
    seg = jnp.asarray((np.arange(BLK)[:, None] // ST) == (np.arange(LANES)[None, :] // ST), dtype=BF16)

    ck = cache_k_win.reshape(DEPTH, SB, WINDOW, KV_WIDTH)
    cv = cache_v_win.reshape(DEPTH, SB, WINDOW, KV_WIDTH)
    ys, ks, vs, ss = _sample_call(x_sample, p_sample, ck, cv, state_hgrn, tab_s, lvl, mlow_s, seg, weights,
                                  sinks.reshape(DEPTH * ATTN_HEADS), v512, v1024)

    yp = x_prompt
    kp_l, vp_l, sp_l = [], [], []
    for i in range(DEPTH):
        yp, kp, vp, sp = _prompt_layer(i, yp, p_prompt, tab_p, lvl, mlow_p, weights, sinks[i], v512, v1024)
        kp_l.append(kp); vp_l.append(vp); sp_l.append(sp)

    def kv5(a, b):
        return a.reshape(DEPTH, b, WINDOW, KV_HEADS, HEAD_DIM)

    return (yp, ys, kv5(jnp.stack(kp_l), B), kv5(jnp.stack(vp_l), B), jnp.stack(sp_l),
            kv5(ks, SB), kv5(vs, SB), ss)
```

```python
import math

import numpy as np
import jax
import jax.numpy as jnp
from jax import lax
from jax.experimental import pallas as pl
from jax.experimental.pallas import tpu as pltpu

D_MODEL = 1024
DEPTH = 4
PAST_LEN = 8192
ATTN_HEADS = 8
KV_HEADS = 2
HEAD_DIM = 64
ATTN_WIDTH = ATTN_HEADS * HEAD_DIM
KV_WIDTH = KV_HEADS * HEAD_DIM
WINDOW = 128
ROPE_THETA = 10000.0
HG_HEADS = 4
HG_DK = 128
HG_DV = 128
HG_KW = HG_HEADS * HG_DK
HG_VW = HG_HEADS * HG_DV
MIX_WIDTH = ATTN_WIDTH + HG_VW
IN_COLS = 2 * ATTN_WIDTH + 2 * KV_WIDTH + 2 * HG_KW + 2 * HG_VW
PLE_DIM = 256
DN_ALPHA = (2 * DEPTH) ** 0.25
NORM_EPS = 1e-5
NEG_INF = -1e30
LOG2E = 1.4426950408889634

C_Q = 0
C_K = C_Q + ATTN_WIDTH
C_V = C_K + KV_WIDTH
C_GA = C_V + KV_WIDTH
C_HQ = C_GA + ATTN_WIDTH
C_HF = C_HQ + HG_KW
C_HI = C_HF + HG_KW
C_GH = C_HI + HG_VW

LANES = 128
MXU_COLS = 256
CHUNK_COLS = 2 * MXU_COLS
BLK = 128
PROMPT_TILE = 512
PROMPT_SEQS = 1
IN_GROUP = 2
SAMPLE_SEQS = 16
LOW_LEVELS = 3
PROMPT_LEVELS = 7
VMEM_LIMIT = 56 * 1024 * 1024
IN_CHUNKS = IN_COLS // CHUNK_COLS
IN_TAIL = IN_COLS - IN_CHUNKS * CHUNK_COLS
OUT_CHUNKS = D_MODEL // CHUNK_COLS

F32 = jnp.float32
BF16 = jnp.bfloat16


def _dot(a, b):
    return jnp.dot(a, b, preferred_element_type=F32)


def _dot_nt(a, b):
    return lax.dot_general(a, b, (((1,), (1,)), ((), ())), preferred_element_type=F32)


def _dot_tn(a, b):
    return lax.dot_general(a, b, (((0,), (0,)), ((), ())), preferred_element_type=F32)


def _split_bf16(x):
    hi = x.astype(BF16)
    lo = (x - hi.astype(F32)).astype(BF16)
    return hi, lo


def _sigmoid(x):
    return 0.5 * jnp.tanh(0.5 * x) + 0.5


def _silu(x):
    h = 0.5 * x
    return h * jnp.tanh(h) + h


def _lane_iota(rows):
    return lax.broadcasted_iota(jnp.int32, (rows, LANES), 1)


def _rope(x, cos_t, sin_t):
    lane = _lane_iota(x.shape[0])
    first_half = (lane & 32) == 0
    swapped = jnp.where(first_half, pltpu.roll(x, 96, 1), pltpu.roll(x, 32, 1))
    return x * cos_t + swapped * sin_t


def _kv_variants(a, fill):
    lane = _lane_iota(a.shape[0])
    lo = lane < HEAD_DIM
    sw = pltpu.roll(a, HEAD_DIM, 1)
    f = jnp.full_like(a, fill)
    out = [jnp.where(lo, a, f), jnp.where(lo, f, sw), jnp.where(lo, sw, f), jnp.where(lo, f, a)]
    return [o.astype(BF16) for o in out]


def _sum_half(e, es, rows):
    lane = _lane_iota(rows)
    sum_half = (lane >= HEAD_DIM) if e == 0 else (lane < HEAD_DIM)
    return jnp.where(sum_half, es, 0.0)


def _assemble_attn(res, rows):
    lane = _lane_iota(rows)
    lo = lane < HEAD_DIM
    cols = []
    for c in range(4):
        g, cc = c // 2, c % 2
        r0 = res[g * 2 + 0][cc * rows:(cc + 1) * rows]
        r1 = res[g * 2 + 1][cc * rows:(cc + 1) * rows]
        num = jnp.where(lo, r0, r1)
        den = pltpu.roll(jnp.where(lo, r1, r0), HEAD_DIM, 1)
        cols.append(num / den)
    return cols


def _log_decay(hf, log_lb, log1m_lb):
    ls = jnp.minimum(hf, 0.0) - jnp.log(1.0 + jnp.exp(-jnp.abs(hf)))
    b = log1m_lb + ls
    return jnp.maximum(log_lb, b) + jnp.log(1.0 + jnp.exp(-jnp.abs(log_lb - b)))


def _level_operand(l, q_h, k_h, e_low, g_h):
    if l < LOW_LEVELS:
        rows = lax.broadcasted_iota(jnp.int32, (BLK, LANES), 0)
        upper = ((rows >> l) & 1) == 1
        return jnp.where(upper, q_h, k_h) * jnp.exp2(e_low[l])
    b, h = 2 << l, 1 << l
    pieces = []
    for i in range(BLK // b):
        lower, upper = slice(i * b, i * b + h), slice(i * b + h, (i + 1) * b)
        g_mid = g_h[i * b + h - 1:i * b + h, :]
        pieces.append(k_h[lower, :] * jnp.exp2(g_mid - g_h[lower, :]))
        pieces.append(q_h[upper, :] * jnp.exp2(g_h[upper, :] - g_mid))
    return jnp.concatenate(pieces, axis=0)


def _hgrn_scores_steps(q_h, k_h, e_low, g_h, lvl, levels, out):
    a = jnp.zeros((BLK, BLK), F32)
    for l in range(levels):
        u = _level_operand(l, q_h, k_h, e_low, g_h).astype(BF16)
        a = jnp.where(lvl == l, _dot_nt(u, u), a)
        yield
    diag = jnp.sum(q_h * k_h, axis=-1, keepdims=True)
    out["a"] = jnp.where(lvl == -1, diag, a)


def _hgrn_scores(q_h, k_h, e_low, g_h, lvl, levels):
    out = {}
    for _ in _hgrn_scores_steps(q_h, k_h, e_low, g_h, lvl, levels, out):
        pass
    return out["a"]


def _gated_rmsnorm_cols(cols, gain_row, gate, width):
    ss = None
    for c in cols:
        s = jnp.sum(c * c, axis=-1, keepdims=True)
        ss = s if ss is None else ss + s
    inv = lax.rsqrt(ss * (1.0 / width) + NORM_EPS)
    out = []
    for i, c in enumerate(cols):
        gt = gate[:, i * LANES:(i + 1) * LANES]
        out.append(c * inv * gain_row[:, i * LANES:(i + 1) * LANES] * _silu(gt))
    return out


def _layernorm(hpre, ln_g, ln_b):
    mu = jnp.mean(hpre, axis=-1, keepdims=True)
    cen = hpre - mu
    var = jnp.mean(cen * cen, axis=-1, keepdims=True)
    return cen * lax.rsqrt(var + NORM_EPS) * ln_g + ln_b


def _chunked_dot(a, w_ref, n):
    return jnp.concatenate([_dot(a, w_ref[c]) for c in range(n)], axis=1)


def _merge(a, b):
    out, ia, ib = [], 0, 0
    while ia < len(a) or ib < len(b):
        if ib >= len(b) or (ia < len(a) and ia * len(b) <= ib * len(a)):
            out.append(a[ia])
            ia += 1
        else:
            out.append(b[ib])
            ib += 1
    return out


def _interleave(gen, n_yields, thunks):
    thunks = list(thunks)
    total, done, seen = len(thunks), 0, 0
    for _ in gen:
        seen += 1
        want = min(total, (seen * total + n_yields - 1) // n_yields)
        while done < want:
            thunks[done]()
            done += 1
    while done < total:
        thunks[done]()
        done += 1


def _prompt_kernel(sinks_ref, x_ref, xn_ref, p_ref, cq_ref, sq_ref, ck_ref, sk_ref, lvl_ref, mlow_ref,
                   w_in_ref, w_tail_ref, w_out_ref, w_pg_ref, w_pp_ref, v512_ref, v1024_ref,
                   y_ref, kk_ref, vk_ref, sfin_ref,
                   st_scr, kprev_scr, vprev_scr, z_scr, mix_scr):
    t = pl.program_id(1)
    nblk = PROMPT_TILE // BLK

    @pl.when(t == 0)
    def _():
        st_scr[...] = jnp.zeros_like(st_scr)
        kprev_scr[...] = jnp.zeros_like(kprev_scr)
        vprev_scr[...] = jnp.zeros_like(vprev_scr)

    lvl = lvl_ref[...]
    attn_g = v512_ref[0:1, :]
    log_lb = v512_ref[1:2, :]
    log1m_lb = v512_ref[2:3, :]
    hg_g = v512_ref[3:4, :]

    row = lax.broadcasted_iota(jnp.int32, (2 * BLK, BLK), 0)
    col = lax.broadcasted_iota(jnp.int32, (2 * BLK, BLK), 1)
    from_prev = col > (row & (BLK - 1))
    first_rows = lax.broadcasted_iota(jnp.int32, (2 * BLK, 1), 0) < BLK
    carry = [{} for _ in range(PROMPT_SEQS)]

    ngrp = nblk // IN_GROUP
    grp_rows = IN_GROUP * BLK

    def stage_in(q, g):
        cell = {}
        first = (g % ngrp) * IN_GROUP

        def chunk(c):
            def run():
                if "xb" not in cell:
                    src = xn_ref[q] if g == ngrp else x_ref[q, g * grp_rows:(g + 1) * grp_rows, :]
                    cell["xb"] = src.astype(BF16)
                if c < IN_CHUNKS:
                    cols, zz = slice(c * CHUNK_COLS, (c + 1) * CHUNK_COLS), _dot(cell["xb"], w_in_ref[c])
                else:
                    cols, zz = slice(IN_CHUNKS * CHUNK_COLS, IN_COLS), _dot(cell["xb"], w_tail_ref[0])
                for i in range(IN_GROUP):
                    z_scr[q, first + i, :, cols] = zz[i * BLK:(i + 1) * BLK, :]
            return run
        return [chunk(c) for c in range(IN_CHUNKS + 1)]

    def stage_out(q, j):
        slot, rows_j, cell = j % 2, slice(j * BLK, (j + 1) * BLK), {"m": [], "g": []}

        def mix_chunk(c):
            def run():
                cell["m"].append(_dot(mix_scr[q, slot], w_out_ref[c]))
                if c + 1 == OUT_CHUNKS:
                    hpre = DN_ALPHA * x_ref[q, rows_j, :] + jnp.concatenate(cell.pop("m"), axis=1)
                    h = _layernorm(hpre, v1024_ref[0:1, :], v1024_ref[1:2, :])
                    cell["h"] = h
                    cell["hb"] = h.astype(BF16)
            return run

        def gate_chunk(c):
            def run():
                cell["g"].append(_sigmoid(_dot(cell["hb"], w_pg_ref[c])))
            return run

        def finish():
            pb = p_ref[q, rows_j, :].astype(BF16)
            for c in range(OUT_CHUNKS):
                cs = slice(c * CHUNK_COLS, (c + 1) * CHUNK_COLS)
                y_ref[q, rows_j, cs] = cell["h"][:, cs] + cell["g"][c] * _dot(pb, w_pp_ref[c])
        return ([mix_chunk(c) for c in range(OUT_CHUNKS)] + [gate_chunk(c) for c in range(OUT_CHUNKS)]
                + [finish])

    def attn_part(q, j):
        slot, rows_j = j % 2, slice(j * BLK, (j + 1) * BLK)

        def zc(c0, width):
            return z_scr[q, j, :, c0:c0 + width]
        qcols = [_rope(zc(C_Q + c * LANES, LANES), cq_ref[rows_j, :], sq_ref[rows_j, :]).astype(BF16)
                 for c in range(4)]
        k_rot = _rope(zc(C_K, KV_WIDTH), ck_ref[rows_j, :], sk_ref[rows_j, :])
        v_new = zc(C_V, KV_WIDTH)
        yield
        if j == 0:
            k_prev = [kprev_scr[q, i] for i in range(4)]
            v_prev = [vprev_scr[q, i] for i in range(4)]
        else:
            k_prev, v_prev = carry[q]["k_var"], carry[q]["v_var"]
        k_var = _kv_variants(k_rot, 0.0)
        v_var = _kv_variants(v_new, 1.0)
        carry[q]["k_var"], carry[q]["v_var"] = k_var, v_var
        if j == nblk - 1:
            for i in range(4):
                kprev_scr[q, i] = k_var[i]
                vprev_scr[q, i] = v_var[i]
            kk_ref[q] = k_rot
            vk_ref[q] = v_new
        yield
        res = []
        for g in range(2):
            qst = jnp.concatenate([qcols[2 * g], qcols[2 * g + 1]], axis=0)
            for e in range(2):
                i = g * 2 + e
                s_prev = _dot_nt(qst, k_prev[i])
                if j == 0:
                    s_prev = jnp.where(t > 0, s_prev, NEG_INF)
                s = jnp.where(from_prev, s_prev, _dot_nt(qst, k_var[i]))
                yield
                sink_col = jnp.where(first_rows, sinks_ref[4 * g + e], sinks_ref[4 * g + 2 + e])
                m = jnp.maximum(jnp.max(s, axis=-1, keepdims=True), sink_col)
                p = jnp.exp(s - m)
                es = jnp.exp(sink_col - m)
                yield
                p_both = jnp.concatenate([jnp.where(from_prev, p, 0.0), jnp.where(from_prev, 0.0, p)], axis=1)
                r = _dot(p_both.astype(BF16), jnp.concatenate([v_prev[i], v_var[i]], axis=0))
                res.append(r + _sum_half(e, es, 2 * BLK))
                yield
        acols = _assemble_attn(res, BLK)
        acols = _gated_rmsnorm_cols(acols, attn_g, zc(C_GA, ATTN_WIDTH), ATTN_WIDTH)
        for c in range(4):
            mix_scr[q, slot, :, c * LANES:(c + 1) * LANES] = acols[c].astype(BF16)
        yield

    def hgrn_part(q, j):
        slot = j % 2

        def zc(c0, width):
            return z_scr[q, j, :, c0:c0 + width]
        lf = _log_decay(zc(C_HF, HG_KW), log_lb, log1m_lb) * LOG2E
        kin = 1.0 - jnp.exp2(lf)
        lf_hi, lf_lo = _split_bf16(lf)
        m_low = mlow_ref[0:LOW_LEVELS * BLK, :]
        m_cum = mlow_ref[LOW_LEVELS * BLK:(LOW_LEVELS + 1) * BLK, :]
        eg = jnp.concatenate([_dot(m_low, lf_hi), _dot(m_cum, lf_hi) + _dot(m_cum, lf_lo)], axis=0)
        yield
        for h in range(HG_HEADS):
            hs = slice(h * LANES, (h + 1) * LANES)
            q_h = zc(C_HQ + h * LANES, LANES)
            k_h = kin[:, hs]
            v_h = zc(C_HI + h * LANES, LANES).astype(BF16)
            e_low = [eg[l * BLK:(l + 1) * BLK, hs] for l in range(LOW_LEVELS)]
            g_h = eg[LOW_LEVELS * BLK:(LOW_LEVELS + 1) * BLK, hs]
            out = {}
            yield from _hgrn_scores_steps(q_h, k_h, e_low, g_h, lvl, PROMPT_LEVELS, out)
            st = st_scr[q, h]
            o_h = (_dot(out["a"].astype(BF16), v_h)
                   + _dot_nt((q_h * jnp.exp2(g_h)).astype(BF16), st.astype(BF16)))
            g_last = g_h[BLK - 1:BLK, :]
            kd = (k_h * jnp.exp2(g_last - g_h)).astype(BF16)
            st_scr[q, h] = st * jnp.exp2(g_last) + _dot_tn(v_h, kd)
            yield
            gt = zc(C_GH + h * LANES, LANES)
            inv = lax.rsqrt(jnp.mean(o_h * o_h, axis=-1, keepdims=True) + NORM_EPS)
            mix_scr[q, slot, :, ATTN_WIDTH + h * LANES:ATTN_WIDTH + (h + 1) * LANES] = (
                o_h * inv * hg_g[:, hs] * _silu(gt)).astype(BF16)
            yield

    def mix_units(q, j):
        slot, rows_j, s = j % 2, slice(j * BLK, (j + 1) * BLK), {}

        def zc(c0, width):
            return z_scr[q, j, :, c0:c0 + width]

        def a_rope():
            qcols = [_rope(zc(C_Q + c * LANES, LANES), cq_ref[rows_j, :], sq_ref[rows_j, :]).astype(BF16)
                     for c in range(4)]
            k_rot = _rope(zc(C_K, KV_WIDTH), ck_ref[rows_j, :], sk_ref[rows_j, :])
            v_new = zc(C_V, KV_WIDTH)
            if j == 0:
                s["k_prev"] = [kprev_scr[q, i] for i in range(4)]
                s["v_prev"] = [vprev_scr[q, i] for i in range(4)]
            else:
                s["k_prev"], s["v_prev"] = carry[q]["k_var"], carry[q]["v_var"]
            k_var = _kv_variants(k_rot, 0.0)
            v_var = _kv_variants(v_new, 1.0)
            carry[q]["k_var"], carry[q]["v_var"] = k_var, v_var
            s["k_var"], s["v_var"] = k_var, v_var
            if j == nblk - 1:
                for i in range(4):
                    kprev_scr[q, i] = k_var[i]
                    vprev_scr[q, i] = v_var[i]
                kk_ref[q] = k_rot
                vk_ref[q] = v_new
            s["qst"] = [jnp.concatenate([qcols[2 * g], qcols[2 * g + 1]], axis=0) for g in range(2)]
            s["scores"], s["probs"], s["res"] = {}, {}, {}

        def a_scores(i):
            def run():
                s_prev = _dot_nt(s["qst"][i // 2], s["k_prev"][i])
                if j == 0:
                    s_prev = jnp.where(t > 0, s_prev, NEG_INF)
                s["scores"][i] = jnp.where(from_prev, s_prev, _dot_nt(s["qst"][i // 2], s["k_var"][i]))
            return run

        def a_softmax(i):
            def run():
                g, e = i // 2, i % 2
                sc = s["scores"].pop(i)
                sink_col = jnp.where(first_rows, sinks_ref[4 * g + e], sinks_ref[4 * g + 2 + e])
                m = jnp.maximum(jnp.max(sc, axis=-1, keepdims=True), sink_col)
                p = jnp.exp(sc - m)
                p_both = jnp.concatenate([jnp.where(from_prev, p, 0.0), jnp.where(from_prev, 0.0, p)], axis=1)
                s["probs"][i] = (p_both.astype(BF16), jnp.exp(sink_col - m))
            return run

        def a_values(i):
            def run():
                p_both, es = s["probs"].pop(i)
                r = _dot(p_both, jnp.concatenate([s["v_prev"][i], s["v_var"][i]], axis=0))
                s["res"][i] = r + _sum_half(i % 2, es, 2 * BLK)
            return run

        def a_finish():
            acols = _assemble_attn([s["res"][i] for i in range(4)], BLK)
            acols = _gated_rmsnorm_cols(acols, attn_g, zc(C_GA, ATTN_WIDTH), ATTN_WIDTH)
            for c in range(4):
                mix_scr[q, slot, :, c * LANES:(c + 1) * LANES] = acols[c].astype(BF16)

        def h_prep():
            lf = _log_decay(zc(C_HF, HG_KW), log_lb, log1m_lb) * LOG2E
            kin = 1.0 - jnp.exp2(lf)
            lf_hi, lf_lo = _split_bf16(lf)
            m_low = mlow_ref[0:LOW_LEVELS * BLK, :]
            m_cum = mlow_ref[LOW_LEVELS * BLK:(LOW_LEVELS + 1) * BLK, :]
            eg = jnp.concatenate([_dot(m_low, lf_hi), _dot(m_cum, lf_hi) + _dot(m_cum, lf_lo)], axis=0)
            s["heads"] = []
            for h in range(HG_HEADS):
                hs = slice(h * LANES, (h + 1) * LANES)
                s["heads"].append(dict(
                    q=zc(C_HQ + h * LANES, LANES), k=kin[:, hs],
                    e_low=[eg[l * BLK:(l + 1) * BLK, hs] for l in range(LOW_LEVELS)],
                    g=eg[LOW_LEVELS * BLK:(LOW_LEVELS + 1) * BLK, hs], u=[], p=[]))

        def h_operand(h, l):
            def run():
                hd = s["heads"][h]
                hd["u"].append(_level_operand(l, hd["q"], hd["k"], hd["e_low"], hd["g"]).astype(BF16))
            return run

        def h_products(h):
            def run():
                hd = s["heads"][h]
                hd["p"] = [_dot_nt(u, u) for u in hd.pop("u")]
            return run

        def h_combine(h):
            def run():
                hd = s["heads"][h]
                a = jnp.zeros((BLK, BLK), F32)
                for l, p in enumerate(hd.pop("p")):
                    a = jnp.where(lvl == l, p, a)
                diag = jnp.sum(hd["q"] * hd["k"], axis=-1, keepdims=True)
                hd["a"] = jnp.where(lvl == -1, diag, a).astype(BF16)
                g_last = hd["g"][BLK - 1:BLK, :]
                hd["qe"] = (hd["q"] * jnp.exp2(hd["g"])).astype(BF16)
                hd["kd"] = (hd["k"] * jnp.exp2(g_last - hd["g"])).astype(BF16)
                hd["dec"] = jnp.exp2(g_last)
            return run

        def h_output(h):
            def run():
                hd = s["heads"][h]
                v_h = zc(C_HI + h * LANES, LANES).astype(BF16)
                st = st_scr[q, h]
                hd["o"] = _dot(hd.pop("a"), v_h) + _dot_nt(hd.pop("qe"), st.astype(BF16))
                st_scr[q, h] = st * hd.pop("dec") + _dot_tn(v_h, hd.pop("kd"))
            return run

        def h_finish(h):
            def run():
                o_h = s["heads"][h].pop("o")
                gt = zc(C_GH + h * LANES, LANES)
                inv = lax.rsqrt(jnp.mean(o_h * o_h, axis=-1, keepdims=True) + NORM_EPS)
                mix_scr[q, slot, :, ATTN_WIDTH + h * LANES:ATTN_WIDTH + (h + 1) * LANES] = (
                    o_h * inv * hg_g[:, h * LANES:(h + 1) * LANES] * _silu(gt)).astype(BF16)
            return run

        heads, variants = range(HG_HEADS), range(4)
        return ([a_rope, h_prep]
                + _merge([a_scores(i) for i in variants],
                         [h_operand(h, l) for h in heads for l in range(PROMPT_LEVELS)])
                + _merge([a_softmax(i) for i in variants], [h_products(h) for h in heads])
                + _merge([a_values(i) for i in variants], [h_combine(h) for h in heads])
                + _merge([a_finish], [h_output(h) for h in heads])
                + [h_finish(h) for h in heads])

    def stage_mix(j):
        units = mix_units(0, j)
        for q in range(1, PROMPT_SEQS):
            units = _merge(units, mix_units(q, j))
        for u in units:
            u()
            yield

    mix_yields = PROMPT_SEQS * (2 + 4 * 3 + 1 + HG_HEADS * (PROMPT_LEVELS + 4))

    @pl.when(jnp.logical_and(pl.program_id(0) == 0, t == 0))
    def _():
        for q in range(PROMPT_SEQS):
            for th in stage_in(q, 0):
                th()

    def both(stage, j):
        lists = [stage(q, j) for q in range(PROMPT_SEQS)]
        return [th for group in zip(*lists) for th in group]

    for j in range(nblk):
        nxt = both(stage_in, j // IN_GROUP + 1)
        part = j % IN_GROUP
        per = -(-len(nxt) // IN_GROUP)
        side = nxt[part * per:(part + 1) * per]
        if j >= 1:
            side = side + both(stage_out, j - 1)
        _interleave(stage_mix(j), mix_yields, side)
    for th in both(stage_out, nblk - 1):
        th()

    @pl.when(t == pl.num_programs(1) - 1)
    def _():
        for q in range(PROMPT_SEQS):
            for h in range(HG_HEADS):
                sfin_ref[q, h] = st_scr[q, h].T


def _sample_kernel(sinks_ref, x_ref, p_ref, ck_ref, cv_ref, s0_ref, cq_ref, sq_ref, ckt_ref, skt_ref,
                   lvl_ref, mlow_ref, seg_ref, w_in_ref, w_tail_ref, w_out_ref, w_pg_ref, w_pp_ref, v512_ref,
                   v1024_ref,
                   y_ref, nk_ref, nv_ref, ns_ref,
                   y_scr, qm_scr, attn_scr, mix_scr):
    layer = pl.program_id(0)
    blk = pl.program_id(1)
    rows = SAMPLE_SEQS * 8
    yrows = pl.ds(pl.multiple_of(blk * rows, rows), rows)

    @pl.when(layer == 0)
    def _():
        y_scr[yrows, :] = x_ref[...].reshape(rows, D_MODEL)

    x = y_scr[yrows, :]
    xb = x.astype(BF16)
    z = jnp.concatenate([_chunked_dot(xb, w_in_ref, IN_CHUNKS), _dot(xb, w_tail_ref[0])], axis=1)
    lvl = lvl_ref[...]
    attn_g = v512_ref[0:1, :]
    log_lb = v512_ref[1:2, :]
    log1m_lb = v512_ref[2:3, :]
    hg_g = v512_ref[3:4, :]

    qcols = [_rope(z[:, C_Q + c * LANES:C_Q + (c + 1) * LANES], cq_ref[...], sq_ref[...]) for c in range(4)]
    k_rot = _rope(z[:, C_K:C_K + KV_WIDTH], ckt_ref[...], skt_ref[...])
    v_new = z[:, C_V:C_V + KV_WIDTH]
    k_new_b = k_rot.astype(BF16)
    v_new_b = v_new.astype(BF16)
    lo2 = _lane_iota(2 * rows) < HEAD_DIM
    for g in range(2):
        xg = jnp.concatenate([qcols[2 * g], qcols[2 * g + 1]], axis=0)
        xsw = pltpu.roll(xg, HEAD_DIM, 1)
        keep = lo2 if g == 0 else jnp.logical_not(lo2)
        for e in range(2):
            qm = jnp.where(keep, xg if e == g else xsw, 0.0)
            for cc in range(2):
                r0 = ((g * 2 + e) * 2 + cc) * 8
                for s in range(SAMPLE_SEQS):
                    qm_scr[s, r0:r0 + 8, :] = qm[cc * rows + s * 8:cc * rows + s * 8 + 8, :]

    lf = _log_decay(z[:, C_HF:C_HF + HG_KW], log_lb, log1m_lb) * LOG2E
    kin = 1.0 - jnp.exp2(lf)
    lf_hi, lf_lo = _split_bf16(lf)
    eg = _dot(mlow_ref[...], lf_hi) + _dot(mlow_ref[...], lf_lo)
    dec_t = jnp.exp2(_dot_tn(lf_hi, seg_ref[...]) + _dot_tn(lf_lo, seg_ref[...]))
    o_intra, qe, kd, vh = [], [], [], []
    for h in range(HG_HEADS):
        hs = slice(h * LANES, (h + 1) * LANES)
        q_h = z[:, C_HQ + h * LANES:C_HQ + (h + 1) * LANES]
        k_h = kin[:, hs]
        v_h = z[:, C_HI + h * LANES:C_HI + (h + 1) * LANES]
        e_low = [eg[l * rows:(l + 1) * rows, hs] for l in range(LOW_LEVELS)]
        g_h = eg[LOW_LEVELS * rows:(LOW_LEVELS + 1) * rows, hs]
        tot_h = eg[(LOW_LEVELS + 1) * rows:(LOW_LEVELS + 2) * rows, hs]
        a = _hgrn_scores(q_h, k_h, e_low, g_h, lvl, LOW_LEVELS)
        o_intra.append(_dot(a.astype(BF16), v_h.astype(BF16)))
        qe.append(q_h * jnp.exp2(g_h))
        kd.append(k_h * jnp.exp2(tot_h - g_h))
        vh.append(v_h)

    grp = 4
    grows = grp * 64
    rid = lax.broadcasted_iota(jnp.int32, (grows, 1), 0)
    r8 = (rid >> 3) & 7
    sink_col = jnp.zeros((grows, 1), F32)
    for idx in range(8):
        g, e, cc = idx // 4, (idx // 2) % 2, idx % 2
        sink_col = jnp.where(r8 == idx, sinks_ref[layer * ATTN_HEADS + 4 * g + 2 * cc + e], sink_col)
    row = lax.broadcasted_iota(jnp.int32, (grows, 2 * WINDOW), 0)
    col = lax.broadcasted_iota(jnp.int32, (grows, 2 * WINDOW), 1)
    tok = row & 7
    new_col = col - WINDOW
    mask_cache = jnp.logical_and(col < WINDOW, col > tok)
    lo8 = _lane_iota(8) < HEAD_DIM
    o_inter = [[None] * SAMPLE_SEQS for _ in range(HG_HEADS)]
    for gi in range(SAMPLE_SEQS // grp):
        seqs = range(gi * grp, (gi + 1) * grp)
        kcs = {s: ck_ref[s] for s in seqs}
        vcs = {s: cv_ref[s] for s in seqs}
        sc = [_dot_nt(qm_scr[s].astype(BF16),
                      jnp.concatenate([kcs[s].astype(BF16), k_new_b], axis=0)) for s in seqs]
        for s in seqs:
            nk_ref[s, 0:WINDOW - 8, :] = kcs[s][8:WINDOW, :]
            nk_ref[s, WINDOW - 8:WINDOW, :] = k_rot[s * 8:s * 8 + 8, :]
            nv_ref[s, 0:WINDOW - 8, :] = vcs[s][8:WINDOW, :]
            nv_ref[s, WINDOW - 8:WINDOW, :] = v_new[s * 8:s * 8 + 8, :]
        mask_new = jnp.logical_and(jnp.logical_and(col >= WINDOW, (new_col >> 3) == (row >> 6) + gi * grp),
                                   (new_col & 7) <= tok)
        s_all = jnp.where(jnp.logical_or(mask_cache, mask_new), jnp.concatenate(sc, axis=0), NEG_INF)
        m = jnp.maximum(jnp.max(s_all, axis=-1, keepdims=True), sink_col)
        p = jnp.exp(s_all - m)
        den = jnp.sum(p, axis=-1, keepdims=True) + jnp.exp(sink_col - m)
        pb = p.astype(BF16)
        o_all = jnp.concatenate(
            [_dot(pb[i * 64:(i + 1) * 64, :], jnp.concatenate([vcs[s].astype(BF16), v_new_b], axis=0))
             for i, s in enumerate(seqs)], axis=0) / den
        o_sw = pltpu.roll(o_all, HEAD_DIM, 1)
        for i, s in enumerate(seqs):
            for c in range(4):
                g, cc = c // 2, c % 2
                ra = i * 64 + ((g * 2 + 0) * 2 + cc) * 8
                rb = i * 64 + ((g * 2 + 1) * 2 + cc) * 8
                part0 = (o_all if g == 0 else o_sw)[ra:ra + 8, :]
                part1 = (o_all if g == 1 else o_sw)[rb:rb + 8, :]
                attn_scr[c, s * 8:s * 8 + 8, :] = jnp.where(lo8, part0, part1)
        s0s = {(s, h): s0_ref[s, h] for s in seqs for h in range(HG_HEADS)}
        for s in seqs:
            for h in range(HG_HEADS):
                o_inter[h][s] = _dot(qe[h][s * 8:s * 8 + 8, :].astype(BF16), s0s[(s, h)].astype(BF16))
        upd = {(s, h): _dot_tn(kd[h][s * 8:s * 8 + 8, :].astype(BF16), vh[h][s * 8:s * 8 + 8, :].astype(BF16))
               for s in seqs for h in range(HG_HEADS)}
        for s in seqs:
            for h in range(HG_HEADS):
                dcol = dec_t[h * LANES:(h + 1) * LANES, s * 8:s * 8 + 1]
                ns_ref[s, h] = dcol * s0s[(s, h)] + upd[(s, h)]

    acols = _gated_rmsnorm_cols([attn_scr[c] for c in range(4)], attn_g, z[:, C_GA:C_GA + ATTN_WIDTH],
                                ATTN_WIDTH)
    for c in range(4):
        mix_scr[:, c * LANES:(c + 1) * LANES] = acols[c].astype(BF16)
    for h in range(HG_HEADS):
        hs = slice(h * LANES, (h + 1) * LANES)
        o_h = o_intra[h] + jnp.concatenate(o_inter[h], axis=0)
        gt = z[:, C_GH + h * LANES:C_GH + (h + 1) * LANES]
        inv = lax.rsqrt(jnp.mean(o_h * o_h, axis=-1, keepdims=True) + NORM_EPS)
        mix_scr[:, ATTN_WIDTH + h * LANES:ATTN_WIDTH + (h + 1) * LANES] = (
            o_h * inv * hg_g[:, hs] * _silu(gt)).astype(BF16)

    hpre = DN_ALPHA * x + _chunked_dot(mix_scr[...], w_out_ref, OUT_CHUNKS)
    hn = _layernorm(hpre, v1024_ref[0:1, :], v1024_ref[1:2, :])
    gate = _sigmoid(_chunked_dot(hn.astype(BF16), w_pg_ref, OUT_CHUNKS))
    pp = _chunked_dot(p_ref[...].reshape(rows, PLE_DIM).astype(BF16), w_pp_ref, OUT_CHUNKS)
    y = hn + gate * pp
    y_scr[yrows, :] = y
    y_ref[...] = y.reshape(SAMPLE_SEQS, 8, D_MODEL)


def _level_matrix():
    t = np.arange(BLK)[:, None]
    s = np.arange(BLK)[None, :]
    x = t ^ s
    lv = np.floor(np.log2(np.maximum(x, 1))).astype(np.int32)
    return np.where(t > s, lv, np.where(t == s, -1, -2)).astype(np.int32)


def _level_exponent_matrix(l):
    m = np.zeros((BLK, BLK), np.float32)
    b, h = 2 << l, 1 << l
    for t in range(BLK):
        mid = t - t % b + h
        if t >= mid:
            m[t, mid:t + 1] = 1.0
        else:
            m[t, t + 1:mid] = 1.0
    return m


def _mask_matrices(seq_rows):
    blocks = [_level_exponent_matrix(l) for l in range(LOW_LEVELS)]
    t = np.arange(BLK)[:, None]
    s = np.arange(BLK)[None, :]
    same = (t // seq_rows) == (s // seq_rows)
    blocks.append((same & (s <= t)).astype(np.float32))
    if seq_rows < BLK:
        blocks.append(same.astype(np.float32))
    return np.concatenate(blocks, axis=0)


def _rope_tables(pos, scale):
    half = HEAD_DIM // 2
    inv = jnp.exp(-math.log(ROPE_THETA) * jnp.arange(half, dtype=F32) * 2.0 / HEAD_DIM)
    ang = pos.astype(F32)[:, None] * inv[None, :]
    cos = jnp.cos(ang) * scale
    sin = jnp.sin(ang) * scale
    return jnp.tile(cos, (1, 4)), jnp.concatenate([-sin, sin, -sin, sin], axis=1)


def _column_chunks(w, width):
    d, k, n = w.shape
    return w.astype(BF16).reshape(d, k, n // width, width).transpose(0, 2, 1, 3)


def _const_spec(shape, layer=None):
    if layer is None:
        return pl.BlockSpec(shape, lambda *_: (0,) * len(shape))
    return pl.BlockSpec((None,) + shape, lambda *_: (layer,) + (0,) * len(shape))


def _prompt_layer(i, x, p_all, tables, lvl, mlow, weights, sinks, v512, v1024):
    B, T, _ = x.shape
    nt = T // PROMPT_TILE
    nb = B // PROMPT_SEQS
    tab_spec = pl.BlockSpec((PROMPT_TILE, LANES), lambda b, t: (t, 0))
    nblk = PROMPT_TILE // BLK
    assert nblk % IN_GROUP == 0 and nblk % 2 == 0

    def next_first_group(b, t):
        flat = jnp.minimum(b * nt + t + 1, nb * nt - 1)
        return (flat // nt, (flat % nt) * (nblk // IN_GROUP), 0)

    def wspec(shape):
        return pl.BlockSpec((None,) + shape, lambda b, t: (i,) + (0,) * len(shape),
                            pipeline_mode=pl.Buffered(1))
    in_specs = [
        pl.BlockSpec(memory_space=pltpu.SMEM),
        pl.BlockSpec((PROMPT_SEQS, PROMPT_TILE, D_MODEL), lambda b, t: (b, t, 0)),
        pl.BlockSpec((PROMPT_SEQS, IN_GROUP * BLK, D_MODEL), next_first_group),
        pl.BlockSpec((None, PROMPT_SEQS, PROMPT_TILE, PLE_DIM), lambda b, t: (i, b, t, 0)),
        tab_spec, tab_spec, tab_spec, tab_spec,
        _const_spec((BLK, BLK)),
        _const_spec(mlow.shape),
        wspec((IN_CHUNKS, D_MODEL, CHUNK_COLS)),
        wspec((1, D_MODEL, IN_TAIL)),
        wspec((OUT_CHUNKS, MIX_WIDTH, CHUNK_COLS)),
        wspec((OUT_CHUNKS, D_MODEL, CHUNK_COLS)),
        wspec((OUT_CHUNKS, PLE_DIM, CHUNK_COLS)),
        _const_spec((4, ATTN_WIDTH), i),
        _const_spec((2, D_MODEL), i),
    ]
    out_shape = [
        jax.ShapeDtypeStruct((B, T, D_MODEL), F32),
        jax.ShapeDtypeStruct((B, WINDOW, KV_WIDTH), F32),
        jax.ShapeDtypeStruct((B, WINDOW, KV_WIDTH), F32),
        jax.ShapeDtypeStruct((B, HG_HEADS, HG_DK, HG_DV), F32),
    ]
    out_specs = [
        pl.BlockSpec((PROMPT_SEQS, PROMPT_TILE, D_MODEL), lambda b, t: (b, t, 0)),
        pl.BlockSpec((PROMPT_SEQS, WINDOW, KV_WIDTH), lambda b, t: (b, 0, 0)),
        pl.BlockSpec((PROMPT_SEQS, WINDOW, KV_WIDTH), lambda b, t: (b, 0, 0)),
        pl.BlockSpec((PROMPT_SEQS, HG_HEADS, HG_DK, HG_DV), lambda b, t: (b, 0, 0, 0)),
    ]
    scratch = [
        pltpu.VMEM((PROMPT_SEQS, HG_HEADS, HG_DV, HG_DK), F32),
        pltpu.VMEM((PROMPT_SEQS, 4, BLK, LANES), BF16),
        pltpu.VMEM((PROMPT_SEQS, 4, BLK, LANES), BF16),
        pltpu.VMEM((PROMPT_SEQS, nblk, BLK, IN_COLS), F32),
        pltpu.VMEM((PROMPT_SEQS, 2, BLK, MIX_WIDTH), BF16),
    ]
    return pl.pallas_call(
        _prompt_kernel,
        grid=(nb, nt),
        in_specs=in_specs,
        out_specs=out_specs,
        out_shape=out_shape,
        scratch_shapes=scratch,
        compiler_params=pltpu.CompilerParams(
            dimension_semantics=("arbitrary", "arbitrary"), vmem_limit_bytes=VMEM_LIMIT),
        name=f"prompt_layer{i}",
    )(sinks, x, x, p_all, *tables, lvl, mlow, *weights, v512, v1024)


def _sample_call(x, p_all, ck_all, cv_all, s0_all, tables, lvl, mlow, seg, weights, sinks, v512, v1024):
    B, T, _ = x.shape
    rows = SAMPLE_SEQS * T

    def wspec(shape):
        return pl.BlockSpec((None,) + shape, lambda l, b: (l,) + (0,) * len(shape),
                            pipeline_mode=pl.Buffered(1))
    tab_spec = _const_spec((rows, LANES))
    in_specs = [
        pl.BlockSpec(memory_space=pltpu.SMEM),
        pl.BlockSpec((SAMPLE_SEQS, T, D_MODEL), lambda l, b: (jnp.where(l == 0, b, 0), 0, 0)),
        pl.BlockSpec((None, SAMPLE_SEQS, T, PLE_DIM), lambda l, b: (l, b, 0, 0)),
        pl.BlockSpec((None, SAMPLE_SEQS, WINDOW, KV_WIDTH), lambda l, b: (l, b, 0, 0)),
        pl.BlockSpec((None, SAMPLE_SEQS, WINDOW, KV_WIDTH), lambda l, b: (l, b, 0, 0)),
        pl.BlockSpec((None, SAMPLE_SEQS, HG_HEADS, HG_DK, HG_DV), lambda l, b: (l, b, 0, 0, 0)),
        tab_spec, tab_spec, tab_spec, tab_spec,
        _const_spec((BLK, BLK)),
        _const_spec(mlow.shape),
        _const_spec((BLK, LANES)),
        wspec((IN_CHUNKS, D_MODEL, CHUNK_COLS)),
        wspec((1, D_MODEL, IN_TAIL)),
        wspec((OUT_CHUNKS, MIX_WIDTH, CHUNK_COLS)),
        wspec((OUT_CHUNKS, D_MODEL, CHUNK_COLS)),
        wspec((OUT_CHUNKS, PLE_DIM, CHUNK_COLS)),
        pl.BlockSpec((None, 4, ATTN_WIDTH), lambda l, b: (l, 0, 0)),
        pl.BlockSpec((None, 2, D_MODEL), lambda l, b: (l, 0, 0)),
    ]
    out_shape = [
        jax.ShapeDtypeStruct((B, T, D_MODEL), F32),
        jax.ShapeDtypeStruct((DEPTH, B, WINDOW, KV_WIDTH), F32),
        jax.ShapeDtypeStruct((DEPTH, B, WINDOW, KV_WIDTH), F32),
        jax.ShapeDtypeStruct((DEPTH, B, HG_HEADS, HG_DK, HG_DV), F32),
    ]
    out_specs = [
        pl.BlockSpec((SAMPLE_SEQS, T, D_MODEL), lambda l, b: (jnp.where(l == DEPTH - 1, b, 0), 0, 0)),
        pl.BlockSpec((None, SAMPLE_SEQS, WINDOW, KV_WIDTH), lambda l, b: (l, b, 0, 0)),
        pl.BlockSpec((None, SAMPLE_SEQS, WINDOW, KV_WIDTH), lambda l, b: (l, b, 0, 0)),
        pl.BlockSpec((None, SAMPLE_SEQS, HG_HEADS, HG_DK, HG_DV), lambda l, b: (l, b, 0, 0, 0)),
    ]
    scratch = [
        pltpu.VMEM((B * T, D_MODEL), F32),
        pltpu.VMEM((SAMPLE_SEQS, 64, LANES), F32),
        pltpu.VMEM((4, rows, LANES), F32),
        pltpu.VMEM((rows, MIX_WIDTH), BF16),
    ]
    return pl.pallas_call(
        _sample_kernel,
        grid=(DEPTH, B // SAMPLE_SEQS),
        in_specs=in_specs,
        out_specs=out_specs,
        out_shape=out_shape,
        scratch_shapes=scratch,
        compiler_params=pltpu.CompilerParams(
            dimension_semantics=("arbitrary", "arbitrary"), vmem_limit_bytes=VMEM_LIMIT),
        name="sample_layers",
    )(sinks, x, p_all, ck_all, cv_all, s0_all, *tables, lvl, mlow, seg, *weights, v512, v1024)


def kernel(x_prompt, x_sample, cache_k_win, cache_v_win, state_hgrn, p_prompt, p_sample, w_in, attn_sinks,
           attn_norm_g, hg_lb_logits, hg_norm_g, w_out, ln_g, ln_b, w_ple_proj, w_ple_gate):
    B, T, _ = x_prompt.shape
    SB, ST, _ = x_sample.shape
    assert T % PROMPT_TILE == 0 and B % PROMPT_SEQS == 0
    assert SB % SAMPLE_SEQS == 0 and ST == 8 and SAMPLE_SEQS * ST == BLK
    assert cache_k_win.shape[2] == WINDOW

    cs = jnp.cumsum(jax.nn.softmax(hg_lb_logits.astype(F32), axis=0), axis=0)
    lbs = cs - cs[:1]
    v512 = jnp.stack([attn_norm_g.astype(F32), jnp.log(lbs), jnp.log1p(-lbs), hg_norm_g.astype(F32)], axis=1)
    v1024 = jnp.stack([ln_g.astype(F32), ln_b.astype(F32)], axis=1)
    n_main = IN_CHUNKS * CHUNK_COLS
    weights = (_column_chunks(w_in[:, :, :n_main], CHUNK_COLS), _column_chunks(w_in[:, :, n_main:], IN_TAIL),
               _column_chunks(w_out, CHUNK_COLS), _column_chunks(w_ple_gate, CHUNK_COLS),
               _column_chunks(w_ple_proj, CHUNK_COLS))
    sinks = attn_sinks.astype(F32)

    scale = HEAD_DIM ** -0.5
    pos_p = jnp.arange(T, dtype=jnp.int32)
    pos_s = jnp.tile(PAST_LEN + jnp.arange(ST, dtype=jnp.int32), SAMPLE_SEQS)
    tab_p = _rope_tables(pos_p, scale) + _rope_tables(pos_p, 1.0)
    tab_s = _rope_tables(pos_s, scale) + _rope_tables(pos_s, 1.0)
    lvl = jnp.asarray(_level_matrix())
    mlow_p = jnp.asarray(_mask_matrices(BLK), dtype=BF16)
    mlow_s = jnp.asarray(_mask_matrices(ST), dtype=BF16)
    seg = jnp.asarray((np.arange(BLK)[:, None] // ST) == (np.arange(LANES)[None, :] // ST), dtype=BF16)

    ck = cache_k_win.reshape(DEPTH, SB, WINDOW, KV_WIDTH)
    cv = cache_v_win.reshape(DEPTH, SB, WINDOW, KV_WIDTH)
    ys, ks, vs, ss = _sample_call(x_sample, p_sample, ck, cv, state_hgrn, tab_s, lvl, mlow_s, seg, weights,
                                  sinks.reshape(DEPTH * ATTN_HEADS), v512, v1024)

    yp = x_prompt
    kp_l, vp_l, sp_l = [], [], []
    for i in range(DEPTH):
        yp, kp, vp, sp = _prompt_layer(i, yp, p_prompt, tab_p, lvl, mlow_p, weights, sinks[i], v512, v1024)
        kp_l.append(kp); vp_l.append(vp); sp_l.append(sp)

    def kv5(a, b):
        return a.reshape(DEPTH, b, WINDOW, KV_HEADS, HEAD_DIM)

    return (yp, ys, kv5(jnp.stack(kp_l), B), kv5(jnp.stack(vp_l), B), jnp.stack(sp_l),
            kv5(ks, SB), kv5(vs, SB), ss)
```

```python
import math

import numpy as np
import jax
import jax.numpy as jnp
from jax import lax
from jax.experimental import pallas as pl
from jax.experimental.pallas import tpu as pltpu

D_MODEL = 1024
DEPTH = 4
PAST_LEN = 8192
ATTN_HEADS = 8
KV_HEADS = 2
HEAD_DIM = 64
ATTN_WIDTH = ATTN_HEADS * HEAD_DIM
KV_WIDTH = KV_HEADS * HEAD_DIM
WINDOW = 128
ROPE_THETA = 10000.0
HG_HEADS = 4
HG_DK = 128
HG_DV = 128
HG_KW = HG_HEADS * HG_DK
HG_VW = HG_HEADS * HG_DV
MIX_WIDTH = ATTN_WIDTH + HG_VW
IN_COLS = 2 * ATTN_WIDTH + 2 * KV_WIDTH + 2 * HG_KW + 2 * HG_VW
PLE_DIM = 256
DN_ALPHA = (2 * DEPTH) ** 0.25
NORM_EPS = 1e-5
NEG_INF = -1e30
LOG2E = 1.4426950408889634

C_Q = 0
C_K = C_Q + ATTN_WIDTH
C_V = C_K + KV_WIDTH
C_GA = C_V + KV_WIDTH
C_HQ = C_GA + ATTN_WIDTH
C_HF = C_HQ + HG_KW
C_HI = C_HF + HG_KW
C_GH = C_HI + HG_VW

LANES = 128
MXU_COLS = 256
CHUNK_COLS = 2 * MXU_COLS
BLK = 128
PROMPT_TILE = 1024
PROMPT_SEQS = 1
IN_GROUP = 2
SAMPLE_SEQS = 16
LOW_LEVELS = 3
PROMPT_LEVELS = 7
VMEM_LIMIT = 56 * 1024 * 1024
IN_CHUNKS = IN_COLS // CHUNK_COLS
IN_TAIL = IN_COLS - IN_CHUNKS * CHUNK_COLS
OUT_CHUNKS = D_MODEL // CHUNK_COLS

F32 = jnp.float32
BF16 = jnp.bfloat16


def _dot(a, b):
    return jnp.dot(a, b, preferred_element_type=F32)


def _dot_nt(a, b):
    return lax.dot_general(a, b, (((1,), (1,)), ((), ())), preferred_element_type=F32)


def _dot_tn(a, b):
    return lax.dot_general(a, b, (((0,), (0,)), ((), ())), preferred_element_type=F32)


def _split_bf16(x):
    hi = x.astype(BF16)
    lo = (x - hi.astype(F32)).astype(BF16)
    return hi, lo


def _sigmoid(x):
    return 0.5 * jnp.tanh(0.5 * x) + 0.5


def _silu(x):
    h = 0.5 * x
    return h * jnp.tanh(h) + h


def _lane_iota(rows):
    return lax.broadcasted_iota(jnp.int32, (rows, LANES), 1)


def _rope(x, cos_t, sin_t):
    lane = _lane_iota(x.shape[0])
    first_half = (lane & 32) == 0
    swapped = jnp.where(first_half, pltpu.roll(x, 96, 1), pltpu.roll(x, 32, 1))
    return x * cos_t + swapped * sin_t


def _kv_variants(a, fill):
    lane = _lane_iota(a.shape[0])
    lo = lane < HEAD_DIM
    sw = pltpu.roll(a, HEAD_DIM, 1)
    f = jnp.full_like(a, fill)
    out = [jnp.where(lo, a, f), jnp.where(lo, f, sw), jnp.where(lo, sw, f), jnp.where(lo, f, a)]
    return [o.astype(BF16) for o in out]


def _sum_half(e, es, rows):
    lane = _lane_iota(rows)
    sum_half = (lane >= HEAD_DIM) if e == 0 else (lane < HEAD_DIM)
    return jnp.where(sum_half, es, 0.0)


def _assemble_attn(res, rows):
    lane = _lane_iota(rows)
    lo = lane < HEAD_DIM
    cols = []
    for c in range(4):
        g, cc = c // 2, c % 2
        r0 = res[g * 2 + 0][cc * rows:(cc + 1) * rows]
        r1 = res[g * 2 + 1][cc * rows:(cc + 1) * rows]
        num = jnp.where(lo, r0, r1)
        den = pltpu.roll(jnp.where(lo, r1, r0), HEAD_DIM, 1)
        cols.append(num / den)
    return cols


def _log_decay(hf, log_lb, log1m_lb):
    ls = jnp.minimum(hf, 0.0) - jnp.log(1.0 + jnp.exp(-jnp.abs(hf)))
    b = log1m_lb + ls
    return jnp.maximum(log_lb, b) + jnp.log(1.0 + jnp.exp(-jnp.abs(log_lb - b)))


def _level_operand(l, q_h, k_h, e_low, g_h):
    if l < LOW_LEVELS:
        rows = lax.broadcasted_iota(jnp.int32, (BLK, LANES), 0)
        upper = ((rows >> l) & 1) == 1
        return jnp.where(upper, q_h, k_h) * jnp.exp2(e_low[l])
    b, h = 2 << l, 1 << l
    pieces = []
    for i in range(BLK // b):
        lower, upper = slice(i * b, i * b + h), slice(i * b + h, (i + 1) * b)
        g_mid = g_h[i * b + h - 1:i * b + h, :]
        pieces.append(k_h[lower, :] * jnp.exp2(g_mid - g_h[lower, :]))
        pieces.append(q_h[upper, :] * jnp.exp2(g_h[upper, :] - g_mid))
    return jnp.concatenate(pieces, axis=0)


def _hgrn_scores_steps(q_h, k_h, e_low, g_h, lvl, levels, out):
    a = jnp.zeros((BLK, BLK), F32)
    for l in range(levels):
        u = _level_operand(l, q_h, k_h, e_low, g_h).astype(BF16)
        a = jnp.where(lvl == l, _dot_nt(u, u), a)
        yield
    diag = jnp.sum(q_h * k_h, axis=-1, keepdims=True)
    out["a"] = jnp.where(lvl == -1, diag, a)


def _hgrn_scores(q_h, k_h, e_low, g_h, lvl, levels):
    out = {}
    for _ in _hgrn_scores_steps(q_h, k_h, e_low, g_h, lvl, levels, out):
        pass
    return out["a"]


def _gated_rmsnorm_cols(cols, gain_row, gate, width):
    ss = None
    for c in cols:
        s = jnp.sum(c * c, axis=-1, keepdims=True)
        ss = s if ss is None else ss + s
    inv = lax.rsqrt(ss * (1.0 / width) + NORM_EPS)
    out = []
    for i, c in enumerate(cols):
        gt = gate[:, i * LANES:(i + 1) * LANES]
        out.append(c * inv * gain_row[:, i * LANES:(i + 1) * LANES] * _silu(gt))
    return out


def _layernorm(hpre, ln_g, ln_b):
    mu = jnp.mean(hpre, axis=-1, keepdims=True)
    cen = hpre - mu
    var = jnp.mean(cen * cen, axis=-1, keepdims=True)
    return cen * lax.rsqrt(var + NORM_EPS) * ln_g + ln_b


def _chunked_dot(a, w_ref, n):
    return jnp.concatenate([_dot(a, w_ref[c]) for c in range(n)], axis=1)


def _merge(a, b):
    out, ia, ib = [], 0, 0
    while ia < len(a) or ib < len(b):
        if ib >= len(b) or (ia < len(a) and ia * len(b) <= ib * len(a)):
            out.append(a[ia])
            ia += 1
        else:
            out.append(b[ib])
            ib += 1
    return out


def _interleave(units, thunks):
    thunks = list(thunks)
    total, done, acc = len(thunks), 0, 0
    weight = sum(w for w, _ in units)
    for w, fn in units:
        fn()
        acc += w
        want = min(total, -(-acc * total // weight))
        while done < want:
            thunks[done]()
            done += 1
    while done < total:
        thunks[done]()
        done += 1


def _prompt_kernel(sinks_ref, x_ref, xn_ref, p_ref, cq_ref, sq_ref, ck_ref, sk_ref, lvl_ref, mlow_ref,
                   w_in_ref, w_tail_ref, w_out_ref, w_pg_ref, w_pp_ref, v512_ref, v1024_ref,
                   y_ref, kk_ref, vk_ref, sfin_ref,
                   st_scr, kprev_scr, vprev_scr, z_scr, mix_scr):
    t = pl.program_id(1)
    nblk = PROMPT_TILE // BLK

    @pl.when(t == 0)
    def _():
        st_scr[...] = jnp.zeros_like(st_scr)
        kprev_scr[...] = jnp.zeros_like(kprev_scr)
        vprev_scr[...] = jnp.zeros_like(vprev_scr)

    lvl = lvl_ref[...]
    attn_g = v512_ref[0:1, :]
    log_lb = v512_ref[1:2, :]
    log1m_lb = v512_ref[2:3, :]
    hg_g = v512_ref[3:4, :]

    row = lax.broadcasted_iota(jnp.int32, (2 * BLK, BLK), 0)
    col = lax.broadcasted_iota(jnp.int32, (2 * BLK, BLK), 1)
    from_prev = col > (row & (BLK - 1))
    first_rows = lax.broadcasted_iota(jnp.int32, (2 * BLK, 1), 0) < BLK
    carry = [{} for _ in range(PROMPT_SEQS)]

    ngrp = nblk // IN_GROUP
    grp_rows = IN_GROUP * BLK

    def stage_in(q, g):
        cell = {}
        first = (g % ngrp) * IN_GROUP

        def chunk(c):
            def run():
                if "xb" not in cell:
                    src = xn_ref[q] if g == ngrp else x_ref[q, g * grp_rows:(g + 1) * grp_rows, :]
                    cell["xb"] = src.astype(BF16)
                if c < IN_CHUNKS:
                    cols, zz = slice(c * CHUNK_COLS, (c + 1) * CHUNK_COLS), _dot(cell["xb"], w_in_ref[c])
                else:
                    cols, zz = slice(IN_CHUNKS * CHUNK_COLS, IN_COLS), _dot(cell["xb"], w_tail_ref[0])
                for i in range(IN_GROUP):
                    z_scr[q, first + i, :, cols] = zz[i * BLK:(i + 1) * BLK, :]
            return run
        return [chunk(c) for c in range(IN_CHUNKS + 1)]

    def stage_out(q, j):
        slot, rows_j, cell = j % 2, slice(j * BLK, (j + 1) * BLK), {"m": [], "g": []}

        def mix_chunk(c):
            def run():
                cell["m"].append(_dot(mix_scr[q, slot], w_out_ref[c]))
                if c + 1 == OUT_CHUNKS:
                    hpre = DN_ALPHA * x_ref[q, rows_j, :] + jnp.concatenate(cell.pop("m"), axis=1)
                    h = _layernorm(hpre, v1024_ref[0:1, :], v1024_ref[1:2, :])
                    cell["h"] = h
                    cell["hb"] = h.astype(BF16)
            return run

        def gate_chunk(c):
            def run():
                cell["g"].append(_sigmoid(_dot(cell["hb"], w_pg_ref[c])))
            return run

        def finish():
            pb = p_ref[q, rows_j, :].astype(BF16)
            for c in range(OUT_CHUNKS):
                cs = slice(c * CHUNK_COLS, (c + 1) * CHUNK_COLS)
                y_ref[q, rows_j, cs] = cell["h"][:, cs] + cell["g"][c] * _dot(pb, w_pp_ref[c])
        return ([mix_chunk(c) for c in range(OUT_CHUNKS)] + [gate_chunk(c) for c in range(OUT_CHUNKS)]
                + [finish])

    def attn_part(q, j):
        slot, rows_j = j % 2, slice(j * BLK, (j + 1) * BLK)

        def zc(c0, width):
            return z_scr[q, j, :, c0:c0 + width]
        qcols = [_rope(zc(C_Q + c * LANES, LANES), cq_ref[rows_j, :], sq_ref[rows_j, :]).astype(BF16)
                 for c in range(4)]
        k_rot = _rope(zc(C_K, KV_WIDTH), ck_ref[rows_j, :], sk_ref[rows_j, :])
        v_new = zc(C_V, KV_WIDTH)
        yield
        if j == 0:
            k_prev = [kprev_scr[q, i] for i in range(4)]
            v_prev = [vprev_scr[q, i] for i in range(4)]
        else:
            k_prev, v_prev = carry[q]["k_var"], carry[q]["v_var"]
        k_var = _kv_variants(k_rot, 0.0)
        v_var = _kv_variants(v_new, 1.0)
        carry[q]["k_var"], carry[q]["v_var"] = k_var, v_var
        if j == nblk - 1:
            for i in range(4):
                kprev_scr[q, i] = k_var[i]
                vprev_scr[q, i] = v_var[i]
            kk_ref[q] = k_rot
            vk_ref[q] = v_new
        yield
        res = []
        for g in range(2):
            qst = jnp.concatenate([qcols[2 * g], qcols[2 * g + 1]], axis=0)
            for e in range(2):
                i = g * 2 + e
                s_prev = _dot_nt(qst, k_prev[i])
                if j == 0:
                    s_prev = jnp.where(t > 0, s_prev, NEG_INF)
                s = jnp.where(from_prev, s_prev, _dot_nt(qst, k_var[i]))
                yield
                sink_col = jnp.where(first_rows, sinks_ref[4 * g + e], sinks_ref[4 * g + 2 + e])
                m = jnp.maximum(jnp.max(s, axis=-1, keepdims=True), sink_col)
                p = jnp.exp(s - m)
                es = jnp.exp(sink_col - m)
                yield
                p_both = jnp.concatenate([jnp.where(from_prev, p, 0.0), jnp.where(from_prev, 0.0, p)], axis=1)
                r = _dot(p_both.astype(BF16), jnp.concatenate([v_prev[i], v_var[i]], axis=0))
                res.append(r + _sum_half(e, es, 2 * BLK))
                yield
        acols = _assemble_attn(res, BLK)
        acols = _gated_rmsnorm_cols(acols, attn_g, zc(C_GA, ATTN_WIDTH), ATTN_WIDTH)
        for c in range(4):
            mix_scr[q, slot, :, c * LANES:(c + 1) * LANES] = acols[c].astype(BF16)
        yield

    def hgrn_part(q, j):
        slot = j % 2

        def zc(c0, width):
            return z_scr[q, j, :, c0:c0 + width]
        lf = _log_decay(zc(C_HF, HG_KW), log_lb, log1m_lb) * LOG2E
        kin = 1.0 - jnp.exp2(lf)
        lf_hi, lf_lo = _split_bf16(lf)
        m_low = mlow_ref[0:LOW_LEVELS * BLK, :]
        m_cum = mlow_ref[LOW_LEVELS * BLK:(LOW_LEVELS + 1) * BLK, :]
        eg = jnp.concatenate([_dot(m_low, lf_hi), _dot(m_cum, lf_hi) + _dot(m_cum, lf_lo)], axis=0)
        yield
        for h in range(HG_HEADS):
            hs = slice(h * LANES, (h + 1) * LANES)
            q_h = zc(C_HQ + h * LANES, LANES)
            k_h = kin[:, hs]
            v_h = zc(C_HI + h * LANES, LANES).astype(BF16)
            e_low = [eg[l * BLK:(l + 1) * BLK, hs] for l in range(LOW_LEVELS)]
            g_h = eg[LOW_LEVELS * BLK:(LOW_LEVELS + 1) * BLK, hs]
            out = {}
            yield from _hgrn_scores_steps(q_h, k_h, e_low, g_h, lvl, PROMPT_LEVELS, out)
            st = st_scr[q, h]
            o_h = (_dot(out["a"].astype(BF16), v_h)
                   + _dot_nt((q_h * jnp.exp2(g_h)).astype(BF16), st.astype(BF16)))
            g_last = g_h[BLK - 1:BLK, :]
            kd = (k_h * jnp.exp2(g_last - g_h)).astype(BF16)
            st_scr[q, h] = st * jnp.exp2(g_last) + _dot_tn(v_h, kd)
            yield
            gt = zc(C_GH + h * LANES, LANES)
            inv = lax.rsqrt(jnp.mean(o_h * o_h, axis=-1, keepdims=True) + NORM_EPS)
            mix_scr[q, slot, :, ATTN_WIDTH + h * LANES:ATTN_WIDTH + (h + 1) * LANES] = (
                o_h * inv * hg_g[:, hs] * _silu(gt)).astype(BF16)
            yield

    def mix_units(q, j):
        slot, rows_j, s = j % 2, slice(j * BLK, (j + 1) * BLK), {}

        def zc(c0, width):
            return z_scr[q, j, :, c0:c0 + width]

        def a_rope():
            qcols = [_rope(zc(C_Q + c * LANES, LANES), cq_ref[rows_j, :], sq_ref[rows_j, :]).astype(BF16)
                     for c in range(4)]
            k_rot = _rope(zc(C_K, KV_WIDTH), ck_ref[rows_j, :], sk_ref[rows_j, :])
            v_new = zc(C_V, KV_WIDTH)
            if j == 0:
                s["k_prev"] = [kprev_scr[q, i] for i in range(4)]
                s["v_prev"] = [vprev_scr[q, i] for i in range(4)]
            else:
                s["k_prev"], s["v_prev"] = carry[q]["k_var"], carry[q]["v_var"]
            k_var = _kv_variants(k_rot, 0.0)
            v_var = _kv_variants(v_new, 1.0)
            carry[q]["k_var"], carry[q]["v_var"] = k_var, v_var
            s["k_var"], s["v_var"] = k_var, v_var
            if j == nblk - 1:
                for i in range(4):
                    kprev_scr[q, i] = k_var[i]
                    vprev_scr[q, i] = v_var[i]
                kk_ref[q] = k_rot
                vk_ref[q] = v_new
            s["qst"] = [jnp.concatenate([qcols[2 * g], qcols[2 * g + 1]], axis=0) for g in range(2)]
            s["scores"], s["probs"], s["res"] = {}, {}, {}

        def a_scores(i):
            def run():
                s_prev = _dot_nt(s["qst"][i // 2], s["k_prev"][i])
                if j == 0:
                    s_prev = jnp.where(t > 0, s_prev, NEG_INF)
                s["scores"][i] = jnp.where(from_prev, s_prev, _dot_nt(s["qst"][i // 2], s["k_var"][i]))
            return run

        def a_softmax(i):
            def run():
                g, e = i // 2, i % 2
                sc = s["scores"].pop(i)
                sink_col = jnp.where(first_rows, sinks_ref[4 * g + e], sinks_ref[4 * g + 2 + e])
                m = jnp.maximum(jnp.max(sc, axis=-1, keepdims=True), sink_col)
                p = jnp.exp(sc - m)
                p_both = jnp.concatenate([jnp.where(from_prev, p, 0.0), jnp.where(from_prev, 0.0, p)], axis=1)
                s["probs"][i] = (p_both.astype(BF16), jnp.exp(sink_col - m))
            return run

        def a_values(i):
            def run():
                p_both, es = s["probs"].pop(i)
                r = _dot(p_both, jnp.concatenate([s["v_prev"][i], s["v_var"][i]], axis=0))
                s["res"][i] = r + _sum_half(i % 2, es, 2 * BLK)
            return run

        def a_finish():
            acols = _assemble_attn([s["res"][i] for i in range(4)], BLK)
            acols = _gated_rmsnorm_cols(acols, attn_g, zc(C_GA, ATTN_WIDTH), ATTN_WIDTH)
            for c in range(4):
                mix_scr[q, slot, :, c * LANES:(c + 1) * LANES] = acols[c].astype(BF16)

        def h_prep():
            lf = _log_decay(zc(C_HF, HG_KW), log_lb, log1m_lb) * LOG2E
            kin = 1.0 - jnp.exp2(lf)
            lf_hi, lf_lo = _split_bf16(lf)
            m_low = mlow_ref[0:LOW_LEVELS * BLK, :]
            m_cum = mlow_ref[LOW_LEVELS * BLK:(LOW_LEVELS + 1) * BLK, :]
            eg = jnp.concatenate([_dot(m_low, lf_hi), _dot(m_cum, lf_hi) + _dot(m_cum, lf_lo)], axis=0)
            s["heads"] = []
            for h in range(HG_HEADS):
                hs = slice(h * LANES, (h + 1) * LANES)
                s["heads"].append(dict(
                    q=zc(C_HQ + h * LANES, LANES), k=kin[:, hs],
                    e_low=[eg[l * BLK:(l + 1) * BLK, hs] for l in range(LOW_LEVELS)],
                    g=eg[LOW_LEVELS * BLK:(LOW_LEVELS + 1) * BLK, hs], u=[], p=[]))

        def h_operand(h, l):
            def run():
                hd = s["heads"][h]
                hd["u"].append(_level_operand(l, hd["q"], hd["k"], hd["e_low"], hd["g"]).astype(BF16))
            return run

        def h_products(h):
            def run():
                hd = s["heads"][h]
                hd["p"] = [_dot_nt(u, u) for u in hd.pop("u")]
            return run

        def h_combine(h):
            def run():
                hd = s["heads"][h]
                a = jnp.zeros((BLK, BLK), F32)
                for l, p in enumerate(hd.pop("p")):
                    a = jnp.where(lvl == l, p, a)
                diag = jnp.sum(hd["q"] * hd["k"], axis=-1, keepdims=True)
                hd["a"] = jnp.where(lvl == -1, diag, a).astype(BF16)
                g_last = hd["g"][BLK - 1:BLK, :]
                hd["qe"] = (hd["q"] * jnp.exp2(hd["g"])).astype(BF16)
                hd["kd"] = (hd["k"] * jnp.exp2(g_last - hd["g"])).astype(BF16)
                hd["dec"] = jnp.exp2(g_last)
            return run

        def h_output(h):
            def run():
                hd = s["heads"][h]
                v_h = zc(C_HI + h * LANES, LANES).astype(BF16)
                st = st_scr[q, h]
                hd["o"] = _dot(hd.pop("a"), v_h) + _dot_nt(hd.pop("qe"), st.astype(BF16))
                st_scr[q, h] = st * hd.pop("dec") + _dot_tn(v_h, hd.pop("kd"))
            return run

        def h_finish(h):
            def run():
                o_h = s["heads"][h].pop("o")
                gt = zc(C_GH + h * LANES, LANES)
                inv = lax.rsqrt(jnp.mean(o_h * o_h, axis=-1, keepdims=True) + NORM_EPS)
                mix_scr[q, slot, :, ATTN_WIDTH + h * LANES:ATTN_WIDTH + (h + 1) * LANES] = (
                    o_h * inv * hg_g[:, h * LANES:(h + 1) * LANES] * _silu(gt)).astype(BF16)
            return run

        heads, variants = range(HG_HEADS), range(4)
        return ([(40, a_rope), (90, h_prep)]
                + _merge([(3, a_scores(i)) for i in variants],
                         [(3, h_operand(h, l)) for h in heads for l in range(PROMPT_LEVELS)])
                + _merge([(14, a_softmax(i)) for i in variants], [(1, h_products(h)) for h in heads])
                + _merge([(4, a_values(i)) for i in variants], [(22, h_combine(h)) for h in heads])
                + _merge([(56, a_finish)], [(3, h_output(h)) for h in heads])
                + [(8, h_finish(h)) for h in heads])

    def stage_mix(j):
        units = mix_units(0, j)
        for q in range(1, PROMPT_SEQS):
            units = _merge(units, mix_units(q, j))
        return units

    @pl.when(jnp.logical_and(pl.program_id(0) == 0, t == 0))
    def _():
        for q in range(PROMPT_SEQS):
            for th in stage_in(q, 0):
                th()

    def both(stage, j):
        lists = [stage(q, j) for q in range(PROMPT_SEQS)]
        return [th for group in zip(*lists) for th in group]

    for j in range(nblk):
        nxt = both(stage_in, j // IN_GROUP + 1)
        part = j % IN_GROUP
        per = -(-len(nxt) // IN_GROUP)
        side = nxt[part * per:(part + 1) * per]
        if j >= 1:
            side = side + both(stage_out, j - 1)
        _interleave(stage_mix(j), side)
    for th in both(stage_out, nblk - 1):
        th()

    @pl.when(t == pl.num_programs(1) - 1)
    def _():
        for q in range(PROMPT_SEQS):
            for h in range(HG_HEADS):
                sfin_ref[q, h] = st_scr[q, h].T


def _sample_kernel(sinks_ref, x_ref, p_ref, ck_ref, cv_ref, s0_ref, cq_ref, sq_ref, ckt_ref, skt_ref,
                   lvl_ref, mlow_ref, seg_ref, w_in_ref, w_tail_ref, w_out_ref, w_pg_ref, w_pp_ref, v512_ref,
                   v1024_ref,
                   y_ref, nk_ref, nv_ref, ns_ref,
                   y_scr, qm_scr, attn_scr, mix_scr):
    layer = pl.program_id(0)
    blk = pl.program_id(1)
    rows = SAMPLE_SEQS * 8
    yrows = pl.ds(pl.multiple_of(blk * rows, rows), rows)

    @pl.when(layer == 0)
    def _():
        y_scr[yrows, :] = x_ref[...].reshape(rows, D_MODEL)

    x = y_scr[yrows, :]
    xb = x.astype(BF16)
    z = jnp.concatenate([_chunked_dot(xb, w_in_ref, IN_CHUNKS), _dot(xb, w_tail_ref[0])], axis=1)
    lvl = lvl_ref[...]
    attn_g = v512_ref[0:1, :]
    log_lb = v512_ref[1:2, :]
    log1m_lb = v512_ref[2:3, :]
    hg_g = v512_ref[3:4, :]

    qcols = [_rope(z[:, C_Q + c * LANES:C_Q + (c + 1) * LANES], cq_ref[...], sq_ref[...]) for c in range(4)]
    k_rot = _rope(z[:, C_K:C_K + KV_WIDTH], ckt_ref[...], skt_ref[...])
    v_new = z[:, C_V:C_V + KV_WIDTH]
    k_new_b = k_rot.astype(BF16)
    v_new_b = v_new.astype(BF16)
    lo2 = _lane_iota(2 * rows) < HEAD_DIM
    for g in range(2):
        xg = jnp.concatenate([qcols[2 * g], qcols[2 * g + 1]], axis=0)
        xsw = pltpu.roll(xg, HEAD_DIM, 1)
        keep = lo2 if g == 0 else jnp.logical_not(lo2)
        for e in range(2):
            qm = jnp.where(keep, xg if e == g else xsw, 0.0)
            for cc in range(2):
                r0 = ((g * 2 + e) * 2 + cc) * 8
                for s in range(SAMPLE_SEQS):
                    qm_scr[s, r0:r0 + 8, :] = qm[cc * rows + s * 8:cc * rows + s * 8 + 8, :]

    lf = _log_decay(z[:, C_HF:C_HF + HG_KW], log_lb, log1m_lb) * LOG2E
    kin = 1.0 - jnp.exp2(lf)
    lf_hi, lf_lo = _split_bf16(lf)
    eg = _dot(mlow_ref[...], lf_hi) + _dot(mlow_ref[...], lf_lo)
    dec_t = jnp.exp2(_dot_tn(lf_hi, seg_ref[...]) + _dot_tn(lf_lo, seg_ref[...]))
    o_intra, qe, kd, vh = [], [], [], []
    for h in range(HG_HEADS):
        hs = slice(h * LANES, (h + 1) * LANES)
        q_h = z[:, C_HQ + h * LANES:C_HQ + (h + 1) * LANES]
        k_h = kin[:, hs]
        v_h = z[:, C_HI + h * LANES:C_HI + (h + 1) * LANES]
        e_low = [eg[l * rows:(l + 1) * rows, hs] for l in range(LOW_LEVELS)]
        g_h = eg[LOW_LEVELS * rows:(LOW_LEVELS + 1) * rows, hs]
        tot_h = eg[(LOW_LEVELS + 1) * rows:(LOW_LEVELS + 2) * rows, hs]
        a = _hgrn_scores(q_h, k_h, e_low, g_h, lvl, LOW_LEVELS)
        o_intra.append(_dot(a.astype(BF16), v_h.astype(BF16)))
        qe.append(q_h * jnp.exp2(g_h))
        kd.append(k_h * jnp.exp2(tot_h - g_h))
        vh.append(v_h)

    grp = 4
    grows = grp * 64
    rid = lax.broadcasted_iota(jnp.int32, (grows, 1), 0)
    r8 = (rid >> 3) & 7
    sink_col = jnp.zeros((grows, 1), F32)
    for idx in range(8):
        g, e, cc = idx // 4, (idx // 2) % 2, idx % 2
        sink_col = jnp.where(r8 == idx, sinks_ref[layer * ATTN_HEADS + 4 * g + 2 * cc + e], sink_col)
    row = lax.broadcasted_iota(jnp.int32, (grows, 2 * WINDOW), 0)
    col = lax.broadcasted_iota(jnp.int32, (grows, 2 * WINDOW), 1)
    tok = row & 7
    new_col = col - WINDOW
    mask_cache = jnp.logical_and(col < WINDOW, col > tok)
    lo8 = _lane_iota(8) < HEAD_DIM
    o_inter = [[None] * SAMPLE_SEQS for _ in range(HG_HEADS)]
    for gi in range(SAMPLE_SEQS // grp):
        seqs = range(gi * grp, (gi + 1) * grp)
        kcs = {s: ck_ref[s] for s in seqs}
        vcs = {s: cv_ref[s] for s in seqs}
        sc = [_dot_nt(qm_scr[s].astype(BF16),
                      jnp.concatenate([kcs[s].astype(BF16), k_new_b], axis=0)) for s in seqs]
        for s in seqs:
            nk_ref[s, 0:WINDOW - 8, :] = kcs[s][8:WINDOW, :]
            nk_ref[s, WINDOW - 8:WINDOW, :] = k_rot[s * 8:s * 8 + 8, :]
            nv_ref[s, 0:WINDOW - 8, :] = vcs[s][8:WINDOW, :]
            nv_ref[s, WINDOW - 8:WINDOW, :] = v_new[s * 8:s * 8 + 8, :]
        mask_new = jnp.logical_and(jnp.logical_and(col >= WINDOW, (new_col >> 3) == (row >> 6) + gi * grp),
                                   (new_col & 7) <= tok)
        s_all = jnp.where(jnp.logical_or(mask_cache, mask_new), jnp.concatenate(sc, axis=0), NEG_INF)
        m = jnp.maximum(jnp.max(s_all, axis=-1, keepdims=True), sink_col)
        p = jnp.exp(s_all - m)
        den = jnp.sum(p, axis=-1, keepdims=True) + jnp.exp(sink_col - m)
        pb = p.astype(BF16)
        o_all = jnp.concatenate(
            [_dot(pb[i * 64:(i + 1) * 64, :], jnp.concatenate([vcs[s].astype(BF16), v_new_b], axis=0))
             for i, s in enumerate(seqs)], axis=0) / den
        o_sw = pltpu.roll(o_all, HEAD_DIM, 1)
        for i, s in enumerate(seqs):
            for c in range(4):
                g, cc = c // 2, c % 2
                ra = i * 64 + ((g * 2 + 0) * 2 + cc) * 8
                rb = i * 64 + ((g * 2 + 1) * 2 + cc) * 8
                part0 = (o_all if g == 0 else o_sw)[ra:ra + 8, :]
                part1 = (o_all if g == 1 else o_sw)[rb:rb + 8, :]
                attn_scr[c, s * 8:s * 8 + 8, :] = jnp.where(lo8, part0, part1)
        s0s = {(s, h): s0_ref[s, h] for s in seqs for h in range(HG_HEADS)}
        for s in seqs:
            for h in range(HG_HEADS):
                o_inter[h][s] = _dot(qe[h][s * 8:s * 8 + 8, :].astype(BF16), s0s[(s, h)].astype(BF16))
        upd = {(s, h): _dot_tn(kd[h][s * 8:s * 8 + 8, :].astype(BF16), vh[h][s * 8:s * 8 + 8, :].astype(BF16))
               for s in seqs for h in range(HG_HEADS)}
        for s in seqs:
            for h in range(HG_HEADS):
                dcol = dec_t[h * LANES:(h + 1) * LANES, s * 8:s * 8 + 1]
                ns_ref[s, h] = dcol * s0s[(s, h)] + upd[(s, h)]

    acols = _gated_rmsnorm_cols([attn_scr[c] for c in range(4)], attn_g, z[:, C_GA:C_GA + ATTN_WIDTH],
                                ATTN_WIDTH)
    for c in range(4):
        mix_scr[:, c * LANES:(c + 1) * LANES] = acols[c].astype(BF16)
    for h in range(HG_HEADS):
        hs = slice(h * LANES, (h + 1) * LANES)
        o_h = o_intra[h] + jnp.concatenate(o_inter[h], axis=0)
        gt = z[:, C_GH + h * LANES:C_GH + (h + 1) * LANES]
        inv = lax.rsqrt(jnp.mean(o_h * o_h, axis=-1, keepdims=True) + NORM_EPS)
        mix_scr[:, ATTN_WIDTH + h * LANES:ATTN_WIDTH + (h + 1) * LANES] = (
            o_h * inv * hg_g[:, hs] * _silu(gt)).astype(BF16)

    hpre = DN_ALPHA * x + _chunked_dot(mix_scr[...], w_out_ref, OUT_CHUNKS)
    hn = _layernorm(hpre, v1024_ref[0:1, :], v1024_ref[1:2, :])
    gate = _sigmoid(_chunked_dot(hn.astype(BF16), w_pg_ref, OUT_CHUNKS))
    pp = _chunked_dot(p_ref[...].reshape(rows, PLE_DIM).astype(BF16), w_pp_ref, OUT_CHUNKS)
    y = hn + gate * pp
    y_scr[yrows, :] = y
    y_ref[...] = y.reshape(SAMPLE_SEQS, 8, D_MODEL)


def _level_matrix():
    t = np.arange(BLK)[:, None]
    s = np.arange(BLK)[None, :]
    x = t ^ s
    lv = np.floor(np.log2(np.maximum(x, 1))).astype(np.int32)
    return np.where(t > s, lv, np.where(t == s, -1, -2)).astype(np.int32)


def _level_exponent_matrix(l):
    m = np.zeros((BLK, BLK), np.float32)
    b, h = 2 << l, 1 << l
    for t in range(BLK):
        mid = t - t % b + h
        if t >= mid:
            m[t, mid:t + 1] = 1.0
        else:
            m[t, t + 1:mid] = 1.0
    return m


def _mask_matrices(seq_rows):
    blocks = [_level_exponent_matrix(l) for l in range(LOW_LEVELS)]
    t = np.arange(BLK)[:, None]
    s = np.arange(BLK)[None, :]
    same = (t // seq_rows) == (s // seq_rows)
    blocks.append((same & (s <= t)).astype(np.float32))
    if seq_rows < BLK:
        blocks.append(same.astype(np.float32))
    return np.concatenate(blocks, axis=0)


def _rope_tables(pos, scale):
    half = HEAD_DIM // 2
    inv = jnp.exp(-math.log(ROPE_THETA) * jnp.arange(half, dtype=F32) * 2.0 / HEAD_DIM)
    ang = pos.astype(F32)[:, None] * inv[None, :]
    cos = jnp.cos(ang) * scale
    sin = jnp.sin(ang) * scale
    return jnp.tile(cos, (1, 4)), jnp.concatenate([-sin, sin, -sin, sin], axis=1)


def _column_chunks(w, width):
    d, k, n = w.shape
    return w.astype(BF16).reshape(d, k, n // width, width).transpose(0, 2, 1, 3)


def _const_spec(shape, layer=None):
    if layer is None:
        return pl.BlockSpec(shape, lambda *_: (0,) * len(shape))
    return pl.BlockSpec((None,) + shape, lambda *_: (layer,) + (0,) * len(shape))


def _prompt_layer(i, x, p_all, tables, lvl, mlow, weights, sinks, v512, v1024):
    B, T, _ = x.shape
    nt = T // PROMPT_TILE
    nb = B // PROMPT_SEQS
    tab_spec = pl.BlockSpec((PROMPT_TILE, LANES), lambda b, t: (t, 0))
    nblk = PROMPT_TILE // BLK
    assert nblk % IN_GROUP == 0 and nblk % 2 == 0

    def next_first_group(b, t):
        flat = jnp.minimum(b * nt + t + 1, nb * nt - 1)
        return (flat // nt, (flat % nt) * (nblk // IN_GROUP), 0)

    def wspec(shape):
        return pl.BlockSpec((None,) + shape, lambda b, t: (i,) + (0,) * len(shape),
                            pipeline_mode=pl.Buffered(1))
    in_specs = [
        pl.BlockSpec(memory_space=pltpu.SMEM),
        pl.BlockSpec((PROMPT_SEQS, PROMPT_TILE, D_MODEL), lambda b, t: (b, t, 0)),
        pl.BlockSpec((PROMPT_SEQS, IN_GROUP * BLK, D_MODEL), next_first_group),
        pl.BlockSpec((None, PROMPT_SEQS, PROMPT_TILE, PLE_DIM), lambda b, t: (i, b, t, 0)),
        tab_spec, tab_spec, tab_spec, tab_spec,
        _const_spec((BLK, BLK)),
        _const_spec(mlow.shape),
        wspec((IN_CHUNKS, D_MODEL, CHUNK_COLS)),
        wspec((1, D_MODEL, IN_TAIL)),
        wspec((OUT_CHUNKS, MIX_WIDTH, CHUNK_COLS)),
        wspec((OUT_CHUNKS, D_MODEL, CHUNK_COLS)),
        wspec((OUT_CHUNKS, PLE_DIM, CHUNK_COLS)),
        _const_spec((4, ATTN_WIDTH), i),
        _const_spec((2, D_MODEL), i),
    ]
    out_shape = [
        jax.ShapeDtypeStruct((B, T, D_MODEL), F32),
        jax.ShapeDtypeStruct((B, WINDOW, KV_WIDTH), F32),
        jax.ShapeDtypeStruct((B, WINDOW, KV_WIDTH), F32),
        jax.ShapeDtypeStruct((B, HG_HEADS, HG_DK, HG_DV), F32),
    ]
    out_specs = [
        pl.BlockSpec((PROMPT_SEQS, PROMPT_TILE, D_MODEL), lambda b, t: (b, t, 0)),
        pl.BlockSpec((PROMPT_SEQS, WINDOW, KV_WIDTH), lambda b, t: (b, 0, 0)),
        pl.BlockSpec((PROMPT_SEQS, WINDOW, KV_WIDTH), lambda b, t: (b, 0, 0)),
        pl.BlockSpec((PROMPT_SEQS, HG_HEADS, HG_DK, HG_DV), lambda b, t: (b, 0, 0, 0)),
    ]
    scratch = [
        pltpu.VMEM((PROMPT_SEQS, HG_HEADS, HG_DV, HG_DK), F32),
        pltpu.VMEM((PROMPT_SEQS, 4, BLK, LANES), BF16),
        pltpu.VMEM((PROMPT_SEQS, 4, BLK, LANES), BF16),
        pltpu.VMEM((PROMPT_SEQS, nblk, BLK, IN_COLS), F32),
        pltpu.VMEM((PROMPT_SEQS, 2, BLK, MIX_WIDTH), BF16),
    ]
    return pl.pallas_call(
        _prompt_kernel,
        grid=(nb, nt),
        in_specs=in_specs,
        out_specs=out_specs,
        out_shape=out_shape,
        scratch_shapes=scratch,
        compiler_params=pltpu.CompilerParams(
            dimension_semantics=("arbitrary", "arbitrary"), vmem_limit_bytes=VMEM_LIMIT),
        name=f"prompt_layer{i}",
    )(sinks, x, x, p_all, *tables, lvl, mlow, *weights, v512, v1024)


def _sample_call(x, p_all, ck_all, cv_all, s0_all, tables, lvl, mlow, seg, weights, sinks, v512, v1024):
    B, T, _ = x.shape
    rows = SAMPLE_SEQS * T

    def wspec(shape):
        return pl.BlockSpec((None,) + shape, lambda l, b: (l,) + (0,) * len(shape),
                            pipeline_mode=pl.Buffered(1))
    tab_spec = _const_spec((rows, LANES))
    in_specs = [
        pl.BlockSpec(memory_space=pltpu.SMEM),
        pl.BlockSpec((SAMPLE_SEQS, T, D_MODEL), lambda l, b: (jnp.where(l == 0, b, 0), 0, 0)),
        pl.BlockSpec((None, SAMPLE_SEQS, T, PLE_DIM), lambda l, b: (l, b, 0, 0)),
        pl.BlockSpec((None, SAMPLE_SEQS, WINDOW, KV_WIDTH), lambda l, b: (l, b, 0, 0)),
        pl.BlockSpec((None, SAMPLE_SEQS, WINDOW, KV_WIDTH), lambda l, b: (l, b, 0, 0)),
        pl.BlockSpec((None, SAMPLE_SEQS, HG_HEADS, HG_DK, HG_DV), lambda l, b: (l, b, 0, 0, 0)),
        tab_spec, tab_spec, tab_spec, tab_spec,
        _const_spec((BLK, BLK)),
        _const_spec(mlow.shape),
        _const_spec((BLK, LANES)),
        wspec((IN_CHUNKS, D_MODEL, CHUNK_COLS)),
        wspec((1, D_MODEL, IN_TAIL)),
        wspec((OUT_CHUNKS, MIX_WIDTH, CHUNK_COLS)),
        wspec((OUT_CHUNKS, D_MODEL, CHUNK_COLS)),
        wspec((OUT_CHUNKS, PLE_DIM, CHUNK_COLS)),
        pl.BlockSpec((None, 4, ATTN_WIDTH), lambda l, b: (l, 0, 0)),
        pl.BlockSpec((None, 2, D_MODEL), lambda l, b: (l, 0, 0)),
    ]
    out_shape = [
        jax.ShapeDtypeStruct((B, T, D_MODEL), F32),
        jax.ShapeDtypeStruct((DEPTH, B, WINDOW, KV_WIDTH), F32),
        jax.ShapeDtypeStruct((DEPTH, B, WINDOW, KV_WIDTH), F32),
        jax.ShapeDtypeStruct((DEPTH, B, HG_HEADS, HG_DK, HG_DV), F32),
    ]
    out_specs = [
        pl.BlockSpec((SAMPLE_SEQS, T, D_MODEL), lambda l, b: (jnp.where(l == DEPTH - 1, b, 0), 0, 0)),
        pl.BlockSpec((None, SAMPLE_SEQS, WINDOW, KV_WIDTH), lambda l, b: (l, b, 0, 0)),
        pl.BlockSpec((None, SAMPLE_SEQS, WINDOW, KV_WIDTH), lambda l, b: (l, b, 0, 0)),
        pl.BlockSpec((None, SAMPLE_SEQS, HG_HEADS, HG_DK, HG_DV), lambda l, b: (l, b, 0, 0, 0)),
    ]
    scratch = [
        pltpu.VMEM((B * T, D_MODEL), F32),
        pltpu.VMEM((SAMPLE_SEQS, 64, LANES), F32),
        pltpu.VMEM((4, rows, LANES), F32),
        pltpu.VMEM((rows, MIX_WIDTH), BF16),
    ]
    return pl.pallas_call(
        _sample_kernel,
        grid=(DEPTH, B // SAMPLE_SEQS),
        in_specs=in_specs,
        out_specs=out_specs,
        out_shape=out_shape,
        scratch_shapes=scratch,
        compiler_params=pltpu.CompilerParams(
            dimension_semantics=("arbitrary", "arbitrary"), vmem_limit_bytes=VMEM_LIMIT),
        name="sample_layers",
    )(sinks, x, p_all, ck_all, cv_all, s0_all, *tables, lvl, mlow, seg, *weights, v512, v1024)


def kernel(x_prompt, x_sample, cache_k_win, cache_v_win, state_hgrn, p_prompt, p_sample, w_in, attn_sinks,
           attn_norm_g, hg_lb_logits, hg_norm_g, w_out, ln_g, ln_b, w_ple_proj, w_ple_gate):
    B, T, _ = x_prompt.shape
    SB, ST, _ = x_sample.shape
    assert T % PROMPT_TILE == 0 and B % PROMPT_SEQS == 0
    assert SB % SAMPLE_SEQS == 0 and ST == 8 and SAMPLE_SEQS * ST == BLK
    assert cache_k_win.shape[2] == WINDOW

    cs = jnp.cumsum(jax.nn.softmax(hg_lb_logits.astype(F32), axis=0), axis=0)
    lbs = cs - cs[:1]
    v512 = jnp.stack([attn_norm_g.astype(F32), jnp.log(lbs), jnp.log1p(-lbs), hg_norm_g.astype(F32)], axis=1)
    v1024 = jnp.stack([ln_g.astype(F32), ln_b.astype(F32)], axis=1)
    n_main = IN_CHUNKS * CHUNK_COLS
    weights = (_column_chunks(w_in[:, :, :n_main], CHUNK_COLS), _column_chunks(w_in[:, :, n_main:], IN_TAIL),
               _column_chunks(w_out, CHUNK_COLS), _column_chunks(w_ple_gate, CHUNK_COLS),
               _column_chunks(w_ple_proj, CHUNK_COLS))
    sinks = attn_sinks.astype(F32)

    scale = HEAD_DIM ** -0.5
    pos_p = jnp.arange(T, dtype=jnp.int32)
    pos_s = jnp.tile(PAST_LEN + jnp.arange(ST, dtype=jnp.int32), SAMPLE_SEQS)
    tab_p = _rope_tables(pos_p, scale) + _rope_tables(pos_p, 1.0)
    tab_s = _rope_tables(pos_s, scale) + _rope_tables(pos_s, 1.0)
    lvl = jnp.asarray(_level_matrix())
    mlow_p = jnp.asarray(_mask_matrices(BLK), dtype=BF16)
    mlow_s = jnp.asarray(_mask_matrices(ST), dtype=BF16)
    seg = jnp.asarray((np.arange(BLK)[:, None] // ST) == (np.arange(LANES)[None, :] // ST), dtype=BF16)

    ck = cache_k_win.reshape(DEPTH, SB, WINDOW, KV_WIDTH)
    cv = cache_v_win.reshape(DEPTH, SB, WINDOW, KV_WIDTH)
    ys, ks, vs, ss = _sample_call(x_sample, p_sample, ck, cv, state_hgrn, tab_s, lvl, mlow_s, seg, weights,
                                  sinks.reshape(DEPTH * ATTN_HEADS), v512, v1024)

    yp = x_prompt
    kp_l, vp_l, sp_l = [], [], []
    for i in range(DEPTH):
        yp, kp, vp, sp = _prompt_layer(i, yp, p_prompt, tab_p, lvl, mlow_p, weights, sinks[i], v512, v1024)
        kp_l.append(kp); vp_l.append(vp); sp_l.append(sp)

    def kv5(a, b):
        return a.reshape(DEPTH, b, WINDOW, KV_HEADS, HEAD_DIM)

    return (yp, ys, kv5(jnp.stack(kp_l), B), kv5(jnp.stack(vp_l), B), jnp.stack(sp_l),
            kv5(ks, SB), kv5(vs, SB), ss)
```

```python
import math

import numpy as np
import jax
import jax.numpy as jnp
from jax import lax
from jax.experimental import pallas as pl
from jax.experimental.pallas import tpu as pltpu

D_MODEL = 1024
DEPTH = 4
PAST_LEN = 8192
ATTN_HEADS = 8
KV_HEADS = 2
HEAD_DIM = 64
ATTN_WIDTH = ATTN_HEADS * HEAD_DIM
KV_WIDTH = KV_HEADS * HEAD_DIM
WINDOW = 128
ROPE_THETA = 10000.0
HG_HEADS = 4
HG_DK = 128
HG_DV = 128
HG_KW = HG_HEADS * HG_DK
HG_VW = HG_HEADS * HG_DV
MIX_WIDTH = ATTN_WIDTH + HG_VW
IN_COLS = 2 * ATTN_WIDTH + 2 * KV_WIDTH + 2 * HG_KW + 2 * HG_VW
PLE_DIM = 256
DN_ALPHA = (2 * DEPTH) ** 0.25
NORM_EPS = 1e-5
NEG_INF = -1e30
LOG2E = 1.4426950408889634

C_Q = 0
C_K = C_Q + ATTN_WIDTH
C_V = C_K + KV_WIDTH
C_GA = C_V + KV_WIDTH
C_HQ = C_GA + ATTN_WIDTH
C_HF = C_HQ + HG_KW
C_HI = C_HF + HG_KW
C_GH = C_HI + HG_VW

HALF_DIM = HEAD_DIM // 2
DEC_TOKENS = 8
TOK_BITS = 3
QM_ROWS = ATTN_HEADS * DEC_TOKENS
QM_BITS = 6

LANES = 128
MXU_COLS = 256
CHUNK_COLS = 2 * MXU_COLS
BLK = 128
PROMPT_TILE = 512
PROMPT_SEQS = 1
IN_GROUP = 2
SAMPLE_SEQS = 16
LOW_LEVELS = 3
PROMPT_LEVELS = 7
VMEM_LIMIT = 56 * 1024 * 1024
IN_CHUNKS = IN_COLS // CHUNK_COLS
IN_TAIL = IN_COLS - IN_CHUNKS * CHUNK_COLS
OUT_CHUNKS = D_MODEL // CHUNK_COLS
N_WEIGHT_REFS = IN_CHUNKS + 1 + 3 * OUT_CHUNKS

F32 = jnp.float32
BF16 = jnp.bfloat16


def _dot(a, b):
    return jnp.dot(a, b, preferred_element_type=F32)


def _dot_nt(a, b):
    return lax.dot_general(a, b, (((1,), (1,)), ((), ())), preferred_element_type=F32)


def _dot_tn(a, b):
    return lax.dot_general(a, b, (((0,), (0,)), ((), ())), preferred_element_type=F32)


def _split_bf16(x):
    hi = x.astype(BF16)
    lo = (x - hi.astype(F32)).astype(BF16)
    return hi, lo


def _sigmoid(x):
    return 0.5 * jnp.tanh(0.5 * x) + 0.5


def _silu(x):
    h = 0.5 * x
    return h * jnp.tanh(h) + h


def _lane_iota(rows):
    return lax.broadcasted_iota(jnp.int32, (rows, LANES), 1)


def _rope(x, cos_t, sin_t):
    lane = _lane_iota(x.shape[0])
    first_half = (lane & HALF_DIM) == 0
    swapped = jnp.where(first_half, pltpu.roll(x, LANES - HALF_DIM, 1), pltpu.roll(x, HALF_DIM, 1))
    return x * cos_t + swapped * sin_t


def _kv_variants(a, fill):
    lane = _lane_iota(a.shape[0])
    lo = lane < HEAD_DIM
    sw = pltpu.roll(a, HEAD_DIM, 1)
    f = jnp.full_like(a, fill)
    out = [jnp.where(lo, a, f), jnp.where(lo, f, sw), jnp.where(lo, sw, f), jnp.where(lo, f, a)]
    return [o.astype(BF16) for o in out]


def _sum_half(e, es, rows):
    lane = _lane_iota(rows)
    sum_half = (lane >= HEAD_DIM) if e == 0 else (lane < HEAD_DIM)
    return jnp.where(sum_half, es, 0.0)


def _assemble_attn(res, rows):
    lane = _lane_iota(rows)
    lo = lane < HEAD_DIM
    cols = []
    for c in range(4):
        g, cc = c // 2, c % 2
        r0 = res[g * 2 + 0][cc * rows:(cc + 1) * rows]
        r1 = res[g * 2 + 1][cc * rows:(cc + 1) * rows]
        num = jnp.where(lo, r0, r1)
        den = pltpu.roll(jnp.where(lo, r1, r0), HEAD_DIM, 1)
        cols.append(num / den)
    return cols


def _log_decay(hf, log_lb, log1m_lb):
    ls = jnp.minimum(hf, 0.0) - jnp.log(1.0 + jnp.exp(-jnp.abs(hf)))
    b = log1m_lb + ls
    return jnp.maximum(log_lb, b) + jnp.log(1.0 + jnp.exp(-jnp.abs(log_lb - b)))


def _level_operand(l, q_h, k_h, e_low, g_h):
    if l < LOW_LEVELS:
        rows = lax.broadcasted_iota(jnp.int32, (BLK, LANES), 0)
        upper = ((rows >> l) & 1) == 1
        return jnp.where(upper, q_h, k_h) * jnp.exp2(e_low[l])
    b, h = 2 << l, 1 << l
    pieces = []
    for i in range(BLK // b):
        lower, upper = slice(i * b, i * b + h), slice(i * b + h, (i + 1) * b)
        g_mid = g_h[i * b + h - 1:i * b + h, :]
        pieces.append(k_h[lower, :] * jnp.exp2(g_mid - g_h[lower, :]))
        pieces.append(q_h[upper, :] * jnp.exp2(g_h[upper, :] - g_mid))
    return jnp.concatenate(pieces, axis=0)


def _hgrn_scores(q_h, k_h, e_low, g_h, lvl, levels):
    a = jnp.zeros((BLK, BLK), F32)
    for l in range(levels):
        u = _level_operand(l, q_h, k_h, e_low, g_h).astype(BF16)
        a = jnp.where(lvl == l, _dot_nt(u, u), a)
    diag = jnp.sum(q_h * k_h, axis=-1, keepdims=True)
    return jnp.where(lvl == -1, diag, a)


def _gated_rmsnorm_cols(cols, gain_row, gate, width):
    ss = None
    for c in cols:
        s = jnp.sum(c * c, axis=-1, keepdims=True)
        ss = s if ss is None else ss + s
    inv = lax.rsqrt(ss * (1.0 / width) + NORM_EPS)
    out = []
    for i, c in enumerate(cols):
        gt = gate[:, i * LANES:(i + 1) * LANES]
        out.append(c * inv * gain_row[:, i * LANES:(i + 1) * LANES] * _silu(gt))
    return out


def _layernorm(hpre, ln_g, ln_b):
    mu = jnp.mean(hpre, axis=-1, keepdims=True)
    cen = hpre - mu
    var = jnp.mean(cen * cen, axis=-1, keepdims=True)
    return cen * lax.rsqrt(var + NORM_EPS) * ln_g + ln_b


def _chunked_dot(a, w_refs):
    return jnp.concatenate([_dot(a, w[...]) for w in w_refs], axis=1)


def _split_refs(refs):
    n = IN_CHUNKS + 1
    return (refs[:IN_CHUNKS], refs[IN_CHUNKS], refs[n:n + OUT_CHUNKS], refs[n + OUT_CHUNKS:n + 2 * OUT_CHUNKS],
            refs[n + 2 * OUT_CHUNKS:N_WEIGHT_REFS], refs[N_WEIGHT_REFS:])


def _merge(a, b):
    out, ia, ib = [], 0, 0
    while ia < len(a) or ib < len(b):
        if ib >= len(b) or (ia < len(a) and ia * len(b) <= ib * len(a)):
            out.append(a[ia])
            ia += 1
        else:
            out.append(b[ib])
            ib += 1
    return out


def _interleave(units, thunks):
    thunks = list(thunks)
    total, done = len(thunks), 0
    for seen, fn in enumerate(units, 1):
        fn()
        want = min(total, -(-seen * total // len(units)))
        while done < want:
            thunks[done]()
            done += 1


def _prompt_kernel(sinks_ref, x_ref, xn_ref, p_ref, cq_ref, sq_ref, ck_ref, sk_ref, lvl_ref, mlow_ref, *refs):
    w_in, w_tail, w_out, w_pg, w_pp, rest = _split_refs(refs)
    (v512_ref, v1024_ref, y_ref, kk_ref, vk_ref, sfin_ref,
     st_scr, kprev_scr, vprev_scr, z_scr, mix_scr) = rest
    t = pl.program_id(1)
    nblk = PROMPT_TILE // BLK

    @pl.when(t == 0)
    def _():
        st_scr[...] = jnp.zeros_like(st_scr)
        kprev_scr[...] = jnp.zeros_like(kprev_scr)
        vprev_scr[...] = jnp.zeros_like(vprev_scr)

    lvl = lvl_ref[...]
    attn_g = v512_ref[0:1, :]
    log_lb = v512_ref[1:2, :]
    log1m_lb = v512_ref[2:3, :]
    hg_g = v512_ref[3:4, :]

    row = lax.broadcasted_iota(jnp.int32, (2 * BLK, BLK), 0)
    col = lax.broadcasted_iota(jnp.int32, (2 * BLK, BLK), 1)
    from_prev = col > (row & (BLK - 1))
    first_rows = lax.broadcasted_iota(jnp.int32, (2 * BLK, 1), 0) < BLK
    carry = [{} for _ in range(PROMPT_SEQS)]

    ngrp = nblk // IN_GROUP
    grp_rows = IN_GROUP * BLK

    def stage_in(q, g):
        cell = {}
        first = (g % ngrp) * IN_GROUP

        def chunk(c):
            def run():
                if "xb" not in cell:
                    src = xn_ref[q] if g == ngrp else x_ref[q, g * grp_rows:(g + 1) * grp_rows, :]
                    cell["xb"] = src.astype(BF16)
                if c < IN_CHUNKS:
                    cols, zz = slice(c * CHUNK_COLS, (c + 1) * CHUNK_COLS), _dot(cell["xb"], w_in[c][...])
                else:
                    cols, zz = slice(IN_CHUNKS * CHUNK_COLS, IN_COLS), _dot(cell["xb"], w_tail[...])
                for i in range(IN_GROUP):
                    z_scr[q, first + i, :, cols] = zz[i * BLK:(i + 1) * BLK, :]
            return run
        return [chunk(c) for c in range(IN_CHUNKS + 1)]

    def stage_out(q, j):
        slot, rows_j, cell = j % 2, slice(j * BLK, (j + 1) * BLK), {"m": [], "g": []}

        def mix_chunk(c):
            def run():
                cell["m"].append(_dot(mix_scr[q, slot], w_out[c][...]))
                if c + 1 == OUT_CHUNKS:
                    hpre = DN_ALPHA * x_ref[q, rows_j, :] + jnp.concatenate(cell.pop("m"), axis=1)
                    h = _layernorm(hpre, v1024_ref[0:1, :], v1024_ref[1:2, :])
                    cell["h"] = h
                    cell["hb"] = h.astype(BF16)
            return run

        def gate_chunk(c):
            def run():
                cell["g"].append(_sigmoid(_dot(cell["hb"], w_pg[c][...])))
            return run

        def finish():
            pb = p_ref[q, rows_j, :].astype(BF16)
            for c in range(OUT_CHUNKS):
                cs = slice(c * CHUNK_COLS, (c + 1) * CHUNK_COLS)
                y_ref[q, rows_j, cs] = cell["h"][:, cs] + cell["g"][c] * _dot(pb, w_pp[c][...])
        return ([mix_chunk(c) for c in range(OUT_CHUNKS)] + [gate_chunk(c) for c in range(OUT_CHUNKS)]
                + [finish])

    def mix_units(q, j):
        slot, rows_j, s = j % 2, slice(j * BLK, (j + 1) * BLK), {}

        def zc(c0, width):
            return z_scr[q, j, :, c0:c0 + width]

        def a_rope():
            qcols = [_rope(zc(C_Q + c * LANES, LANES), cq_ref[rows_j, :], sq_ref[rows_j, :]).astype(BF16)
                     for c in range(4)]
            k_rot = _rope(zc(C_K, KV_WIDTH), ck_ref[rows_j, :], sk_ref[rows_j, :])
            v_new = zc(C_V, KV_WIDTH)
            if j == 0:
                s["k_prev"] = [kprev_scr[q, i] for i in range(4)]
                s["v_prev"] = [vprev_scr[q, i] for i in range(4)]
            else:
                s["k_prev"], s["v_prev"] = carry[q]["k_var"], carry[q]["v_var"]
            k_var = _kv_variants(k_rot, 0.0)
            v_var = _kv_variants(v_new, 1.0)
            carry[q]["k_var"], carry[q]["v_var"] = k_var, v_var
            s["k_var"], s["v_var"] = k_var, v_var
            if j == nblk - 1:
                for i in range(4):
                    kprev_scr[q, i] = k_var[i]
                    vprev_scr[q, i] = v_var[i]
                kk_ref[q] = k_rot
                vk_ref[q] = v_new
            s["qst"] = [jnp.concatenate([qcols[2 * g], qcols[2 * g + 1]], axis=0) for g in range(2)]
            s["scores"], s["probs"], s["res"] = {}, {}, {}

        def a_scores(i):
            def run():
                s_prev = _dot_nt(s["qst"][i // 2], s["k_prev"][i])
                if j == 0:
                    s_prev = jnp.where(t > 0, s_prev, NEG_INF)
                s["scores"][i] = jnp.where(from_prev, s_prev, _dot_nt(s["qst"][i // 2], s["k_var"][i]))
            return run

        def a_softmax(i):
            def run():
                g, e = i // 2, i % 2
                sc = s["scores"].pop(i)
                sink_col = jnp.where(first_rows, sinks_ref[4 * g + e], sinks_ref[4 * g + 2 + e])
                m = jnp.maximum(jnp.max(sc, axis=-1, keepdims=True), sink_col)
                p = jnp.exp(sc - m)
                p_both = jnp.concatenate([jnp.where(from_prev, p, 0.0), jnp.where(from_prev, 0.0, p)], axis=1)
                s["probs"][i] = (p_both.astype(BF16), jnp.exp(sink_col - m))
            return run

        def a_values(i):
            def run():
                p_both, es = s["probs"].pop(i)
                r = _dot(p_both, jnp.concatenate([s["v_prev"][i], s["v_var"][i]], axis=0))
                s["res"][i] = r + _sum_half(i % 2, es, 2 * BLK)
            return run

        def a_finish():
            acols = _assemble_attn([s["res"][i] for i in range(4)], BLK)
            acols = _gated_rmsnorm_cols(acols, attn_g, zc(C_GA, ATTN_WIDTH), ATTN_WIDTH)
            for c in range(4):
                mix_scr[q, slot, :, c * LANES:(c + 1) * LANES] = acols[c].astype(BF16)

        def h_prep():
            lf = _log_decay(zc(C_HF, HG_KW), log_lb, log1m_lb) * LOG2E
            kin = 1.0 - jnp.exp2(lf)
            lf_hi, lf_lo = _split_bf16(lf)
            m_low = mlow_ref[0:LOW_LEVELS * BLK, :]
            m_cum = mlow_ref[LOW_LEVELS * BLK:(LOW_LEVELS + 1) * BLK, :]
            eg = jnp.concatenate([_dot(m_low, lf_hi), _dot(m_cum, lf_hi) + _dot(m_cum, lf_lo)], axis=0)
            s["heads"] = []
            for h in range(HG_HEADS):
                hs = slice(h * LANES, (h + 1) * LANES)
                s["heads"].append(dict(
                    q=zc(C_HQ + h * LANES, LANES), k=kin[:, hs],
                    e_low=[eg[l * BLK:(l + 1) * BLK, hs] for l in range(LOW_LEVELS)],
                    g=eg[LOW_LEVELS * BLK:(LOW_LEVELS + 1) * BLK, hs], u=[], p=[]))

        def h_operand(h, l):
            def run():
                hd = s["heads"][h]
                hd["u"].append(_level_operand(l, hd["q"], hd["k"], hd["e_low"], hd["g"]).astype(BF16))
            return run

        def h_products(h):
            def run():
                hd = s["heads"][h]
                hd["p"] = [_dot_nt(u, u) for u in hd.pop("u")]
            return run

        def h_combine(h):
            def run():
                hd = s["heads"][h]
                a = jnp.zeros((BLK, BLK), F32)
                for l, p in enumerate(hd.pop("p")):
                    a = jnp.where(lvl == l, p, a)
                diag = jnp.sum(hd["q"] * hd["k"], axis=-1, keepdims=True)
                hd["a"] = jnp.where(lvl == -1, diag, a).astype(BF16)
                g_last = hd["g"][BLK - 1:BLK, :]
                hd["qe"] = (hd["q"] * jnp.exp2(hd["g"])).astype(BF16)
                hd["kd"] = (hd["k"] * jnp.exp2(g_last - hd["g"])).astype(BF16)
                hd["dec"] = jnp.exp2(g_last)
            return run

        def h_output(h):
            def run():
                hd = s["heads"][h]
                v_h = zc(C_HI + h * LANES, LANES).astype(BF16)
                st = st_scr[q, h]
                hd["o"] = _dot(hd.pop("a"), v_h) + _dot_nt(hd.pop("qe"), st.astype(BF16))
                st_scr[q, h] = st * hd.pop("dec") + _dot_tn(v_h, hd.pop("kd"))
            return run

        def h_finish(h):
            def run():
                o_h = s["heads"][h].pop("o")
                gt = zc(C_GH + h * LANES, LANES)
                inv = lax.rsqrt(jnp.mean(o_h * o_h, axis=-1, keepdims=True) + NORM_EPS)
                mix_scr[q, slot, :, ATTN_WIDTH + h * LANES:ATTN_WIDTH + (h + 1) * LANES] = (
                    o_h * inv * hg_g[:, h * LANES:(h + 1) * LANES] * _silu(gt)).astype(BF16)
            return run

        heads, variants = range(HG_HEADS), range(4)
        return ([a_rope, h_prep]
                + _merge([a_scores(i) for i in variants],
                         [h_operand(h, l) for h in heads for l in range(PROMPT_LEVELS)])
                + _merge([a_softmax(i) for i in variants], [h_products(h) for h in heads])
                + _merge([a_values(i) for i in variants], [h_combine(h) for h in heads])
                + _merge([a_finish], [h_output(h) for h in heads])
                + [h_finish(h) for h in heads])

    def stage_mix(j):
        units = mix_units(0, j)
        for q in range(1, PROMPT_SEQS):
            units = _merge(units, mix_units(q, j))
        return units

    @pl.when(jnp.logical_and(pl.program_id(0) == 0, t == 0))
    def _():
        for q in range(PROMPT_SEQS):
            for th in stage_in(q, 0):
                th()

    def both(stage, j):
        lists = [stage(q, j) for q in range(PROMPT_SEQS)]
        return [th for group in zip(*lists) for th in group]

    for j in range(nblk):
        nxt = both(stage_in, j // IN_GROUP + 1)
        part = j % IN_GROUP
        per = -(-len(nxt) // IN_GROUP)
        side = nxt[part * per:(part + 1) * per]
        if j >= 1:
            side = side + both(stage_out, j - 1)
        _interleave(stage_mix(j), side)
    for th in both(stage_out, nblk - 1):
        th()

    @pl.when(t == pl.num_programs(1) - 1)
    def _():
        for q in range(PROMPT_SEQS):
            for h in range(HG_HEADS):
                sfin_ref[q, h] = st_scr[q, h].T


def _sample_kernel(sinks_ref, x_ref, p_ref, ck_ref, cv_ref, s0_ref, cq_ref, sq_ref, ckt_ref, skt_ref,
                   lvl_ref, mlow_ref, seg_ref, *refs):
    w_in, w_tail, w_out, w_pg, w_pp, rest = _split_refs(refs)
    v512_ref, v1024_ref, y_ref, nk_ref, nv_ref, ns_ref, y_scr, qm_scr, attn_scr, mix_scr = rest
    layer = pl.program_id(0)
    blk = pl.program_id(1)
    tk = DEC_TOKENS
    rows = SAMPLE_SEQS * tk
    yrows = pl.ds(pl.multiple_of(blk * rows, rows), rows)

    @pl.when(layer == 0)
    def _():
        y_scr[yrows, :] = x_ref[...].reshape(rows, D_MODEL)

    x = y_scr[yrows, :]
    xb = x.astype(BF16)
    z = jnp.concatenate([_chunked_dot(xb, w_in), _dot(xb, w_tail[...])], axis=1)
    lvl = lvl_ref[...]
    attn_g = v512_ref[0:1, :]
    log_lb = v512_ref[1:2, :]
    log1m_lb = v512_ref[2:3, :]
    hg_g = v512_ref[3:4, :]

    qcols = [_rope(z[:, C_Q + c * LANES:C_Q + (c + 1) * LANES], cq_ref[...], sq_ref[...]) for c in range(4)]
    k_rot = _rope(z[:, C_K:C_K + KV_WIDTH], ckt_ref[...], skt_ref[...])
    v_new = z[:, C_V:C_V + KV_WIDTH]
    k_new_b = k_rot.astype(BF16)
    v_new_b = v_new.astype(BF16)
    lo2 = _lane_iota(2 * rows) < HEAD_DIM
    for g in range(2):
        xg = jnp.concatenate([qcols[2 * g], qcols[2 * g + 1]], axis=0)
        xsw = pltpu.roll(xg, HEAD_DIM, 1)
        keep = lo2 if g == 0 else jnp.logical_not(lo2)
        for e in range(2):
            qm = jnp.where(keep, xg if e == g else xsw, 0.0)
            for cc in range(2):
                r0 = ((g * 2 + e) * 2 + cc) * tk
                for s in range(SAMPLE_SEQS):
                    qm_scr[s, r0:r0 + tk, :] = qm[cc * rows + s * tk:cc * rows + (s + 1) * tk, :]

    lf = _log_decay(z[:, C_HF:C_HF + HG_KW], log_lb, log1m_lb) * LOG2E
    kin = 1.0 - jnp.exp2(lf)
    lf_hi, lf_lo = _split_bf16(lf)
    eg = _dot(mlow_ref[...], lf_hi) + _dot(mlow_ref[...], lf_lo)
    dec_t = jnp.exp2(_dot_tn(lf_hi, seg_ref[...]) + _dot_tn(lf_lo, seg_ref[...]))
    o_intra, qe, kd, vh = [], [], [], []
    for h in range(HG_HEADS):
        hs = slice(h * LANES, (h + 1) * LANES)
        q_h = z[:, C_HQ + h * LANES:C_HQ + (h + 1) * LANES]
        k_h = kin[:, hs]
        v_h = z[:, C_HI + h * LANES:C_HI + (h + 1) * LANES]
        e_low = [eg[l * rows:(l + 1) * rows, hs] for l in range(LOW_LEVELS)]
        g_h = eg[LOW_LEVELS * rows:(LOW_LEVELS + 1) * rows, hs]
        tot_h = eg[(LOW_LEVELS + 1) * rows:(LOW_LEVELS + 2) * rows, hs]
        a = _hgrn_scores(q_h, k_h, e_low, g_h, lvl, LOW_LEVELS)
        o_intra.append(_dot(a.astype(BF16), v_h.astype(BF16)))
        qe.append(q_h * jnp.exp2(g_h))
        kd.append(k_h * jnp.exp2(tot_h - g_h))
        vh.append(v_h)

    grp = 4
    grows = grp * QM_ROWS
    rid = lax.broadcasted_iota(jnp.int32, (grows, 1), 0)
    r8 = (rid >> TOK_BITS) & (ATTN_HEADS - 1)
    sink_col = jnp.zeros((grows, 1), F32)
    for idx in range(ATTN_HEADS):
        g, e, cc = idx // 4, (idx // 2) % 2, idx % 2
        sink_col = jnp.where(r8 == idx, sinks_ref[layer * ATTN_HEADS + 4 * g + 2 * cc + e], sink_col)
    row = lax.broadcasted_iota(jnp.int32, (grows, 2 * WINDOW), 0)
    col = lax.broadcasted_iota(jnp.int32, (grows, 2 * WINDOW), 1)
    tok = row & (tk - 1)
    new_col = col - WINDOW
    mask_cache = jnp.logical_and(col < WINDOW, col > tok)
    lo8 = _lane_iota(tk) < HEAD_DIM
    o_inter = [[None] * SAMPLE_SEQS for _ in range(HG_HEADS)]
    for gi in range(SAMPLE_SEQS // grp):
        seqs = range(gi * grp, (gi + 1) * grp)
        kcs = {s: ck_ref[s] for s in seqs}
        vcs = {s: cv_ref[s] for s in seqs}
        sc = [_dot_nt(qm_scr[s].astype(BF16),
                      jnp.concatenate([kcs[s].astype(BF16), k_new_b], axis=0)) for s in seqs]
        for s in seqs:
            nk_ref[s, 0:WINDOW - tk, :] = kcs[s][tk:WINDOW, :]
            nk_ref[s, WINDOW - tk:WINDOW, :] = k_rot[s * tk:(s + 1) * tk, :]
            nv_ref[s, 0:WINDOW - tk, :] = vcs[s][tk:WINDOW, :]
            nv_ref[s, WINDOW - tk:WINDOW, :] = v_new[s * tk:(s + 1) * tk, :]
        same_seq = (new_col >> TOK_BITS) == (row >> QM_BITS) + gi * grp
        mask_new = jnp.logical_and(jnp.logical_and(col >= WINDOW, same_seq), (new_col & (tk - 1)) <= tok)
        s_all = jnp.where(jnp.logical_or(mask_cache, mask_new), jnp.concatenate(sc, axis=0), NEG_INF)
        m = jnp.maximum(jnp.max(s_all, axis=-1, keepdims=True), sink_col)
        p = jnp.exp(s_all - m)
        den = jnp.sum(p, axis=-1, keepdims=True) + jnp.exp(sink_col - m)
        pb = p.astype(BF16)
        o_all = jnp.concatenate(
            [_dot(pb[i * QM_ROWS:(i + 1) * QM_ROWS, :],
                  jnp.concatenate([vcs[s].astype(BF16), v_new_b], axis=0))
             for i, s in enumerate(seqs)], axis=0) / den
        o_sw = pltpu.roll(o_all, HEAD_DIM, 1)
        for i, s in enumerate(seqs):
            for c in range(4):
                g, cc = c // 2, c % 2
                ra = i * QM_ROWS + ((g * 2 + 0) * 2 + cc) * tk
                rb = i * QM_ROWS + ((g * 2 + 1) * 2 + cc) * tk
                part0 = (o_all if g == 0 else o_sw)[ra:ra + tk, :]
                part1 = (o_all if g == 1 else o_sw)[rb:rb + tk, :]
                attn_scr[c, s * tk:(s + 1) * tk, :] = jnp.where(lo8, part0, part1)
        s0s = {(s, h): s0_ref[s, h] for s in seqs for h in range(HG_HEADS)}
        for s in seqs:
            for h in range(HG_HEADS):
                o_inter[h][s] = _dot(qe[h][s * tk:(s + 1) * tk, :].astype(BF16), s0s[(s, h)].astype(BF16))
        upd = {(s, h): _dot_tn(kd[h][s * tk:(s + 1) * tk, :].astype(BF16), vh[h][s * tk:(s + 1) * tk, :].astype(BF16))
               for s in seqs for h in range(HG_HEADS)}
        for s in seqs:
            for h in range(HG_HEADS):
                dcol = dec_t[h * LANES:(h + 1) * LANES, s * tk:s * tk + 1]
                ns_ref[s, h] = dcol * s0s[(s, h)] + upd[(s, h)]

    acols = _gated_rmsnorm_cols([attn_scr[c] for c in range(4)], attn_g, z[:, C_GA:C_GA + ATTN_WIDTH],
                                ATTN_WIDTH)
    for c in range(4):
        mix_scr[:, c * LANES:(c + 1) * LANES] = acols[c].astype(BF16)
    for h in range(HG_HEADS):
        hs = slice(h * LANES, (h + 1) * LANES)
        o_h = o_intra[h] + jnp.concatenate(o_inter[h], axis=0)
        gt = z[:, C_GH + h * LANES:C_GH + (h + 1) * LANES]
        inv = lax.rsqrt(jnp.mean(o_h * o_h, axis=-1, keepdims=True) + NORM_EPS)
        mix_scr[:, ATTN_WIDTH + h * LANES:ATTN_WIDTH + (h + 1) * LANES] = (
            o_h * inv * hg_g[:, hs] * _silu(gt)).astype(BF16)

    hpre = DN_ALPHA * x + _chunked_dot(mix_scr[...], w_out)
    hn = _layernorm(hpre, v1024_ref[0:1, :], v1024_ref[1:2, :])
    gate = _sigmoid(_chunked_dot(hn.astype(BF16), w_pg))
    pp = _chunked_dot(p_ref[...].reshape(rows, PLE_DIM).astype(BF16), w_pp)
    y = hn + gate * pp
    y_scr[yrows, :] = y
    y_ref[...] = y.reshape(SAMPLE_SEQS, tk, D_MODEL)


def _level_matrix():
    t = np.arange(BLK)[:, None]
    s = np.arange(BLK)[None, :]
    x = t ^ s
    lv = np.floor(np.log2(np.maximum(x, 1))).astype(np.int32)
    return np.where(t > s, lv, np.where(t == s, -1, -2)).astype(np.int32)


def _level_exponent_matrix(l):
    m = np.zeros((BLK, BLK), np.float32)
    b, h = 2 << l, 1 << l
    for t in range(BLK):
        mid = t - t % b + h
        if t >= mid:
            m[t, mid:t + 1] = 1.0
        else:
            m[t, t + 1:mid] = 1.0
    return m


def _mask_matrices(seq_rows):
    blocks = [_level_exponent_matrix(l) for l in range(LOW_LEVELS)]
    t = np.arange(BLK)[:, None]
    s = np.arange(BLK)[None, :]
    same = (t // seq_rows) == (s // seq_rows)
    blocks.append((same & (s <= t)).astype(np.float32))
    if seq_rows < BLK:
        blocks.append(same.astype(np.float32))
    return np.concatenate(blocks, axis=0)


def _rope_tables(pos, scale):
    half = HEAD_DIM // 2
    inv = jnp.exp(-math.log(ROPE_THETA) * jnp.arange(half, dtype=F32) * 2.0 / HEAD_DIM)
    ang = pos.astype(F32)[:, None] * inv[None, :]
    cos = jnp.cos(ang) * scale
    sin = jnp.sin(ang) * scale
    return jnp.tile(cos, (1, 4)), jnp.concatenate([-sin, sin, -sin, sin], axis=1)


def _weight_specs(layer_of):
    def window(rows, width, c):
        return pl.BlockSpec((None, rows, width), lambda *g, c=c: (layer_of(*g), 0, c),
                            pipeline_mode=pl.Buffered(1))
    specs = [window(D_MODEL, CHUNK_COLS, c) for c in range(IN_CHUNKS)]
    specs.append(window(D_MODEL, IN_TAIL, IN_CHUNKS * CHUNK_COLS // IN_TAIL))
    for rows in (MIX_WIDTH, D_MODEL, PLE_DIM):
        specs += [window(rows, CHUNK_COLS, c) for c in range(OUT_CHUNKS)]
    return specs


def _weight_args(weights):
    w_in, w_out, w_pg, w_pp = weights
    return [w_in] * (IN_CHUNKS + 1) + [w_out] * OUT_CHUNKS + [w_pg] * OUT_CHUNKS + [w_pp] * OUT_CHUNKS


def _const_spec(shape, layer=None):
    if layer is None:
        return pl.BlockSpec(shape, lambda *_: (0,) * len(shape))
    return pl.BlockSpec((None,) + shape, lambda *_: (layer,) + (0,) * len(shape))


def _prompt_layer(i, x, p_all, tables, lvl, mlow, weights, sinks, v512, v1024):
    B, T, _ = x.shape
    nt = T // PROMPT_TILE
    nb = B // PROMPT_SEQS
    tab_spec = pl.BlockSpec((PROMPT_TILE, LANES), lambda b, t: (t, 0))
    nblk = PROMPT_TILE // BLK
    assert nblk % IN_GROUP == 0 and nblk % 2 == 0

    def next_first_group(b, t):
        flat = jnp.minimum(b * nt + t + 1, nb * nt - 1)
        return (flat // nt, (flat % nt) * (nblk // IN_GROUP), 0)
    in_specs = [
        pl.BlockSpec(memory_space=pltpu.SMEM),
        pl.BlockSpec((PROMPT_SEQS, PROMPT_TILE, D_MODEL), lambda b, t: (b, t, 0)),
        pl.BlockSpec((PROMPT_SEQS, IN_GROUP * BLK, D_MODEL), next_first_group),
        pl.BlockSpec((None, PROMPT_SEQS, PROMPT_TILE, PLE_DIM), lambda b, t: (i, b, t, 0)),
        tab_spec, tab_spec, tab_spec, tab_spec,
        _const_spec((BLK, BLK)),
        _const_spec(mlow.shape),
    ] + _weight_specs(lambda b, t: i) + [
        _const_spec((4, ATTN_WIDTH), i),
        _const_spec((2, D_MODEL), i),
    ]
    out_shape = [
        jax.ShapeDtypeStruct((B, T, D_MODEL), F32),
        jax.ShapeDtypeStruct((B, WINDOW, KV_WIDTH), F32),
        jax.ShapeDtypeStruct((B, WINDOW, KV_WIDTH), F32),
        jax.ShapeDtypeStruct((B, HG_HEADS, HG_DK, HG_DV), F32),
    ]
    out_specs = [
        pl.BlockSpec((PROMPT_SEQS, PROMPT_TILE, D_MODEL), lambda b, t: (b, t, 0)),
        pl.BlockSpec((PROMPT_SEQS, WINDOW, KV_WIDTH), lambda b, t: (b, 0, 0)),
        pl.BlockSpec((PROMPT_SEQS, WINDOW, KV_WIDTH), lambda b, t: (b, 0, 0)),
        pl.BlockSpec((PROMPT_SEQS, HG_HEADS, HG_DK, HG_DV), lambda b, t: (b, 0, 0, 0)),
    ]
    scratch = [
        pltpu.VMEM((PROMPT_SEQS, HG_HEADS, HG_DV, HG_DK), F32),
        pltpu.VMEM((PROMPT_SEQS, 4, BLK, LANES), BF16),
        pltpu.VMEM((PROMPT_SEQS, 4, BLK, LANES), BF16),
        pltpu.VMEM((PROMPT_SEQS, nblk, BLK, IN_COLS), F32),
        pltpu.VMEM((PROMPT_SEQS, 2, BLK, MIX_WIDTH), BF16),
    ]
    return pl.pallas_call(
        _prompt_kernel,
        grid=(nb, nt),
        in_specs=in_specs,
        out_specs=out_specs,
        out_shape=out_shape,
        scratch_shapes=scratch,
        compiler_params=pltpu.CompilerParams(
            dimension_semantics=("arbitrary", "arbitrary"), vmem_limit_bytes=VMEM_LIMIT),
        name=f"prompt_layer{i}",
    )(sinks, x, x, p_all, *tables, lvl, mlow, *_weight_args(weights), v512, v1024)


def _sample_call(x, p_all, ck_all, cv_all, s0_all, tables, lvl, mlow, seg, weights, sinks, v512, v1024):
    B, T, _ = x.shape
    rows = SAMPLE_SEQS * T
    tab_spec = _const_spec((rows, LANES))
    in_specs = [
        pl.BlockSpec(memory_space=pltpu.SMEM),
        pl.BlockSpec((SAMPLE_SEQS, T, D_MODEL), lambda l, b: (jnp.where(l == 0, b, 0), 0, 0)),
        pl.BlockSpec((None, SAMPLE_SEQS, T, PLE_DIM), lambda l, b: (l, b, 0, 0)),
        pl.BlockSpec((None, SAMPLE_SEQS, WINDOW, KV_WIDTH), lambda l, b: (l, b, 0, 0)),
        pl.BlockSpec((None, SAMPLE_SEQS, WINDOW, KV_WIDTH), lambda l, b: (l, b, 0, 0)),
        pl.BlockSpec((None, SAMPLE_SEQS, HG_HEADS, HG_DK, HG_DV), lambda l, b: (l, b, 0, 0, 0)),
        tab_spec, tab_spec, tab_spec, tab_spec,
        _const_spec((BLK, BLK)),
        _const_spec(mlow.shape),
        _const_spec((BLK, LANES)),
    ] + _weight_specs(lambda l, b: l) + [
        pl.BlockSpec((None, 4, ATTN_WIDTH), lambda l, b: (l, 0, 0)),
        pl.BlockSpec((None, 2, D_MODEL), lambda l, b: (l, 0, 0)),
    ]
    out_shape = [
        jax.ShapeDtypeStruct((B, T, D_MODEL), F32),
        jax.ShapeDtypeStruct((DEPTH, B, WINDOW, KV_WIDTH), F32),
        jax.ShapeDtypeStruct((DEPTH, B, WINDOW, KV_WIDTH), F32),
        jax.ShapeDtypeStruct((DEPTH, B, HG_HEADS, HG_DK, HG_DV), F32),
    ]
    out_specs = [
        pl.BlockSpec((SAMPLE_SEQS, T, D_MODEL), lambda l, b: (jnp.where(l == DEPTH - 1, b, 0), 0, 0)),
        pl.BlockSpec((None, SAMPLE_SEQS, WINDOW, KV_WIDTH), lambda l, b: (l, b, 0, 0)),
        pl.BlockSpec((None, SAMPLE_SEQS, WINDOW, KV_WIDTH), lambda l, b: (l, b, 0, 0)),
        pl.BlockSpec((None, SAMPLE_SEQS, HG_HEADS, HG_DK, HG_DV), lambda l, b: (l, b, 0, 0, 0)),
    ]
    scratch = [
        pltpu.VMEM((B * T, D_MODEL), F32),
        pltpu.VMEM((SAMPLE_SEQS, QM_ROWS, LANES), F32),
        pltpu.VMEM((4, rows, LANES), F32),
        pltpu.VMEM((rows, MIX_WIDTH), BF16),
    ]
    return pl.pallas_call(
        _sample_kernel,
        grid=(DEPTH, B // SAMPLE_SEQS),
        in_specs=in_specs,
        out_specs=out_specs,
        out_shape=out_shape,
        scratch_shapes=scratch,
        compiler_params=pltpu.CompilerParams(
            dimension_semantics=("arbitrary", "arbitrary"), vmem_limit_bytes=VMEM_LIMIT),
        name="sample_layers",
    )(sinks, x, p_all, ck_all, cv_all, s0_all, *tables, lvl, mlow, seg, *_weight_args(weights), v512, v1024)


def kernel(x_prompt, x_sample, cache_k_win, cache_v_win, state_hgrn, p_prompt, p_sample, w_in, attn_sinks,
           attn_norm_g, hg_lb_logits, hg_norm_g, w_out, ln_g, ln_b, w_ple_proj, w_ple_gate):
    B, T, _ = x_prompt.shape
    SB, ST, _ = x_sample.shape
    assert T % PROMPT_TILE == 0 and B % PROMPT_SEQS == 0
    assert SB % SAMPLE_SEQS == 0 and ST == DEC_TOKENS and SAMPLE_SEQS * ST == BLK
    assert 1 << TOK_BITS == DEC_TOKENS and 1 << QM_BITS == QM_ROWS
    assert cache_k_win.shape[2] == WINDOW

    cs = jnp.cumsum(jax.nn.softmax(hg_lb_logits.astype(F32), axis=0), axis=0)
    lbs = cs - cs[:1]
    v512 = jnp.stack([attn_norm_g.astype(F32), jnp.log(lbs), jnp.log1p(-lbs), hg_norm_g.astype(F32)], axis=1)
    v1024 = jnp.stack([ln_g.astype(F32), ln_b.astype(F32)], axis=1)
    weights = tuple(w.astype(BF16) for w in (w_in, w_out, w_ple_gate, w_ple_proj))
    sinks = attn_sinks.astype(F32)

    scale = HEAD_DIM ** -0.5
    pos_p = jnp.arange(T, dtype=jnp.int32)
    pos_s = jnp.tile(PAST_LEN + jnp.arange(ST, dtype=jnp.int32), SAMPLE_SEQS)
    tab_p = _rope_tables(pos_p, scale) + _rope_tables(pos_p, 1.0)
    tab_s = _rope_tables(pos_s, scale) + _rope_tables(pos_s, 1.0)
    lvl = jnp.asarray(_level_matrix())
    mlow_p = jnp.asarray(_mask_matrices(BLK), dtype=BF16)
    mlow_s = jnp.asarray(_mask_matrices(ST), dtype=BF16)
    seg = jnp.asarray((np.arange(BLK)[:, None] // ST) == (np.arange(LANES)[None, :] // ST), dtype=BF16)

    ck = cache_k_win.reshape(DEPTH, SB, WINDOW, KV_WIDTH)
    cv = cache_v_win.reshape(DEPTH, SB, WINDOW, KV_WIDTH)
    ys, ks, vs, ss = _sample_call(x_sample, p_sample, ck, cv, state_hgrn, tab_s, lvl, mlow_s, seg, weights,
                                  sinks.reshape(DEPTH * ATTN_HEADS), v512, v1024)

    yp = x_prompt
    kp_l, vp_l, sp_l = [], [], []
    for i in range(DEPTH):
        yp, kp, vp, sp = _prompt_layer(i, yp, p_prompt, tab_p, lvl, mlow_p, weights, sinks[i], v512, v1024)
        kp_l.append(kp); vp_l.append(vp); sp_l.append(sp)

    def kv5(a, b):
        return a.reshape(DEPTH, b, WINDOW, KV_HEADS, HEAD_DIM)

    return (yp, ys, kv5(jnp.stack(kp_l), B), kv5(jnp.stack(vp_l), B), jnp.stack(sp_l),
            kv5(ks, SB), kv5(vs, SB), ss)
```

```python
import math

import numpy as np
import jax
import jax.numpy as jnp
from jax import lax
from jax.experimental import pallas as pl
from jax.experimental.pallas import tpu as pltpu

D_MODEL = 1024
DEPTH = 4
PAST_LEN = 8192
ATTN_HEADS = 8
KV_HEADS = 2
HEAD_DIM = 64
ATTN_WIDTH = ATTN_HEADS * HEAD_DIM
KV_WIDTH = KV_HEADS * HEAD_DIM
WINDOW = 128
ROPE_THETA = 10000.0
HG_HEADS = 4
HG_DK = 128
HG_DV = 128
HG_KW = HG_HEADS * HG_DK
HG_VW = HG_HEADS * HG_DV
MIX_WIDTH = ATTN_WIDTH + HG_VW
IN_COLS = 2 * ATTN_WIDTH + 2 * KV_WIDTH + 2 * HG_KW + 2 * HG_VW
PLE_DIM = 256
DN_ALPHA = (2 * DEPTH) ** 0.25
NORM_EPS = 1e-5
NEG_INF = -1e30
LOG2E = 1.4426950408889634

C_Q = 0
C_K = C_Q + ATTN_WIDTH
C_V = C_K + KV_WIDTH
C_GA = C_V + KV_WIDTH
C_HQ = C_GA + ATTN_WIDTH
C_HF = C_HQ + HG_KW
C_HI = C_HF + HG_KW
C_GH = C_HI + HG_VW

HALF_DIM = HEAD_DIM // 2
DEC_TOKENS = 8
TOK_BITS = 3
QM_ROWS = ATTN_HEADS * DEC_TOKENS
QM_BITS = 6

LANES = 128
MXU_COLS = 256
CHUNK_COLS = 2 * MXU_COLS
BLK = 128
PROMPT_TILE = 512
PROMPT_SEQS = 1
IN_GROUP = 2
SAMPLE_SEQS = 16
LOW_LEVELS = 3
PROMPT_LEVELS = 7
VMEM_LIMIT = 56 * 1024 * 1024
IN_CHUNKS = IN_COLS // CHUNK_COLS
IN_TAIL = IN_COLS - IN_CHUNKS * CHUNK_COLS
OUT_CHUNKS = D_MODEL // CHUNK_COLS
N_WEIGHT_REFS = IN_CHUNKS + 1 + 3 * OUT_CHUNKS

F32 = jnp.float32
BF16 = jnp.bfloat16


def _dot(a, b):
    return jnp.dot(a, b, preferred_element_type=F32)


def _dot_nt(a, b):
    return lax.dot_general(a, b, (((1,), (1,)), ((), ())), preferred_element_type=F32)


def _dot_tn(a, b):
    return lax.dot_general(a, b, (((0,), (0,)), ((), ())), preferred_element_type=F32)


def _split_bf16(x):
    hi = x.astype(BF16)
    lo = (x - hi.astype(F32)).astype(BF16)
    return hi, lo


def _sigmoid(x):
    return 0.5 * jnp.tanh(0.5 * x) + 0.5


def _silu(x):
    h = 0.5 * x
    return h * jnp.tanh(h) + h


def _lane_iota(rows):
    return lax.broadcasted_iota(jnp.int32, (rows, LANES), 1)


def _rope(x, cos_t, sin_t):
    lane = _lane_iota(x.shape[0])
    first_half = (lane & HALF_DIM) == 0
    swapped = jnp.where(first_half, pltpu.roll(x, LANES - HALF_DIM, 1), pltpu.roll(x, HALF_DIM, 1))
    return x * cos_t + swapped * sin_t


def _kv_variants(a, fill):
    lane = _lane_iota(a.shape[0])
    lo = lane < HEAD_DIM
    sw = pltpu.roll(a, HEAD_DIM, 1)
    f = jnp.full_like(a, fill)
    out = [jnp.where(lo, a, f), jnp.where(lo, f, sw), jnp.where(lo, sw, f), jnp.where(lo, f, a)]
    return [o.astype(BF16) for o in out]


def _sum_half(e, es, rows):
    lane = _lane_iota(rows)
    sum_half = (lane >= HEAD_DIM) if e == 0 else (lane < HEAD_DIM)
    return jnp.where(sum_half, es, 0.0)


def _assemble_attn(res, rows):
    lane = _lane_iota(rows)
    lo = lane < HEAD_DIM
    cols = []
    for c in range(4):
        g, cc = c // 2, c % 2
        r0 = res[g * 2 + 0][cc * rows:(cc + 1) * rows]
        r1 = res[g * 2 + 1][cc * rows:(cc + 1) * rows]
        num = jnp.where(lo, r0, r1)
        den = pltpu.roll(jnp.where(lo, r1, r0), HEAD_DIM, 1)
        cols.append(num / den)
    return cols


def _log_decay(hf, log_lb, log1m_lb):
    ls = jnp.minimum(hf, 0.0) - jnp.log(1.0 + jnp.exp(-jnp.abs(hf)))
    b = log1m_lb + ls
    return jnp.maximum(log_lb, b) + jnp.log(1.0 + jnp.exp(-jnp.abs(log_lb - b)))


def _level_operand(l, q_h, k_h, e_low, g_h):
    if l < LOW_LEVELS:
        rows = lax.broadcasted_iota(jnp.int32, (BLK, LANES), 0)
        upper = ((rows >> l) & 1) == 1
        return jnp.where(upper, q_h, k_h) * jnp.exp2(e_low[l])
    b, h = 2 << l, 1 << l
    pieces = []
    for i in range(BLK // b):
        lower, upper = slice(i * b, i * b + h), slice(i * b + h, (i + 1) * b)
        g_mid = g_h[i * b + h - 1:i * b + h, :]
        pieces.append(k_h[lower, :] * jnp.exp2(g_mid - g_h[lower, :]))
        pieces.append(q_h[upper, :] * jnp.exp2(g_h[upper, :] - g_mid))
    return jnp.concatenate(pieces, axis=0)


def _hgrn_scores(q_h, k_h, e_low, g_h, lvl, levels):
    a = jnp.zeros((BLK, BLK), F32)
    for l in range(levels):
        u = _level_operand(l, q_h, k_h, e_low, g_h).astype(BF16)
        a = jnp.where(lvl == l, _dot_nt(u, u), a)
    diag = jnp.sum(q_h * k_h, axis=-1, keepdims=True)
    return jnp.where(lvl == -1, diag, a)


def _gated_rmsnorm_cols(cols, gain_row, gate, width):
    ss = None
    for c in cols:
        s = jnp.sum(c * c, axis=-1, keepdims=True)
        ss = s if ss is None else ss + s
    inv = lax.rsqrt(ss * (1.0 / width) + NORM_EPS)
    out = []
    for i, c in enumerate(cols):
        gt = gate[:, i * LANES:(i + 1) * LANES]
        out.append(c * inv * gain_row[:, i * LANES:(i + 1) * LANES] * _silu(gt))
    return out


def _layernorm(hpre, ln_g, ln_b):
    mu = jnp.mean(hpre, axis=-1, keepdims=True)
    cen = hpre - mu
    var = jnp.mean(cen * cen, axis=-1, keepdims=True)
    return cen * lax.rsqrt(var + NORM_EPS) * ln_g + ln_b


def _chunked_dot(a, w_refs):
    return jnp.concatenate([_dot(a, w[...]) for w in w_refs], axis=1)


def _split_refs(refs):
    n = IN_CHUNKS + 1
    return (refs[:IN_CHUNKS], refs[IN_CHUNKS], refs[n:n + OUT_CHUNKS], refs[n + OUT_CHUNKS:n + 2 * OUT_CHUNKS],
            refs[n + 2 * OUT_CHUNKS:N_WEIGHT_REFS], refs[N_WEIGHT_REFS:])


def _merge(a, b):
    out, ia, ib = [], 0, 0
    while ia < len(a) or ib < len(b):
        if ib >= len(b) or (ia < len(a) and ia * len(b) <= ib * len(a)):
            out.append(a[ia])
            ia += 1
        else:
            out.append(b[ib])
            ib += 1
    return out


def _interleave(units, thunks):
    thunks = list(thunks)
    total, done = len(thunks), 0
    for seen, fn in enumerate(units, 1):
        fn()
        want = min(total, -(-seen * total // len(units)))
        while done < want:
            thunks[done]()
            done += 1


def _prompt_kernel(sinks_ref, x_ref, xn_ref, p_ref, cq_ref, sq_ref, ck_ref, sk_ref, lvl_ref, mlow_ref, *refs):
    w_in, w_tail, w_out, w_pg, w_pp, rest = _split_refs(refs)
    (v512_ref, v1024_ref, y_ref, kk_ref, vk_ref, sfin_ref,
     st_scr, kprev_scr, vprev_scr, z_scr, mix_scr) = rest
    t = pl.program_id(1)
    nblk = PROMPT_TILE // BLK

    @pl.when(t == 0)
    def _():
        st_scr[...] = jnp.zeros_like(st_scr)
        kprev_scr[...] = jnp.zeros_like(kprev_scr)
        vprev_scr[...] = jnp.zeros_like(vprev_scr)

    lvl = lvl_ref[...]
    attn_g = v512_ref[0:1, :]
    log_lb = v512_ref[1:2, :]
    log1m_lb = v512_ref[2:3, :]
    hg_g = v512_ref[3:4, :]

    row = lax.broadcasted_iota(jnp.int32, (2 * BLK, BLK), 0)
    col = lax.broadcasted_iota(jnp.int32, (2 * BLK, BLK), 1)
    from_prev = col > (row & (BLK - 1))
    first_rows = lax.broadcasted_iota(jnp.int32, (2 * BLK, 1), 0) < BLK
    carry = [{} for _ in range(PROMPT_SEQS)]

    ngrp = nblk // IN_GROUP
    grp_rows = IN_GROUP * BLK

    def stage_in(q, g):
        cell = {}
        first = (g % ngrp) * IN_GROUP

        def chunk(c):
            def run():
                if "xb" not in cell:
                    src = xn_ref[q] if g == ngrp else x_ref[q, g * grp_rows:(g + 1) * grp_rows, :]
                    cell["xb"] = src.astype(BF16)
                if c < IN_CHUNKS:
                    cols, zz = slice(c * CHUNK_COLS, (c + 1) * CHUNK_COLS), _dot(cell["xb"], w_in[c][...])
                else:
                    cols, zz = slice(IN_CHUNKS * CHUNK_COLS, IN_COLS), _dot(cell["xb"], w_tail[...])
                for i in range(IN_GROUP):
                    z_scr[q, first + i, :, cols] = zz[i * BLK:(i + 1) * BLK, :]
            return run
        return [chunk(c) for c in range(IN_CHUNKS + 1)]

    def stage_out(q, j):
        slot, rows_j, cell = j % 2, slice(j * BLK, (j + 1) * BLK), {"m": [], "g": []}

        def mix_chunk(c):
            def run():
                cell["m"].append(_dot(mix_scr[q, slot], w_out[c][...]))
                if c + 1 == OUT_CHUNKS:
                    hpre = DN_ALPHA * x_ref[q, rows_j, :] + jnp.concatenate(cell.pop("m"), axis=1)
                    h = _layernorm(hpre, v1024_ref[0:1, :], v1024_ref[1:2, :])
                    cell["h"] = h
                    cell["hb"] = h.astype(BF16)
            return run

        def gate_chunk(c):
            def run():
                cell["g"].append(_sigmoid(_dot(cell["hb"], w_pg[c][...])))
            return run

        def finish():
            pb = p_ref[q, rows_j, :].astype(BF16)
            for c in range(OUT_CHUNKS):
                cs = slice(c * CHUNK_COLS, (c + 1) * CHUNK_COLS)
                y_ref[q, rows_j, cs] = cell["h"][:, cs] + cell["g"][c] * _dot(pb, w_pp[c][...])
        return ([mix_chunk(c) for c in range(OUT_CHUNKS)] + [gate_chunk(c) for c in range(OUT_CHUNKS)]
                + [finish])

    def mix_units(q, j):
        slot, rows_j, s = j % 2, slice(j * BLK, (j + 1) * BLK), {}

        def zc(c0, width):
            return z_scr[q, j, :, c0:c0 + width]

        def a_rope():
            qcols = [_rope(zc(C_Q + c * LANES, LANES), cq_ref[rows_j, :], sq_ref[rows_j, :]).astype(BF16)
                     for c in range(4)]
            k_rot = _rope(zc(C_K, KV_WIDTH), ck_ref[rows_j, :], sk_ref[rows_j, :])
            v_new = zc(C_V, KV_WIDTH)
            if j == 0:
                s["k_prev"] = [kprev_scr[q, i] for i in range(4)]
                s["v_prev"] = [vprev_scr[q, i] for i in range(4)]
            else:
                s["k_prev"], s["v_prev"] = carry[q]["k_var"], carry[q]["v_var"]
            k_var = _kv_variants(k_rot, 0.0)
            v_var = _kv_variants(v_new, 1.0)
            carry[q]["k_var"], carry[q]["v_var"] = k_var, v_var
            s["k_var"], s["v_var"] = k_var, v_var
            if j == nblk - 1:
                for i in range(4):
                    kprev_scr[q, i] = k_var[i]
                    vprev_scr[q, i] = v_var[i]
                kk_ref[q] = k_rot
                vk_ref[q] = v_new
            s["qst"] = [jnp.concatenate([qcols[2 * g], qcols[2 * g + 1]], axis=0) for g in range(2)]
            s["scores"], s["probs"], s["res"] = {}, {}, {}

        def a_scores(i):
            def run():
                s_prev = _dot_nt(s["qst"][i // 2], s["k_prev"][i])
                if j == 0:
                    s_prev = jnp.where(t > 0, s_prev, NEG_INF)
                s["scores"][i] = jnp.where(from_prev, s_prev, _dot_nt(s["qst"][i // 2], s["k_var"][i]))
            return run

        def a_softmax(i):
            def run():
                g, e = i // 2, i % 2
                sc = s["scores"].pop(i)
                sink_col = jnp.where(first_rows, sinks_ref[4 * g + e], sinks_ref[4 * g + 2 + e])
                m = jnp.maximum(jnp.max(sc, axis=-1, keepdims=True), sink_col)
                p = jnp.exp(sc - m)
                p_both = jnp.concatenate([jnp.where(from_prev, p, 0.0), jnp.where(from_prev, 0.0, p)], axis=1)
                s["probs"][i] = (p_both.astype(BF16), jnp.exp(sink_col - m))
            return run

        def a_values(i):
            def run():
                p_both, es = s["probs"].pop(i)
                r = _dot(p_both, jnp.concatenate([s["v_prev"][i], s["v_var"][i]], axis=0))
                s["res"][i] = r + _sum_half(i % 2, es, 2 * BLK)
            return run

        def a_finish():
            acols = _assemble_attn([s["res"][i] for i in range(4)], BLK)
            acols = _gated_rmsnorm_cols(acols, attn_g, zc(C_GA, ATTN_WIDTH), ATTN_WIDTH)
            for c in range(4):
                mix_scr[q, slot, :, c * LANES:(c + 1) * LANES] = acols[c].astype(BF16)

        def h_prep():
            lf = _log_decay(zc(C_HF, HG_KW), log_lb, log1m_lb) * LOG2E
            kin = 1.0 - jnp.exp2(lf)
            lf_hi, lf_lo = _split_bf16(lf)
            m_low = mlow_ref[0:LOW_LEVELS * BLK, :]
            m_cum = mlow_ref[LOW_LEVELS * BLK:(LOW_LEVELS + 1) * BLK, :]
            eg = jnp.concatenate([_dot(m_low, lf_hi), _dot(m_cum, lf_hi) + _dot(m_cum, lf_lo)], axis=0)
            s["heads"] = []
            for h in range(HG_HEADS):
                hs = slice(h * LANES, (h + 1) * LANES)
                s["heads"].append(dict(
                    q=zc(C_HQ + h * LANES, LANES), k=kin[:, hs],
                    e_low=[eg[l * BLK:(l + 1) * BLK, hs] for l in range(LOW_LEVELS)],
                    g=eg[LOW_LEVELS * BLK:(LOW_LEVELS + 1) * BLK, hs], u=[], p=[]))

        def h_operand(h, l):
            def run():
                hd = s["heads"][h]
                hd["u"].append(_level_operand(l, hd["q"], hd["k"], hd["e_low"], hd["g"]).astype(BF16))
            return run

        def h_products(h):
            def run():
                hd = s["heads"][h]
                hd["p"] = [_dot_nt(u, u) for u in hd.pop("u")]
            return run

        def h_combine(h):
            def run():
                hd = s["heads"][h]
                a = jnp.zeros((BLK, BLK), F32)
                for l, p in enumerate(hd.pop("p")):
                    a = jnp.where(lvl == l, p, a)
                diag = jnp.sum(hd["q"] * hd["k"], axis=-1, keepdims=True)
                hd["a"] = jnp.where(lvl == -1, diag, a).astype(BF16)
                g_last = hd["g"][BLK - 1:BLK, :]
                hd["qe"] = (hd["q"] * jnp.exp2(hd["g"])).astype(BF16)
                hd["kd"] = (hd["k"] * jnp.exp2(g_last - hd["g"])).astype(BF16)
                hd["dec"] = jnp.exp2(g_last)
            return run

        def h_output(h):
            def run():
                hd = s["heads"][h]
                v_h = zc(C_HI + h * LANES, LANES).astype(BF16)
                st = st_scr[q, h]
                hd["o"] = _dot(hd.pop("a"), v_h) + _dot_nt(hd.pop("qe"), st.astype(BF16))
                st_scr[q, h] = st * hd.pop("dec") + _dot_tn(v_h, hd.pop("kd"))
            return run

        def h_finish(h):
            def run():
                o_h = s["heads"][h].pop("o")
                gt = zc(C_GH + h * LANES, LANES)
                inv = lax.rsqrt(jnp.mean(o_h * o_h, axis=-1, keepdims=True) + NORM_EPS)
                mix_scr[q, slot, :, ATTN_WIDTH + h * LANES:ATTN_WIDTH + (h + 1) * LANES] = (
                    o_h * inv * hg_g[:, h * LANES:(h + 1) * LANES] * _silu(gt)).astype(BF16)
            return run

        heads, variants = range(HG_HEADS), range(4)
        return ([a_rope, h_prep]
                + _merge([a_scores(i) for i in variants],
                         [h_operand(h, l) for h in heads for l in range(PROMPT_LEVELS)])
                + _merge([a_softmax(i) for i in variants], [h_products(h) for h in heads])
                + _merge([a_values(i) for i in variants], [h_combine(h) for h in heads])
                + _merge([a_finish], [h_output(h) for h in heads])
                + [h_finish(h) for h in heads])

    def stage_mix(j):
        units = mix_units(0, j)
        for q in range(1, PROMPT_SEQS):
            units = _merge(units, mix_units(q, j))
        return units

    @pl.when(jnp.logical_and(pl.program_id(0) == 0, t == 0))
    def _():
        for q in range(PROMPT_SEQS):
            for th in stage_in(q, 0):
                th()

    def both(stage, j):
        lists = [stage(q, j) for q in range(PROMPT_SEQS)]
        return [th for group in zip(*lists) for th in group]

    for j in range(nblk):
        nxt = both(stage_in, j // IN_GROUP + 1)
        part = j % IN_GROUP
        per = -(-len(nxt) // IN_GROUP)
        side = nxt[part * per:(part + 1) * per]
        if j >= 1:
            side = side + both(stage_out, j - 1)
        _interleave(stage_mix(j), side)
    for th in both(stage_out, nblk - 1):
        th()

    @pl.when(t == pl.num_programs(1) - 1)
    def _():
        for q in range(PROMPT_SEQS):
            for h in range(HG_HEADS):
                sfin_ref[q, h] = st_scr[q, h].T


def _sample_kernel(sinks_ref, x_ref, p_ref, ck_ref, cv_ref, s0_ref, cq_ref, sq_ref, ckt_ref, skt_ref,
                   lvl_ref, mlow_ref, seg_ref, *refs):
    w_in, w_tail, w_out, w_pg, w_pp, rest = _split_refs(refs)
    v512_ref, v1024_ref, y_ref, nk_ref, nv_ref, ns_ref, y_scr, qm_scr, attn_scr, mix_scr = rest
    layer = pl.program_id(0)
    blk = pl.program_id(1)
    tk = DEC_TOKENS
    rows = SAMPLE_SEQS * tk
    yrows = pl.ds(pl.multiple_of(blk * rows, rows), rows)

    @pl.when(layer == 0)
    def _():
        y_scr[yrows, :] = x_ref[...].reshape(rows, D_MODEL)

    x = y_scr[yrows, :]
    xb = x.astype(BF16)
    z = jnp.concatenate([_chunked_dot(xb, w_in), _dot(xb, w_tail[...])], axis=1)
    lvl = lvl_ref[...]
    attn_g = v512_ref[0:1, :]
    log_lb = v512_ref[1:2, :]
    log1m_lb = v512_ref[2:3, :]
    hg_g = v512_ref[3:4, :]

    qcols = [_rope(z[:, C_Q + c * LANES:C_Q + (c + 1) * LANES], cq_ref[...], sq_ref[...]) for c in range(4)]
    k_rot = _rope(z[:, C_K:C_K + KV_WIDTH], ckt_ref[...], skt_ref[...])
    v_new = z[:, C_V:C_V + KV_WIDTH]
    k_new_b = k_rot.astype(BF16)
    v_new_b = v_new.astype(BF16)
    lo2 = _lane_iota(2 * rows) < HEAD_DIM
    for g in range(2):
        xg = jnp.concatenate([qcols[2 * g], qcols[2 * g + 1]], axis=0)
        xsw = pltpu.roll(xg, HEAD_DIM, 1)
        keep = lo2 if g == 0 else jnp.logical_not(lo2)
        for e in range(2):
            qm = jnp.where(keep, xg if e == g else xsw, 0.0)
            for cc in range(2):
                r0 = ((g * 2 + e) * 2 + cc) * tk
                for s in range(SAMPLE_SEQS):
                    qm_scr[s, r0:r0 + tk, :] = qm[cc * rows + s * tk:cc * rows + (s + 1) * tk, :]

    lf = _log_decay(z[:, C_HF:C_HF + HG_KW], log_lb, log1m_lb) * LOG2E
    kin = 1.0 - jnp.exp2(lf)
    lf_hi, lf_lo = _split_bf16(lf)
    eg = _dot(mlow_ref[...], lf_hi) + _dot(mlow_ref[...], lf_lo)
    dec_t = jnp.exp2(_dot_tn(lf_hi, seg_ref[...]) + _dot_tn(lf_lo, seg_ref[...]))
    o_intra, qe, kd, vh = [], [], [], []
    for h in range(HG_HEADS):
        hs = slice(h * LANES, (h + 1) * LANES)
        q_h = z[:, C_HQ + h * LANES:C_HQ + (h + 1) * LANES]
        k_h = kin[:, hs]
        v_h = z[:, C_HI + h * LANES:C_HI + (h + 1) * LANES]
        e_low = [eg[l * rows:(l + 1) * rows, hs] for l in range(LOW_LEVELS)]
        g_h = eg[LOW_LEVELS * rows:(LOW_LEVELS + 1) * rows, hs]
        tot_h = eg[(LOW_LEVELS + 1) * rows:(LOW_LEVELS + 2) * rows, hs]
        a = _hgrn_scores(q_h, k_h, e_low, g_h, lvl, LOW_LEVELS)
        o_intra.append(_dot(a.astype(BF16), v_h.astype(BF16)))
        qe.append(q_h * jnp.exp2(g_h))
        kd.append(k_h * jnp.exp2(tot_h - g_h))
        vh.append(v_h)

    grp = 8
    grows = grp * QM_ROWS
    rid = lax.broadcasted_iota(jnp.int32, (grows, 1), 0)
    r8 = (rid >> TOK_BITS) & (ATTN_HEADS - 1)
    sink_col = jnp.zeros((grows, 1), F32)
    for idx in range(ATTN_HEADS):
        g, e, cc = idx // 4, (idx // 2) % 2, idx % 2
        sink_col = jnp.where(r8 == idx, sinks_ref[layer * ATTN_HEADS + 4 * g + 2 * cc + e], sink_col)
    row = lax.broadcasted_iota(jnp.int32, (grows, 2 * WINDOW), 0)
    col = lax.broadcasted_iota(jnp.int32, (grows, 2 * WINDOW), 1)
    tok = row & (tk - 1)
    new_col = col - WINDOW
    mask_cache = jnp.logical_and(col < WINDOW, col > tok)
    lo8 = _lane_iota(tk) < HEAD_DIM
    o_inter = [[None] * SAMPLE_SEQS for _ in range(HG_HEADS)]
    for gi in range(SAMPLE_SEQS // grp):
        seqs = range(gi * grp, (gi + 1) * grp)
        kcs = {s: ck_ref[s] for s in seqs}
        vcs = {s: cv_ref[s] for s in seqs}
        sc = [_dot_nt(qm_scr[s].astype(BF16),
                      jnp.concatenate([kcs[s].astype(BF16), k_new_b], axis=0)) for s in seqs]
        for s in seqs:
            nk_ref[s, 0:WINDOW - tk, :] = kcs[s][tk:WINDOW, :]
            nk_ref[s, WINDOW - tk:WINDOW, :] = k_rot[s * tk:(s + 1) * tk, :]
            nv_ref[s, 0:WINDOW - tk, :] = vcs[s][tk:WINDOW, :]
            nv_ref[s, WINDOW - tk:WINDOW, :] = v_new[s * tk:(s + 1) * tk, :]
        same_seq = (new_col >> TOK_BITS) == (row >> QM_BITS) + gi * grp
        mask_new = jnp.logical_and(jnp.logical_and(col >= WINDOW, same_seq), (new_col & (tk - 1)) <= tok)
        s_all = jnp.where(jnp.logical_or(mask_cache, mask_new), jnp.concatenate(sc, axis=0), NEG_INF)
        m = jnp.maximum(jnp.max(s_all, axis=-1, keepdims=True), sink_col)
        p = jnp.exp(s_all - m)
        den = jnp.sum(p, axis=-1, keepdims=True) + jnp.exp(sink_col - m)
        pb = p.astype(BF16)
        o_all = jnp.concatenate(
            [_dot(pb[i * QM_ROWS:(i + 1) * QM_ROWS, :],
                  jnp.concatenate([vcs[s].astype(BF16), v_new_b], axis=0))
             for i, s in enumerate(seqs)], axis=0) / den
        o_sw = pltpu.roll(o_all, HEAD_DIM, 1)
        for i, s in enumerate(seqs):
            for c in range(4):
                g, cc = c // 2, c % 2
                ra = i * QM_ROWS + ((g * 2 + 0) * 2 + cc) * tk
                rb = i * QM_ROWS + ((g * 2 + 1) * 2 + cc) * tk
                part0 = (o_all if g == 0 else o_sw)[ra:ra + tk, :]
                part1 = (o_all if g == 1 else o_sw)[rb:rb + tk, :]
                attn_scr[c, s * tk:(s + 1) * tk, :] = jnp.where(lo8, part0, part1)
        s0s = {(s, h): s0_ref[s, h] for s in seqs for h in range(HG_HEADS)}
        for s in seqs:
            for h in range(HG_HEADS):
                o_inter[h][s] = _dot(qe[h][s * tk:(s + 1) * tk, :].astype(BF16), s0s[(s, h)].astype(BF16))
        upd = {(s, h): _dot_tn(kd[h][s * tk:(s + 1) * tk, :].astype(BF16), vh[h][s * tk:(s + 1) * tk, :].astype(BF16))
               for s in seqs for h in range(HG_HEADS)}
        for s in seqs:
            for h in range(HG_HEADS):
                dcol = dec_t[h * LANES:(h + 1) * LANES, s * tk:s * tk + 1]
                ns_ref[s, h] = dcol * s0s[(s, h)] + upd[(s, h)]

    acols = _gated_rmsnorm_cols([attn_scr[c] for c in range(4)], attn_g, z[:, C_GA:C_GA + ATTN_WIDTH],
                                ATTN_WIDTH)
    for c in range(4):
        mix_scr[:, c * LANES:(c + 1) * LANES] = acols[c].astype(BF16)
    for h in range(HG_HEADS):
        hs = slice(h * LANES, (h + 1) * LANES)
        o_h = o_intra[h] + jnp.concatenate(o_inter[h], axis=0)
        gt = z[:, C_GH + h * LANES:C_GH + (h + 1) * LANES]
        inv = lax.rsqrt(jnp.mean(o_h * o_h, axis=-1, keepdims=True) + NORM_EPS)
        mix_scr[:, ATTN_WIDTH + h * LANES:ATTN_WIDTH + (h + 1) * LANES] = (
            o_h * inv * hg_g[:, hs] * _silu(gt)).astype(BF16)

    hpre = DN_ALPHA * x + _chunked_dot(mix_scr[...], w_out)
    hn = _layernorm(hpre, v1024_ref[0:1, :], v1024_ref[1:2, :])
    gate = _sigmoid(_chunked_dot(hn.astype(BF16), w_pg))
    pp = _chunked_dot(p_ref[...].reshape(rows, PLE_DIM).astype(BF16), w_pp)
    y = hn + gate * pp
    y_scr[yrows, :] = y
    y_ref[...] = y.reshape(SAMPLE_SEQS, tk, D_MODEL)


def _level_matrix():
    t = np.arange(BLK)[:, None]
    s = np.arange(BLK)[None, :]
    x = t ^ s
    lv = np.floor(np.log2(np.maximum(x, 1))).astype(np.int32)
    return np.where(t > s, lv, np.where(t == s, -1, -2)).astype(np.int32)


def _level_exponent_matrix(l):
    m = np.zeros((BLK, BLK), np.float32)
    b, h = 2 << l, 1 << l
    for t in range(BLK):
        mid = t - t % b + h
        if t >= mid:
            m[t, mid:t + 1] = 1.0
        else:
            m[t, t + 1:mid] = 1.0
    return m


def _mask_matrices(seq_rows):
    blocks = [_level_exponent_matrix(l) for l in range(LOW_LEVELS)]
    t = np.arange(BLK)[:, None]
    s = np.arange(BLK)[None, :]
    same = (t // seq_rows) == (s // seq_rows)
    blocks.append((same & (s <= t)).astype(np.float32))
    if seq_rows < BLK:
        blocks.append(same.astype(np.float32))
    return np.concatenate(blocks, axis=0)


def _rope_tables(pos, scale):
    half = HEAD_DIM // 2
    inv = jnp.exp(-math.log(ROPE_THETA) * jnp.arange(half, dtype=F32) * 2.0 / HEAD_DIM)
    ang = pos.astype(F32)[:, None] * inv[None, :]
    cos = jnp.cos(ang) * scale
    sin = jnp.sin(ang) * scale
    return jnp.tile(cos, (1, 4)), jnp.concatenate([-sin, sin, -sin, sin], axis=1)


def _weight_specs(layer_of):
    def window(rows, width, c):
        return pl.BlockSpec((None, rows, width), lambda *g, c=c: (layer_of(*g), 0, c),
                            pipeline_mode=pl.Buffered(1))
    specs = [window(D_MODEL, CHUNK_COLS, c) for c in range(IN_CHUNKS)]
    specs.append(window(D_MODEL, IN_TAIL, IN_CHUNKS * CHUNK_COLS // IN_TAIL))
    for rows in (MIX_WIDTH, D_MODEL, PLE_DIM):
        specs += [window(rows, CHUNK_COLS, c) for c in range(OUT_CHUNKS)]
    return specs


def _weight_args(weights):
    w_in, w_out, w_pg, w_pp = weights
    return [w_in] * (IN_CHUNKS + 1) + [w_out] * OUT_CHUNKS + [w_pg] * OUT_CHUNKS + [w_pp] * OUT_CHUNKS


def _const_spec(shape, layer=None):
    if layer is None:
        return pl.BlockSpec(shape, lambda *_: (0,) * len(shape))
    return pl.BlockSpec((None,) + shape, lambda *_: (layer,) + (0,) * len(shape))


def _prompt_layer(i, x, p_all, tables, lvl, mlow, weights, sinks, v512, v1024):
    B, T, _ = x.shape
    nt = T // PROMPT_TILE
    nb = B // PROMPT_SEQS
    tab_spec = pl.BlockSpec((PROMPT_TILE, LANES), lambda b, t: (t, 0))
    nblk = PROMPT_TILE // BLK
    assert nblk % IN_GROUP == 0 and nblk % 2 == 0

    def next_first_group(b, t):
        flat = jnp.minimum(b * nt + t + 1, nb * nt - 1)
        return (flat // nt, (flat % nt) * (nblk // IN_GROUP), 0)
    in_specs = [
        pl.BlockSpec(memory_space=pltpu.SMEM),
        pl.BlockSpec((PROMPT_SEQS, PROMPT_TILE, D_MODEL), lambda b, t: (b, t, 0)),
        pl.BlockSpec((PROMPT_SEQS, IN_GROUP * BLK, D_MODEL), next_first_group),
        pl.BlockSpec((None, PROMPT_SEQS, PROMPT_TILE, PLE_DIM), lambda b, t: (i, b, t, 0)),
        tab_spec, tab_spec, tab_spec, tab_spec,
        _const_spec((BLK, BLK)),
        _const_spec(mlow.shape),
    ] + _weight_specs(lambda b, t: i) + [
        _const_spec((4, ATTN_WIDTH), i),
        _const_spec((2, D_MODEL), i),
    ]
    out_shape = [
        jax.ShapeDtypeStruct((B, T, D_MODEL), F32),
        jax.ShapeDtypeStruct((B, WINDOW, KV_WIDTH), F32),
        jax.ShapeDtypeStruct((B, WINDOW, KV_WIDTH), F32),
        jax.ShapeDtypeStruct((B, HG_HEADS, HG_DK, HG_DV), F32),
    ]
    out_specs = [
        pl.BlockSpec((PROMPT_SEQS, PROMPT_TILE, D_MODEL), lambda b, t: (b, t, 0)),
        pl.BlockSpec((PROMPT_SEQS, WINDOW, KV_WIDTH), lambda b, t: (b, 0, 0)),
        pl.BlockSpec((PROMPT_SEQS, WINDOW, KV_WIDTH), lambda b, t: (b, 0, 0)),
        pl.BlockSpec((PROMPT_SEQS, HG_HEADS, HG_DK, HG_DV), lambda b, t: (b, 0, 0, 0)),
    ]
    scratch = [
        pltpu.VMEM((PROMPT_SEQS, HG_HEADS, HG_DV, HG_DK), F32),
        pltpu.VMEM((PROMPT_SEQS, 4, BLK, LANES), BF16),
        pltpu.VMEM((PROMPT_SEQS, 4, BLK, LANES), BF16),
        pltpu.VMEM((PROMPT_SEQS, nblk, BLK, IN_COLS), F32),
        pltpu.VMEM((PROMPT_SEQS, 2, BLK, MIX_WIDTH), BF16),
    ]
    return pl.pallas_call(
        _prompt_kernel,
        grid=(nb, nt),
        in_specs=in_specs,
        out_specs=out_specs,
        out_shape=out_shape,
        scratch_shapes=scratch,
        compiler_params=pltpu.CompilerParams(
            dimension_semantics=("arbitrary", "arbitrary"), vmem_limit_bytes=VMEM_LIMIT),
        name=f"prompt_layer{i}",
    )(sinks, x, x, p_all, *tables, lvl, mlow, *_weight_args(weights), v512, v1024)


def _sample_call(x, p_all, ck_all, cv_all, s0_all, tables, lvl, mlow, seg, weights, sinks, v512, v1024):
    B, T, _ = x.shape
    rows = SAMPLE_SEQS * T
    tab_spec = _const_spec((rows, LANES))
    in_specs = [
        pl.BlockSpec(memory_space=pltpu.SMEM),
        pl.BlockSpec((SAMPLE_SEQS, T, D_MODEL), lambda l, b: (jnp.where(l == 0, b, 0), 0, 0)),
        pl.BlockSpec((None, SAMPLE_SEQS, T, PLE_DIM), lambda l, b: (l, b, 0, 0)),
        pl.BlockSpec((None, SAMPLE_SEQS, WINDOW, KV_WIDTH), lambda l, b: (l, b, 0, 0)),
        pl.BlockSpec((None, SAMPLE_SEQS, WINDOW, KV_WIDTH), lambda l, b: (l, b, 0, 0)),
        pl.BlockSpec((None, SAMPLE_SEQS, HG_HEADS, HG_DK, HG_DV), lambda l, b: (l, b, 0, 0, 0)),
        tab_spec, tab_spec, tab_spec, tab_spec,
        _const_spec((BLK, BLK)),
        _const_spec(mlow.shape),
        _const_spec((BLK, LANES)),
    ] + _weight_specs(lambda l, b: l) + [
        pl.BlockSpec((None, 4, ATTN_WIDTH), lambda l, b: (l, 0, 0)),
        pl.BlockSpec((None, 2, D_MODEL), lambda l, b: (l, 0, 0)),
    ]
    out_shape = [
        jax.ShapeDtypeStruct((B, T, D_MODEL), F32),
        jax.ShapeDtypeStruct((DEPTH, B, WINDOW, KV_WIDTH), F32),
        jax.ShapeDtypeStruct((DEPTH, B, WINDOW, KV_WIDTH), F32),
        jax.ShapeDtypeStruct((DEPTH, B, HG_HEADS, HG_DK, HG_DV), F32),
    ]
    out_specs = [
        pl.BlockSpec((SAMPLE_SEQS, T, D_MODEL), lambda l, b: (jnp.where(l == DEPTH - 1, b, 0), 0, 0)),
        pl.BlockSpec((None, SAMPLE_SEQS, WINDOW, KV_WIDTH), lambda l, b: (l, b, 0, 0)),
        pl.BlockSpec((None, SAMPLE_SEQS, WINDOW, KV_WIDTH), lambda l, b: (l, b, 0, 0)),
        pl.BlockSpec((None, SAMPLE_SEQS, HG_HEADS, HG_DK, HG_DV), lambda l, b: (l, b, 0, 0, 0)),
    ]
    scratch = [
        pltpu.VMEM((B * T, D_MODEL), F32),
        pltpu.VMEM((SAMPLE_SEQS, QM_ROWS, LANES), F32),
        pltpu.VMEM((4, rows, LANES), F32),
        pltpu.VMEM((rows, MIX_WIDTH), BF16),
    ]
    return pl.pallas_call(
        _sample_kernel,
        grid=(DEPTH, B // SAMPLE_SEQS),
        in_specs=in_specs,
        out_specs=out_specs,
        out_shape=out_shape,
        scratch_shapes=scratch,
        compiler_params=pltpu.CompilerParams(
            dimension_semantics=("arbitrary", "arbitrary"), vmem_limit_bytes=VMEM_LIMIT),
        name="sample_layers",
    )(sinks, x, p_all, ck_all, cv_all, s0_all, *tables, lvl, mlow, seg, *_weight_args(weights), v512, v1024)


def kernel(x_prompt, x_sample, cache_k_win, cache_v_win, state_hgrn, p_prompt, p_sample, w_in, attn_sinks,
           attn_norm_g, hg_lb_logits, hg_norm_g, w_out, ln_g, ln_b, w_ple_proj, w_ple_gate):
    B, T, _ = x_prompt.shape
    SB, ST, _ = x_sample.shape
    assert T % PROMPT_TILE == 0 and B % PROMPT_SEQS == 0
    assert SB % SAMPLE_SEQS == 0 and ST == DEC_TOKENS and SAMPLE_SEQS * ST == BLK
    assert 1 << TOK_BITS == DEC_TOKENS and 1 << QM_BITS == QM_ROWS
    assert cache_k_win.shape[2] == WINDOW

    cs = jnp.cumsum(jax.nn.softmax(hg_lb_logits.astype(F32), axis=0), axis=0)
    lbs = cs - cs[:1]
    v512 = jnp.stack([attn_norm_g.astype(F32), jnp.log(lbs), jnp.log1p(-lbs), hg_norm_g.astype(F32)], axis=1)
    v1024 = jnp.stack([ln_g.astype(F32), ln_b.astype(F32)], axis=1)
    weights = tuple(w.astype(BF16) for w in (w_in, w_out, w_ple_gate, w_ple_proj))
    sinks = attn_sinks.astype(F32)

    scale = HEAD_DIM ** -0.5
    pos_p = jnp.arange(T, dtype=jnp.int32)
    pos_s = jnp.tile(PAST_LEN + jnp.arange(ST, dtype=jnp.int32), SAMPLE_SEQS)
    tab_p = _rope_tables(pos_p, scale) + _rope_tables(pos_p, 1.0)
    tab_s = _rope_tables(pos_s, scale) + _rope_tables(pos_s, 1.0)
    lvl = jnp.asarray(_level_matrix())
    mlow_p = jnp.asarray(_mask_matrices(BLK), dtype=BF16)
    mlow_s = jnp.asarray(_mask_matrices(ST), dtype=BF16)
    seg = jnp.asarray((np.arange(BLK)[:, None] // ST) == (np.arange(LANES)[None, :] // ST), dtype=BF16)

    ck = cache_k_win.reshape(DEPTH, SB, WINDOW, KV_WIDTH)
    cv = cache_v_win.reshape(DEPTH, SB, WINDOW, KV_WIDTH)
    ys, ks, vs, ss = _sample_call(x_sample, p_sample, ck, cv, state_hgrn, tab_s, lvl, mlow_s, seg, weights,
                                  sinks.reshape(DEPTH * ATTN_HEADS), v512, v1024)

    yp = x_prompt
    kp_l, vp_l, sp_l = [], [], []
    for i in range(DEPTH):
        yp, kp, vp, sp = _prompt_layer(i, yp, p_prompt, tab_p, lvl, mlow_p, weights, sinks[i], v512, v1024)
        kp_l.append(kp); vp_l.append(vp); sp_l.append(sp)

    def kv5(a, b):
        return a.reshape(DEPTH, b, WINDOW, KV_HEADS, HEAD_DIM)

    return (yp, ys, kv5(jnp.stack(kp_l), B), kv5(jnp.stack(vp_l), B), jnp.stack(sp_l),
            kv5(ks, SB), kv5(vs, SB), ss)
```

```python
import math

import numpy as np
import jax
import jax.numpy as jnp
from jax import lax
from jax.experimental import pallas as pl
from jax.experimental.pallas import tpu as pltpu

D_MODEL = 1024
DEPTH = 4
PAST_LEN = 8192
ATTN_HEADS = 8
KV_HEADS = 2
HEAD_DIM = 64
ATTN_WIDTH = ATTN_HEADS * HEAD_DIM
KV_WIDTH = KV_HEADS * HEAD_DIM
WINDOW = 128
ROPE_THETA = 10000.0
HG_HEADS = 4
HG_DK = 128
HG_DV = 128
HG_KW = HG_HEADS * HG_DK
HG_VW = HG_HEADS * HG_DV
MIX_WIDTH = ATTN_WIDTH + HG_VW
IN_COLS = 2 * ATTN_WIDTH + 2 * KV_WIDTH + 2 * HG_KW + 2 * HG_VW
PLE_DIM = 256
DN_ALPHA = (2 * DEPTH) ** 0.25
NORM_EPS = 1e-5
NEG_INF = -1e30
LOG2E = 1.4426950408889634

C_Q = 0
C_K = C_Q + ATTN_WIDTH
C_V = C_K + KV_WIDTH
C_GA = C_V + KV_WIDTH
C_HQ = C_GA + ATTN_WIDTH
C_HF = C_HQ + HG_KW
C_HI = C_HF + HG_KW
C_GH = C_HI + HG_VW

HALF_DIM = HEAD_DIM // 2
DEC_TOKENS = 8
TOK_BITS = 3
QM_ROWS = ATTN_HEADS * DEC_TOKENS
QM_BITS = 6

LANES = 128
MXU_COLS = 256
CHUNK_COLS = 2 * MXU_COLS
BLK = 128
PROMPT_TILE = 512
PROMPT_SEQS = 1
IN_GROUP = 2
SAMPLE_SEQS = 16
SAMPLE_GROUP = 8
LOW_LEVELS = 3
PROMPT_LEVELS = 7
VMEM_LIMIT = 56 * 1024 * 1024
IN_CHUNKS = IN_COLS // CHUNK_COLS
IN_TAIL = IN_COLS - IN_CHUNKS * CHUNK_COLS
OUT_CHUNKS = D_MODEL // CHUNK_COLS
N_WEIGHT_REFS = IN_CHUNKS + 1 + 3 * OUT_CHUNKS

F32 = jnp.float32
BF16 = jnp.bfloat16


def _dot(a, b):
    return jnp.dot(a, b, preferred_element_type=F32)


def _dot_nt(a, b):
    return lax.dot_general(a, b, (((1,), (1,)), ((), ())), preferred_element_type=F32)


def _dot_tn(a, b):
    return lax.dot_general(a, b, (((0,), (0,)), ((), ())), preferred_element_type=F32)


def _split_bf16(x):
    hi = x.astype(BF16)
    lo = (x - hi.astype(F32)).astype(BF16)
    return hi, lo


def _sigmoid(x):
    return 0.5 * jnp.tanh(0.5 * x) + 0.5


def _silu(x):
    h = 0.5 * x
    return h * jnp.tanh(h) + h


def _lane_iota(rows):
    return lax.broadcasted_iota(jnp.int32, (rows, LANES), 1)


def _rope(x, cos_t, sin_t):
    lane = _lane_iota(x.shape[0])
    first_half = (lane & HALF_DIM) == 0
    swapped = jnp.where(first_half, pltpu.roll(x, LANES - HALF_DIM, 1), pltpu.roll(x, HALF_DIM, 1))
    return x * cos_t + swapped * sin_t


def _kv_variants(a, fill):
    lane = _lane_iota(a.shape[0])
    lo = lane < HEAD_DIM
    sw = pltpu.roll(a, HEAD_DIM, 1)
    f = jnp.full_like(a, fill)
    out = [jnp.where(lo, a, f), jnp.where(lo, f, sw), jnp.where(lo, sw, f), jnp.where(lo, f, a)]
    return [o.astype(BF16) for o in out]


def _sum_half(e, es, rows):
    lane = _lane_iota(rows)
    sum_half = (lane >= HEAD_DIM) if e == 0 else (lane < HEAD_DIM)
    return jnp.where(sum_half, es, 0.0)


def _assemble_attn(res, rows):
    lane = _lane_iota(rows)
    lo = lane < HEAD_DIM
    cols = []
    for c in range(4):
        g, cc = c // 2, c % 2
        r0 = res[g * 2 + 0][cc * rows:(cc + 1) * rows]
        r1 = res[g * 2 + 1][cc * rows:(cc + 1) * rows]
        num = jnp.where(lo, r0, r1)
        den = pltpu.roll(jnp.where(lo, r1, r0), HEAD_DIM, 1)
        cols.append(num / den)
    return cols


def _log_decay(hf, log_lb, log1m_lb):
    ls = jnp.minimum(hf, 0.0) - jnp.log(1.0 + jnp.exp(-jnp.abs(hf)))
    b = log1m_lb + ls
    return jnp.maximum(log_lb, b) + jnp.log(1.0 + jnp.exp(-jnp.abs(log_lb - b)))


def _level_operand(l, q_h, k_h, e_low, g_h):
    if l < LOW_LEVELS:
        rows = lax.broadcasted_iota(jnp.int32, (BLK, LANES), 0)
        upper = ((rows >> l) & 1) == 1
        return jnp.where(upper, q_h, k_h) * jnp.exp2(e_low[l])
    b, h = 2 << l, 1 << l
    pieces = []
    for i in range(BLK // b):
        lower, upper = slice(i * b, i * b + h), slice(i * b + h, (i + 1) * b)
        g_mid = g_h[i * b + h - 1:i * b + h, :]
        pieces.append(k_h[lower, :] * jnp.exp2(g_mid - g_h[lower, :]))
        pieces.append(q_h[upper, :] * jnp.exp2(g_h[upper, :] - g_mid))
    return jnp.concatenate(pieces, axis=0)


def _hgrn_scores(q_h, k_h, e_low, g_h, lvl, levels):
    a = jnp.zeros((BLK, BLK), F32)
    for l in range(levels):
        u = _level_operand(l, q_h, k_h, e_low, g_h).astype(BF16)
        a = jnp.where(lvl == l, _dot_nt(u, u), a)
    diag = jnp.sum(q_h * k_h, axis=-1, keepdims=True)
    return jnp.where(lvl == -1, diag, a)


def _gated_rmsnorm_cols(cols, gain_row, gate, width):
    ss = None
    for c in cols:
        s = jnp.sum(c * c, axis=-1, keepdims=True)
        ss = s if ss is None else ss + s
    inv = lax.rsqrt(ss * (1.0 / width) + NORM_EPS)
    out = []
    for i, c in enumerate(cols):
        gt = gate[:, i * LANES:(i + 1) * LANES]
        out.append(c * inv * gain_row[:, i * LANES:(i + 1) * LANES] * _silu(gt))
    return out


def _layernorm(hpre, ln_g, ln_b):
    mu = jnp.mean(hpre, axis=-1, keepdims=True)
    cen = hpre - mu
    var = jnp.mean(cen * cen, axis=-1, keepdims=True)
    return cen * lax.rsqrt(var + NORM_EPS) * ln_g + ln_b


def _chunked_dot(a, w_refs):
    return jnp.concatenate([_dot(a, w[...]) for w in w_refs], axis=1)


def _split_refs(refs):
    n = IN_CHUNKS + 1
    return (refs[:IN_CHUNKS], refs[IN_CHUNKS], refs[n:n + OUT_CHUNKS], refs[n + OUT_CHUNKS:n + 2 * OUT_CHUNKS],
            refs[n + 2 * OUT_CHUNKS:N_WEIGHT_REFS], refs[N_WEIGHT_REFS:])


def _merge(a, b):
    out, ia, ib = [], 0, 0
    while ia < len(a) or ib < len(b):
        if ib >= len(b) or (ia < len(a) and ia * len(b) <= ib * len(a)):
            out.append(a[ia])
            ia += 1
        else:
            out.append(b[ib])
            ib += 1
    return out


def _interleave(units, thunks):
    thunks = list(thunks)
    total, done = len(thunks), 0
    for seen, fn in enumerate(units, 1):
        fn()
        want = min(total, -(-seen * total // len(units)))
        while done < want:
            thunks[done]()
            done += 1


def _prompt_kernel(sinks_ref, x_ref, xn_ref, p_ref, cq_ref, sq_ref, ck_ref, sk_ref, lvl_ref, mlow_ref, *refs):
    w_in, w_tail, w_out, w_pg, w_pp, rest = _split_refs(refs)
    (v512_ref, v1024_ref, y_ref, kk_ref, vk_ref, sfin_ref,
     st_scr, kprev_scr, vprev_scr, z_scr, mix_scr) = rest
    t = pl.program_id(1)
    nblk = PROMPT_TILE // BLK

    @pl.when(t == 0)
    def _():
        st_scr[...] = jnp.zeros_like(st_scr)
        kprev_scr[...] = jnp.zeros_like(kprev_scr)
        vprev_scr[...] = jnp.zeros_like(vprev_scr)

    lvl = lvl_ref[...]
    attn_g = v512_ref[0:1, :]
    log_lb = v512_ref[1:2, :]
    log1m_lb = v512_ref[2:3, :]
    hg_g = v512_ref[3:4, :]

    row = lax.broadcasted_iota(jnp.int32, (2 * BLK, BLK), 0)
    col = lax.broadcasted_iota(jnp.int32, (2 * BLK, BLK), 1)
    from_prev = col > (row & (BLK - 1))
    first_rows = lax.broadcasted_iota(jnp.int32, (2 * BLK, 1), 0) < BLK
    carry = [{} for _ in range(PROMPT_SEQS)]

    ngrp = nblk // IN_GROUP
    grp_rows = IN_GROUP * BLK

    def stage_in(q, g):
        cell = {}
        first = (g % ngrp) * IN_GROUP

        def chunk(c):
            def run():
                if "xb" not in cell:
                    src = xn_ref[q] if g == ngrp else x_ref[q, g * grp_rows:(g + 1) * grp_rows, :]
                    cell["xb"] = src.astype(BF16)
                if c < IN_CHUNKS:
                    cols, zz = slice(c * CHUNK_COLS, (c + 1) * CHUNK_COLS), _dot(cell["xb"], w_in[c][...])
                else:
                    cols, zz = slice(IN_CHUNKS * CHUNK_COLS, IN_COLS), _dot(cell["xb"], w_tail[...])
                for i in range(IN_GROUP):
                    z_scr[q, first + i, :, cols] = zz[i * BLK:(i + 1) * BLK, :]
            return run
        return [chunk(c) for c in range(IN_CHUNKS + 1)]

    def stage_out(q, j):
        slot, rows_j, cell = j % 2, slice(j * BLK, (j + 1) * BLK), {"m": [], "g": []}

        def mix_chunk(c):
            def run():
                cell["m"].append(_dot(mix_scr[q, slot], w_out[c][...]))
                if c + 1 == OUT_CHUNKS:
                    hpre = DN_ALPHA * x_ref[q, rows_j, :] + jnp.concatenate(cell.pop("m"), axis=1)
                    h = _layernorm(hpre, v1024_ref[0:1, :], v1024_ref[1:2, :])
                    cell["h"] = h
                    cell["hb"] = h.astype(BF16)
            return run

        def gate_chunk(c):
            def run():
                cell["g"].append(_sigmoid(_dot(cell["hb"], w_pg[c][...])))
            return run

        def finish():
            pb = p_ref[q, rows_j, :].astype(BF16)
            for c in range(OUT_CHUNKS):
                cs = slice(c * CHUNK_COLS, (c + 1) * CHUNK_COLS)
                y_ref[q, rows_j, cs] = cell["h"][:, cs] + cell["g"][c] * _dot(pb, w_pp[c][...])
        return ([mix_chunk(c) for c in range(OUT_CHUNKS)] + [gate_chunk(c) for c in range(OUT_CHUNKS)]
                + [finish])

    def mix_units(q, j):
        slot, rows_j, s = j % 2, slice(j * BLK, (j + 1) * BLK), {}

        def zc(c0, width):
            return z_scr[q, j, :, c0:c0 + width]

        def a_rope():
            qcols = [_rope(zc(C_Q + c * LANES, LANES), cq_ref[rows_j, :], sq_ref[rows_j, :]).astype(BF16)
                     for c in range(4)]
            k_rot = _rope(zc(C_K, KV_WIDTH), ck_ref[rows_j, :], sk_ref[rows_j, :])
            v_new = zc(C_V, KV_WIDTH)
            if j == 0:
                s["k_prev"] = [kprev_scr[q, i] for i in range(4)]
                s["v_prev"] = [vprev_scr[q, i] for i in range(4)]
            else:
                s["k_prev"], s["v_prev"] = carry[q]["k_var"], carry[q]["v_var"]
            k_var = _kv_variants(k_rot, 0.0)
            v_var = _kv_variants(v_new, 1.0)
            carry[q]["k_var"], carry[q]["v_var"] = k_var, v_var
            s["k_var"], s["v_var"] = k_var, v_var
            if j == nblk - 1:
                for i in range(4):
                    kprev_scr[q, i] = k_var[i]
                    vprev_scr[q, i] = v_var[i]
                kk_ref[q] = k_rot
                vk_ref[q] = v_new
            s["qst"] = [jnp.concatenate([qcols[2 * g], qcols[2 * g + 1]], axis=0) for g in range(2)]
            s["scores"], s["probs"], s["res"] = {}, {}, {}

        def a_scores(i):
            def run():
                s_prev = _dot_nt(s["qst"][i // 2], s["k_prev"][i])
                if j == 0:
                    s_prev = jnp.where(t > 0, s_prev, NEG_INF)
                s["scores"][i] = jnp.where(from_prev, s_prev, _dot_nt(s["qst"][i // 2], s["k_var"][i]))
            return run

        def a_softmax(i):
            def run():
                g, e = i // 2, i % 2
                sc = s["scores"].pop(i)
                sink_col = jnp.where(first_rows, sinks_ref[4 * g + e], sinks_ref[4 * g + 2 + e])
                m = jnp.maximum(jnp.max(sc, axis=-1, keepdims=True), sink_col)
                p = jnp.exp(sc - m)
                p_both = jnp.concatenate([jnp.where(from_prev, p, 0.0), jnp.where(from_prev, 0.0, p)], axis=1)
                s["probs"][i] = (p_both.astype(BF16), jnp.exp(sink_col - m))
            return run

        def a_values(i):
            def run():
                p_both, es = s["probs"].pop(i)
                r = _dot(p_both, jnp.concatenate([s["v_prev"][i], s["v_var"][i]], axis=0))
                s["res"][i] = r + _sum_half(i % 2, es, 2 * BLK)
            return run

        def a_finish():
            acols = _assemble_attn([s["res"][i] for i in range(4)], BLK)
            acols = _gated_rmsnorm_cols(acols, attn_g, zc(C_GA, ATTN_WIDTH), ATTN_WIDTH)
            for c in range(4):
                mix_scr[q, slot, :, c * LANES:(c + 1) * LANES] = acols[c].astype(BF16)

        def h_prep():
            lf = _log_decay(zc(C_HF, HG_KW), log_lb, log1m_lb) * LOG2E
            kin = 1.0 - jnp.exp2(lf)
            lf_hi, lf_lo = _split_bf16(lf)
            m_low = mlow_ref[0:LOW_LEVELS * BLK, :]
            m_cum = mlow_ref[LOW_LEVELS * BLK:(LOW_LEVELS + 1) * BLK, :]
            eg = jnp.concatenate([_dot(m_low, lf_hi), _dot(m_cum, lf_hi) + _dot(m_cum, lf_lo)], axis=0)
            s["heads"] = []
            for h in range(HG_HEADS):
                hs = slice(h * LANES, (h + 1) * LANES)
                s["heads"].append(dict(
                    q=zc(C_HQ + h * LANES, LANES), k=kin[:, hs],
                    e_low=[eg[l * BLK:(l + 1) * BLK, hs] for l in range(LOW_LEVELS)],
                    g=eg[LOW_LEVELS * BLK:(LOW_LEVELS + 1) * BLK, hs], u=[], p=[]))

        def h_operand(h, l):
            def run():
                hd = s["heads"][h]
                hd["u"].append(_level_operand(l, hd["q"], hd["k"], hd["e_low"], hd["g"]).astype(BF16))
            return run

        def h_products(h):
            def run():
                hd = s["heads"][h]
                hd["p"] = [_dot_nt(u, u) for u in hd.pop("u")]
            return run

        def h_combine(h):
            def run():
                hd = s["heads"][h]
                a = jnp.zeros((BLK, BLK), F32)
                for l, p in enumerate(hd.pop("p")):
                    a = jnp.where(lvl == l, p, a)
                diag = jnp.sum(hd["q"] * hd["k"], axis=-1, keepdims=True)
                hd["a"] = jnp.where(lvl == -1, diag, a).astype(BF16)
                g_last = hd["g"][BLK - 1:BLK, :]
                hd["qe"] = (hd["q"] * jnp.exp2(hd["g"])).astype(BF16)
                hd["kd"] = (hd["k"] * jnp.exp2(g_last - hd["g"])).astype(BF16)
                hd["dec"] = jnp.exp2(g_last)
            return run

        def h_output(h):
            def run():
                hd = s["heads"][h]
                v_h = zc(C_HI + h * LANES, LANES).astype(BF16)
                st = st_scr[q, h]
                hd["o"] = _dot(hd.pop("a"), v_h) + _dot_nt(hd.pop("qe"), st.astype(BF16))
                st_scr[q, h] = st * hd.pop("dec") + _dot_tn(v_h, hd.pop("kd"))
            return run

        def h_finish(h):
            def run():
                o_h = s["heads"][h].pop("o")
                gt = zc(C_GH + h * LANES, LANES)
                inv = lax.rsqrt(jnp.mean(o_h * o_h, axis=-1, keepdims=True) + NORM_EPS)
                mix_scr[q, slot, :, ATTN_WIDTH + h * LANES:ATTN_WIDTH + (h + 1) * LANES] = (
                    o_h * inv * hg_g[:, h * LANES:(h + 1) * LANES] * _silu(gt)).astype(BF16)
            return run

        heads, variants = range(HG_HEADS), range(4)
        return ([a_rope, h_prep]
                + _merge([a_scores(i) for i in variants],
                         [h_operand(h, l) for h in heads for l in range(PROMPT_LEVELS)])
                + _merge([a_softmax(i) for i in variants], [h_products(h) for h in heads])
                + _merge([a_values(i) for i in variants], [h_combine(h) for h in heads])
                + _merge([a_finish], [h_output(h) for h in heads])
                + [h_finish(h) for h in heads])

    def stage_mix(j):
        units = mix_units(0, j)
        for q in range(1, PROMPT_SEQS):
            units = _merge(units, mix_units(q, j))
        return units

    @pl.when(jnp.logical_and(pl.program_id(0) == 0, t == 0))
    def _():
        for q in range(PROMPT_SEQS):
            for th in stage_in(q, 0):
                th()

    def both(stage, j):
        lists = [stage(q, j) for q in range(PROMPT_SEQS)]
        return [th for group in zip(*lists) for th in group]

    for j in range(nblk):
        nxt = both(stage_in, j // IN_GROUP + 1)
        part = j % IN_GROUP
        per = -(-len(nxt) // IN_GROUP)
        side = nxt[part * per:(part + 1) * per]
        if j >= 1:
            side = side + both(stage_out, j - 1)
        _interleave(stage_mix(j), side)
    for th in both(stage_out, nblk - 1):
        th()

    @pl.when(t == pl.num_programs(1) - 1)
    def _():
        for q in range(PROMPT_SEQS):
            for h in range(HG_HEADS):
                sfin_ref[q, h] = st_scr[q, h].T


def _sample_kernel(sinks_ref, x_ref, p_ref, ck_ref, cv_ref, s0_ref, cq_ref, sq_ref, ckt_ref, skt_ref,
                   lvl_ref, mlow_ref, seg_ref, *refs):
    w_in, w_tail, w_out, w_pg, w_pp, rest = _split_refs(refs)
    v512_ref, v1024_ref, y_ref, nk_ref, nv_ref, ns_ref, y_scr, qm_scr, attn_scr, mix_scr = rest
    layer = pl.program_id(0)
    blk = pl.program_id(1)
    tk = DEC_TOKENS
    rows = SAMPLE_SEQS * tk
    yrows = pl.ds(pl.multiple_of(blk * rows, rows), rows)

    @pl.when(layer == 0)
    def _():
        y_scr[yrows, :] = x_ref[...].reshape(rows, D_MODEL)

    x = y_scr[yrows, :]
    xb = x.astype(BF16)
    z_all = jnp.concatenate([_chunked_dot(xb, w_in), _dot(xb, w_tail[...])], axis=1)

    def z(c0, width):
        return z_all[:, c0:c0 + width]
    lvl = lvl_ref[...]
    attn_g = v512_ref[0:1, :]
    log_lb = v512_ref[1:2, :]
    log1m_lb = v512_ref[2:3, :]
    hg_g = v512_ref[3:4, :]

    qcols = [_rope(z(C_Q + c * LANES, LANES), cq_ref[...], sq_ref[...]) for c in range(4)]
    k_rot = _rope(z(C_K, KV_WIDTH), ckt_ref[...], skt_ref[...])
    v_new = z(C_V, KV_WIDTH)
    k_new_b = k_rot.astype(BF16)
    v_new_b = v_new.astype(BF16)
    lo2 = _lane_iota(2 * rows) < HEAD_DIM
    for g in range(2):
        xg = jnp.concatenate([qcols[2 * g], qcols[2 * g + 1]], axis=0)
        xsw = pltpu.roll(xg, HEAD_DIM, 1)
        keep = lo2 if g == 0 else jnp.logical_not(lo2)
        for e in range(2):
            qm = jnp.where(keep, xg if e == g else xsw, 0.0)
            for cc in range(2):
                r0 = ((g * 2 + e) * 2 + cc) * tk
                for s in range(SAMPLE_SEQS):
                    qm_scr[s, r0:r0 + tk, :] = qm[cc * rows + s * tk:cc * rows + (s + 1) * tk, :]

    lf = _log_decay(z(C_HF, HG_KW), log_lb, log1m_lb) * LOG2E
    kin = 1.0 - jnp.exp2(lf)
    lf_hi, lf_lo = _split_bf16(lf)
    eg = _dot(mlow_ref[...], lf_hi) + _dot(mlow_ref[...], lf_lo)
    dec_t = jnp.exp2(_dot_tn(lf_hi, seg_ref[...]) + _dot_tn(lf_lo, seg_ref[...]))
    o_intra, qe, kd, vh = [], [], [], []
    for h in range(HG_HEADS):
        hs = slice(h * LANES, (h + 1) * LANES)
        q_h = z(C_HQ + h * LANES, LANES)
        k_h = kin[:, hs]
        v_h = z(C_HI + h * LANES, LANES)
        e_low = [eg[l * rows:(l + 1) * rows, hs] for l in range(LOW_LEVELS)]
        g_h = eg[LOW_LEVELS * rows:(LOW_LEVELS + 1) * rows, hs]
        tot_h = eg[(LOW_LEVELS + 1) * rows:(LOW_LEVELS + 2) * rows, hs]
        a = _hgrn_scores(q_h, k_h, e_low, g_h, lvl, LOW_LEVELS)
        o_intra.append(_dot(a.astype(BF16), v_h.astype(BF16)))
        qe.append(q_h * jnp.exp2(g_h))
        kd.append(k_h * jnp.exp2(tot_h - g_h))
        vh.append(v_h)

    grp = SAMPLE_GROUP
    grows = grp * QM_ROWS
    rid = lax.broadcasted_iota(jnp.int32, (grows, 1), 0)
    r8 = (rid >> TOK_BITS) & (ATTN_HEADS - 1)
    sink_col = jnp.zeros((grows, 1), F32)
    for idx in range(ATTN_HEADS):
        g, e, cc = idx // 4, (idx // 2) % 2, idx % 2
        sink_col = jnp.where(r8 == idx, sinks_ref[layer * ATTN_HEADS + 4 * g + 2 * cc + e], sink_col)
    row = lax.broadcasted_iota(jnp.int32, (grows, 2 * WINDOW), 0)
    col = lax.broadcasted_iota(jnp.int32, (grows, 2 * WINDOW), 1)
    tok = row & (tk - 1)
    new_col = col - WINDOW
    mask_cache = jnp.logical_and(col < WINDOW, col > tok)
    lo8 = _lane_iota(tk) < HEAD_DIM
    o_inter = [[None] * SAMPLE_SEQS for _ in range(HG_HEADS)]
    for gi in range(SAMPLE_SEQS // grp):
        seqs = range(gi * grp, (gi + 1) * grp)
        kcs = {s: ck_ref[s] for s in seqs}
        vcs = {s: cv_ref[s] for s in seqs}
        sc = [_dot_nt(qm_scr[s].astype(BF16),
                      jnp.concatenate([kcs[s].astype(BF16), k_new_b], axis=0)) for s in seqs]
        for s in seqs:
            nk_ref[s, 0:WINDOW - tk, :] = kcs[s][tk:WINDOW, :]
            nk_ref[s, WINDOW - tk:WINDOW, :] = k_rot[s * tk:(s + 1) * tk, :]
            nv_ref[s, 0:WINDOW - tk, :] = vcs[s][tk:WINDOW, :]
            nv_ref[s, WINDOW - tk:WINDOW, :] = v_new[s * tk:(s + 1) * tk, :]
        same_seq = (new_col >> TOK_BITS) == (row >> QM_BITS) + gi * grp
        mask_new = jnp.logical_and(jnp.logical_and(col >= WINDOW, same_seq), (new_col & (tk - 1)) <= tok)
        s_all = jnp.where(jnp.logical_or(mask_cache, mask_new), jnp.concatenate(sc, axis=0), NEG_INF)
        m = jnp.maximum(jnp.max(s_all, axis=-1, keepdims=True), sink_col)
        p = jnp.exp(s_all - m)
        den = jnp.sum(p, axis=-1, keepdims=True) + jnp.exp(sink_col - m)
        pb = p.astype(BF16)
        o_all = jnp.concatenate(
            [_dot(pb[i * QM_ROWS:(i + 1) * QM_ROWS, :],
                  jnp.concatenate([vcs[s].astype(BF16), v_new_b], axis=0))
             for i, s in enumerate(seqs)], axis=0) / den
        o_sw = pltpu.roll(o_all, HEAD_DIM, 1)
        for i, s in enumerate(seqs):
            for c in range(4):
                g, cc = c // 2, c % 2
                ra = i * QM_ROWS + ((g * 2 + 0) * 2 + cc) * tk
                rb = i * QM_ROWS + ((g * 2 + 1) * 2 + cc) * tk
                part0 = (o_all if g == 0 else o_sw)[ra:ra + tk, :]
                part1 = (o_all if g == 1 else o_sw)[rb:rb + tk, :]
                attn_scr[c, s * tk:(s + 1) * tk, :] = jnp.where(lo8, part0, part1)
        s0s = {(s, h): s0_ref[s, h] for s in seqs for h in range(HG_HEADS)}
        for s in seqs:
            for h in range(HG_HEADS):
                o_inter[h][s] = _dot(qe[h][s * tk:(s + 1) * tk, :].astype(BF16), s0s[(s, h)].astype(BF16))
        upd = {(s, h): _dot_tn(kd[h][s * tk:(s + 1) * tk, :].astype(BF16), vh[h][s * tk:(s + 1) * tk, :].astype(BF16))
               for s in seqs for h in range(HG_HEADS)}
        for s in seqs:
            for h in range(HG_HEADS):
                dcol = dec_t[h * LANES:(h + 1) * LANES, s * tk:s * tk + 1]
                ns_ref[s, h] = dcol * s0s[(s, h)] + upd[(s, h)]

    acols = _gated_rmsnorm_cols([attn_scr[c] for c in range(4)], attn_g, z(C_GA, ATTN_WIDTH),
                                ATTN_WIDTH)
    for c in range(4):
        mix_scr[:, c * LANES:(c + 1) * LANES] = acols[c].astype(BF16)
    for h in range(HG_HEADS):
        hs = slice(h * LANES, (h + 1) * LANES)
        o_h = o_intra[h] + jnp.concatenate(o_inter[h], axis=0)
        gt = z(C_GH + h * LANES, LANES)
        inv = lax.rsqrt(jnp.mean(o_h * o_h, axis=-1, keepdims=True) + NORM_EPS)
        mix_scr[:, ATTN_WIDTH + h * LANES:ATTN_WIDTH + (h + 1) * LANES] = (
            o_h * inv * hg_g[:, hs] * _silu(gt)).astype(BF16)

    hpre = DN_ALPHA * x + _chunked_dot(mix_scr[...], w_out)
    hn = _layernorm(hpre, v1024_ref[0:1, :], v1024_ref[1:2, :])
    gate = _sigmoid(_chunked_dot(hn.astype(BF16), w_pg))
    pp = _chunked_dot(p_ref[...].reshape(rows, PLE_DIM).astype(BF16), w_pp)
    y = hn + gate * pp
    y_scr[yrows, :] = y
    y_ref[...] = y.reshape(SAMPLE_SEQS, tk, D_MODEL)


def _level_matrix():
    t = np.arange(BLK)[:, None]
    s = np.arange(BLK)[None, :]
    x = t ^ s
    lv = np.floor(np.log2(np.maximum(x, 1))).astype(np.int32)
    return np.where(t > s, lv, np.where(t == s, -1, -2)).astype(np.int32)


def _level_exponent_matrix(l):
    m = np.zeros((BLK, BLK), np.float32)
    b, h = 2 << l, 1 << l
    for t in range(BLK):
        mid = t - t % b + h
        if t >= mid:
            m[t, mid:t + 1] = 1.0
        else:
            m[t, t + 1:mid] = 1.0
    return m


def _mask_matrices(seq_rows):
    blocks = [_level_exponent_matrix(l) for l in range(LOW_LEVELS)]
    t = np.arange(BLK)[:, None]
    s = np.arange(BLK)[None, :]
    same = (t // seq_rows) == (s // seq_rows)
    blocks.append((same & (s <= t)).astype(np.float32))
    if seq_rows < BLK:
        blocks.append(same.astype(np.float32))
    return np.concatenate(blocks, axis=0)


def _rope_tables(pos, scale):
    half = HEAD_DIM // 2
    inv = jnp.exp(-math.log(ROPE_THETA) * jnp.arange(half, dtype=F32) * 2.0 / HEAD_DIM)
    ang = pos.astype(F32)[:, None] * inv[None, :]
    cos = jnp.cos(ang) * scale
    sin = jnp.sin(ang) * scale
    return jnp.tile(cos, (1, 4)), jnp.concatenate([-sin, sin, -sin, sin], axis=1)


def _weight_specs(layer_of):
    def window(rows, width, c):
        return pl.BlockSpec((None, rows, width), lambda *g, c=c: (layer_of(*g), 0, c),
                            pipeline_mode=pl.Buffered(1))
    specs = [window(D_MODEL, CHUNK_COLS, c) for c in range(IN_CHUNKS)]
    specs.append(window(D_MODEL, IN_TAIL, IN_CHUNKS * CHUNK_COLS // IN_TAIL))
    for rows in (MIX_WIDTH, D_MODEL, PLE_DIM):
        specs += [window(rows, CHUNK_COLS, c) for c in range(OUT_CHUNKS)]
    return specs


def _weight_args(weights):
    w_in, w_out, w_pg, w_pp = weights
    return [w_in] * (IN_CHUNKS + 1) + [w_out] * OUT_CHUNKS + [w_pg] * OUT_CHUNKS + [w_pp] * OUT_CHUNKS


def _const_spec(shape, layer=None):
    if layer is None:
        return pl.BlockSpec(shape, lambda *_: (0,) * len(shape))
    return pl.BlockSpec((None,) + shape, lambda *_: (layer,) + (0,) * len(shape))


def _prompt_layer(i, x, p_all, tables, lvl, mlow, weights, sinks, v512, v1024):
    B, T, _ = x.shape
    nt = T // PROMPT_TILE
    nb = B // PROMPT_SEQS
    tab_spec = pl.BlockSpec((PROMPT_TILE, LANES), lambda b, t: (t, 0))
    nblk = PROMPT_TILE // BLK
    assert nblk % IN_GROUP == 0 and nblk % 2 == 0

    def next_first_group(b, t):
        flat = jnp.minimum(b * nt + t + 1, nb * nt - 1)
        return (flat // nt, (flat % nt) * (nblk // IN_GROUP), 0)
    in_specs = [
        pl.BlockSpec(memory_space=pltpu.SMEM),
        pl.BlockSpec((PROMPT_SEQS, PROMPT_TILE, D_MODEL), lambda b, t: (b, t, 0)),
        pl.BlockSpec((PROMPT_SEQS, IN_GROUP * BLK, D_MODEL), next_first_group),
        pl.BlockSpec((None, PROMPT_SEQS, PROMPT_TILE, PLE_DIM), lambda b, t: (i, b, t, 0)),
        tab_spec, tab_spec, tab_spec, tab_spec,
        _const_spec((BLK, BLK)),
        _const_spec(mlow.shape),
    ] + _weight_specs(lambda b, t: i) + [
        _const_spec((4, ATTN_WIDTH), i),
        _const_spec((2, D_MODEL), i),
    ]
    out_shape = [
        jax.ShapeDtypeStruct((B, T, D_MODEL), F32),
        jax.ShapeDtypeStruct((B, WINDOW, KV_WIDTH), F32),
        jax.ShapeDtypeStruct((B, WINDOW, KV_WIDTH), F32),
        jax.ShapeDtypeStruct((B, HG_HEADS, HG_DK, HG_DV), F32),
    ]
    out_specs = [
        pl.BlockSpec((PROMPT_SEQS, PROMPT_TILE, D_MODEL), lambda b, t: (b, t, 0)),
        pl.BlockSpec((PROMPT_SEQS, WINDOW, KV_WIDTH), lambda b, t: (b, 0, 0)),
        pl.BlockSpec((PROMPT_SEQS, WINDOW, KV_WIDTH), lambda b, t: (b, 0, 0)),
        pl.BlockSpec((PROMPT_SEQS, HG_HEADS, HG_DK, HG_DV), lambda b, t: (b, 0, 0, 0)),
    ]
    scratch = [
        pltpu.VMEM((PROMPT_SEQS, HG_HEADS, HG_DV, HG_DK), F32),
        pltpu.VMEM((PROMPT_SEQS, 4, BLK, LANES), BF16),
        pltpu.VMEM((PROMPT_SEQS, 4, BLK, LANES), BF16),
        pltpu.VMEM((PROMPT_SEQS, nblk, BLK, IN_COLS), F32),
        pltpu.VMEM((PROMPT_SEQS, 2, BLK, MIX_WIDTH), BF16),
    ]
    return pl.pallas_call(
        _prompt_kernel,
        grid=(nb, nt),
        in_specs=in_specs,
        out_specs=out_specs,
        out_shape=out_shape,
        scratch_shapes=scratch,
        compiler_params=pltpu.CompilerParams(
            dimension_semantics=("arbitrary", "arbitrary"), vmem_limit_bytes=VMEM_LIMIT),
        name=f"prompt_layer{i}",
    )(sinks, x, x, p_all, *tables, lvl, mlow, *_weight_args(weights), v512, v1024)


def _sample_call(x, p_all, ck_all, cv_all, s0_all, tables, lvl, mlow, seg, weights, sinks, v512, v1024):
    B, T, _ = x.shape
    rows = SAMPLE_SEQS * T
    tab_spec = _const_spec((rows, LANES))
    in_specs = [
        pl.BlockSpec(memory_space=pltpu.SMEM),
        pl.BlockSpec((SAMPLE_SEQS, T, D_MODEL), lambda l, b: (jnp.where(l == 0, b, 0), 0, 0)),
        pl.BlockSpec((None, SAMPLE_SEQS, T, PLE_DIM), lambda l, b: (l, b, 0, 0)),
        pl.BlockSpec((None, SAMPLE_SEQS, WINDOW, KV_WIDTH), lambda l, b: (l, b, 0, 0)),
        pl.BlockSpec((None, SAMPLE_SEQS, WINDOW, KV_WIDTH), lambda l, b: (l, b, 0, 0)),
        pl.BlockSpec((None, SAMPLE_SEQS, HG_HEADS, HG_DK, HG_DV), lambda l, b: (l, b, 0, 0, 0)),
        tab_spec, tab_spec, tab_spec, tab_spec,
        _const_spec((BLK, BLK)),
        _const_spec(mlow.shape),
        _const_spec((BLK, LANES)),
    ] + _weight_specs(lambda l, b: l) + [
        pl.BlockSpec((None, 4, ATTN_WIDTH), lambda l, b: (l, 0, 0)),
        pl.BlockSpec((None, 2, D_MODEL), lambda l, b: (l, 0, 0)),
    ]
    out_shape = [
        jax.ShapeDtypeStruct((B, T, D_MODEL), F32),
        jax.ShapeDtypeStruct((DEPTH, B, WINDOW, KV_WIDTH), F32),
        jax.ShapeDtypeStruct((DEPTH, B, WINDOW, KV_WIDTH), F32),
        jax.ShapeDtypeStruct((DEPTH, B, HG_HEADS, HG_DK, HG_DV), F32),
    ]
    out_specs = [
        pl.BlockSpec((SAMPLE_SEQS, T, D_MODEL), lambda l, b: (jnp.where(l == DEPTH - 1, b, 0), 0, 0)),
        pl.BlockSpec((None, SAMPLE_SEQS, WINDOW, KV_WIDTH), lambda l, b: (l, b, 0, 0)),
        pl.BlockSpec((None, SAMPLE_SEQS, WINDOW, KV_WIDTH), lambda l, b: (l, b, 0, 0)),
        pl.BlockSpec((None, SAMPLE_SEQS, HG_HEADS, HG_DK, HG_DV), lambda l, b: (l, b, 0, 0, 0)),
    ]
    scratch = [
        pltpu.VMEM((B * T, D_MODEL), F32),
        pltpu.VMEM((SAMPLE_SEQS, QM_ROWS, LANES), F32),
        pltpu.VMEM((4, rows, LANES), F32),
        pltpu.VMEM((rows, MIX_WIDTH), BF16),
    ]
    return pl.pallas_call(
        _sample_kernel,
        grid=(DEPTH, B // SAMPLE_SEQS),
        in_specs=in_specs,
        out_specs=out_specs,
        out_shape=out_shape,
        scratch_shapes=scratch,
        compiler_params=pltpu.CompilerParams(
            dimension_semantics=("arbitrary", "arbitrary"), vmem_limit_bytes=VMEM_LIMIT),
        name="sample_layers",
    )(sinks, x, p_all, ck_all, cv_all, s0_all, *tables, lvl, mlow, seg, *_weight_args(weights), v512, v1024)


def kernel(x_prompt, x_sample, cache_k_win, cache_v_win, state_hgrn, p_prompt, p_sample, w_in, attn_sinks,
           attn_norm_g, hg_lb_logits, hg_norm_g, w_out, ln_g, ln_b, w_ple_proj, w_ple_gate):
    B, T, _ = x_prompt.shape
    SB, ST, _ = x_sample.shape
    assert T % PROMPT_TILE == 0 and B % PROMPT_SEQS == 0
    assert SB % SAMPLE_SEQS == 0 and ST == DEC_TOKENS and SAMPLE_SEQS * ST == BLK
    assert 1 << TOK_BITS == DEC_TOKENS and 1 << QM_BITS == QM_ROWS
    assert cache_k_win.shape[2] == WINDOW

    cs = jnp.cumsum(jax.nn.softmax(hg_lb_logits.astype(F32), axis=0), axis=0)
    lbs = cs - cs[:1]
    v512 = jnp.stack([attn_norm_g.astype(F32), jnp.log(lbs), jnp.log1p(-lbs), hg_norm_g.astype(F32)], axis=1)
    v1024 = jnp.stack([ln_g.astype(F32), ln_b.astype(F32)], axis=1)
    weights = tuple(w.astype(BF16) for w in (w_in, w_out, w_ple_gate, w_ple_proj))
    sinks = attn_sinks.astype(F32)

    scale = HEAD_DIM ** -0.5
    pos_p = jnp.arange(T, dtype=jnp.int32)
    pos_s = jnp.tile(PAST_LEN + jnp.arange(ST, dtype=jnp.int32), SAMPLE_SEQS)
    tab_p = _rope_tables(pos_p, scale) + _rope_tables(pos_p, 1.0)
    tab_s = _rope_tables(pos_s, scale) + _rope_tables(pos_s, 1.0)
    lvl = jnp.asarray(_level_matrix())
    mlow_p = jnp.asarray(_mask_matrices(BLK), dtype=BF16)
    mlow_s = jnp.asarray(_mask_matrices(ST), dtype=BF16)
    seg = jnp.asarray((np.arange(BLK)[:, None] // ST) == (np.arange(LANES)[None, :] // ST), dtype=BF16)

    ck = cache_k_win.reshape(DEPTH, SB, WINDOW, KV_WIDTH)
    cv = cache_v_win.reshape(DEPTH, SB, WINDOW, KV_WIDTH)
    ys, ks, vs, ss = _sample_call(x_sample, p_sample, ck, cv, state_hgrn, tab_s, lvl, mlow_s, seg, weights,
                                  sinks.reshape(DEPTH * ATTN_HEADS), v512, v1024)

    yp = x_prompt
    kp_l, vp_l, sp_l = [], [], []
    for i in range(DEPTH):
        yp, kp, vp, sp = _prompt_layer(i, yp, p_prompt, tab_p, lvl, mlow_p, weights, sinks[i], v512, v1024)
        kp_l.append(kp); vp_l.append(vp); sp_l.append(sp)

    def kv5(a, b):
        return a.reshape(DEPTH, b, WINDOW, KV_HEADS, HEAD_DIM)

    return (yp, ys, kv5(jnp.stack(kp_l), B), kv5(jnp.stack(vp_l), B), jnp.stack(sp_l),
            kv5(ks, SB), kv5(vs, SB), ss)
```

```python
import math

import numpy as np
import jax
import jax.numpy as jnp
from jax import lax
from jax.experimental import pallas as pl
from jax.experimental.pallas import tpu as pltpu

D_MODEL = 1024
DEPTH = 4
PAST_LEN = 8192
ATTN_HEADS = 8
KV_HEADS = 2
HEAD_DIM = 64
ATTN_WIDTH = ATTN_HEADS * HEAD_DIM
KV_WIDTH = KV_HEADS * HEAD_DIM
WINDOW = 128
ROPE_THETA = 10000.0
HG_HEADS = 4
HG_DK = 128
HG_DV = 128
HG_KW = HG_HEADS * HG_DK
HG_VW = HG_HEADS * HG_DV
MIX_WIDTH = ATTN_WIDTH + HG_VW
IN_COLS = 2 * ATTN_WIDTH + 2 * KV_WIDTH + 2 * HG_KW + 2 * HG_VW
PLE_DIM = 256
DN_ALPHA = (2 * DEPTH) ** 0.25
NORM_EPS = 1e-5
NEG_INF = -1e30
LOG2E = 1.4426950408889634

C_Q = 0
C_K = C_Q + ATTN_WIDTH
C_V = C_K + KV_WIDTH
C_GA = C_V + KV_WIDTH
C_HQ = C_GA + ATTN_WIDTH
C_HF = C_HQ + HG_KW
C_HI = C_HF + HG_KW
C_GH = C_HI + HG_VW

HALF_DIM = HEAD_DIM // 2
DEC_TOKENS = 8
TOK_BITS = 3
QM_ROWS = ATTN_HEADS * DEC_TOKENS
QM_BITS = 6

LANES = 128
MXU_COLS = 256
CHUNK_COLS = 2 * MXU_COLS
BLK = 128
PROMPT_TILE = 512
PROMPT_SEQS = 1
IN_GROUP = 2
SAMPLE_SEQS = 16
SAMPLE_GROUP = 8
LOW_LEVELS = 3
PROMPT_LEVELS = 7
VMEM_LIMIT = 56 * 1024 * 1024
IN_CHUNKS = IN_COLS // CHUNK_COLS
IN_TAIL = IN_COLS - IN_CHUNKS * CHUNK_COLS
OUT_CHUNKS = D_MODEL // CHUNK_COLS
N_WEIGHT_REFS = IN_CHUNKS + 1 + 3 * OUT_CHUNKS

F32 = jnp.float32
BF16 = jnp.bfloat16


def _dot(a, b):
    return jnp.dot(a, b, preferred_element_type=F32)


def _dot_nt(a, b):
    return lax.dot_general(a, b, (((1,), (1,)), ((), ())), preferred_element_type=F32)


def _dot_tn(a, b):
    return lax.dot_general(a, b, (((0,), (0,)), ((), ())), preferred_element_type=F32)


def _split_bf16(x):
    hi = x.astype(BF16)
    lo = (x - hi.astype(F32)).astype(BF16)
    return hi, lo


def _sigmoid(x):
    return 0.5 * jnp.tanh(0.5 * x) + 0.5


def _silu(x):
    h = 0.5 * x
    return h * jnp.tanh(h) + h


def _lane_iota(rows):
    return lax.broadcasted_iota(jnp.int32, (rows, LANES), 1)


def _rope(x, cos_t, sin_t):
    lane = _lane_iota(x.shape[0])
    first_half = (lane & HALF_DIM) == 0
    swapped = jnp.where(first_half, pltpu.roll(x, LANES - HALF_DIM, 1), pltpu.roll(x, HALF_DIM, 1))
    return x * cos_t + swapped * sin_t


def _kv_variants(a, fill):
    lane = _lane_iota(a.shape[0])
    lo = lane < HEAD_DIM
    sw = pltpu.roll(a, HEAD_DIM, 1)
    f = jnp.full_like(a, fill)
    out = [jnp.where(lo, a, f), jnp.where(lo, f, sw), jnp.where(lo, sw, f), jnp.where(lo, f, a)]
    return [o.astype(BF16) for o in out]


def _sum_half(e, es, rows):
    lane = _lane_iota(rows)
    sum_half = (lane >= HEAD_DIM) if e == 0 else (lane < HEAD_DIM)
    return jnp.where(sum_half, es, 0.0)


def _assemble_attn(res, rows):
    lane = _lane_iota(rows)
    lo = lane < HEAD_DIM
    cols = []
    for c in range(4):
        g, cc = c // 2, c % 2
        r0 = res[g * 2 + 0][cc * rows:(cc + 1) * rows]
        r1 = res[g * 2 + 1][cc * rows:(cc + 1) * rows]
        num = jnp.where(lo, r0, r1)
        den = pltpu.roll(jnp.where(lo, r1, r0), HEAD_DIM, 1)
        cols.append(num / den)
    return cols


def _log_decay(hf, log_lb, log1m_lb):
    ls = jnp.minimum(hf, 0.0) - jnp.log(1.0 + jnp.exp(-jnp.abs(hf)))
    b = log1m_lb + ls
    return jnp.maximum(log_lb, b) + jnp.log(1.0 + jnp.exp(-jnp.abs(log_lb - b)))


def _level_operand(l, q_h, k_h, e_low, g_h):
    if l < LOW_LEVELS:
        rows = lax.broadcasted_iota(jnp.int32, (BLK, LANES), 0)
        upper = ((rows >> l) & 1) == 1
        return jnp.where(upper, q_h, k_h) * jnp.exp2(e_low[l])
    b, h = 2 << l, 1 << l
    pieces = []
    for i in range(BLK // b):
        lower, upper = slice(i * b, i * b + h), slice(i * b + h, (i + 1) * b)
        g_mid = g_h[i * b + h - 1:i * b + h, :]
        pieces.append(k_h[lower, :] * jnp.exp2(g_mid - g_h[lower, :]))
        pieces.append(q_h[upper, :] * jnp.exp2(g_h[upper, :] - g_mid))
    return jnp.concatenate(pieces, axis=0)


def _hgrn_scores(q_h, k_h, e_low, g_h, lvl, levels):
    a = jnp.zeros((BLK, BLK), F32)
    for l in range(levels):
        u = _level_operand(l, q_h, k_h, e_low, g_h).astype(BF16)
        a = jnp.where(lvl == l, _dot_nt(u, u), a)
    diag = jnp.sum(q_h * k_h, axis=-1, keepdims=True)
    return jnp.where(lvl == -1, diag, a)


def _gated_rmsnorm_cols(cols, gain_row, gate, width):
    ss = None
    for c in cols:
        s = jnp.sum(c * c, axis=-1, keepdims=True)
        ss = s if ss is None else ss + s
    inv = lax.rsqrt(ss * (1.0 / width) + NORM_EPS)
    out = []
    for i, c in enumerate(cols):
        gt = gate[:, i * LANES:(i + 1) * LANES]
        out.append(c * inv * gain_row[:, i * LANES:(i + 1) * LANES] * _silu(gt))
    return out


def _layernorm(hpre, ln_g, ln_b):
    mu = jnp.mean(hpre, axis=-1, keepdims=True)
    cen = hpre - mu
    var = jnp.mean(cen * cen, axis=-1, keepdims=True)
    return cen * lax.rsqrt(var + NORM_EPS) * ln_g + ln_b


def _chunked_dot(a, w_refs):
    return jnp.concatenate([_dot(a, w[...]) for w in w_refs], axis=1)


def _split_refs(refs):
    n = IN_CHUNKS + 1
    return (refs[:IN_CHUNKS], refs[IN_CHUNKS], refs[n:n + OUT_CHUNKS], refs[n + OUT_CHUNKS:n + 2 * OUT_CHUNKS],
            refs[n + 2 * OUT_CHUNKS:N_WEIGHT_REFS], refs[N_WEIGHT_REFS:])


def _merge(a, b):
    out, ia, ib = [], 0, 0
    while ia < len(a) or ib < len(b):
        if ib >= len(b) or (ia < len(a) and ia * len(b) <= ib * len(a)):
            out.append(a[ia])
            ia += 1
        else:
            out.append(b[ib])
            ib += 1
    return out


def _interleave(units, thunks):
    thunks = list(thunks)
    total, done = len(thunks), 0
    for seen, fn in enumerate(units, 1):
        fn()
        want = min(total, -(-seen * total // len(units)))
        while done < want:
            thunks[done]()
            done += 1


def _prompt_kernel(sinks_ref, x_ref, xn_ref, p_ref, cq_ref, sq_ref, ck_ref, sk_ref, lvl_ref, mlow_ref, *refs):
    w_in, w_tail, w_out, w_pg, w_pp, rest = _split_refs(refs)
    (v512_ref, v1024_ref, y_ref, kk_ref, vk_ref, sfin_ref,
     st_scr, kprev_scr, vprev_scr, z_scr, mix_scr) = rest
    t = pl.program_id(1)
    nblk = PROMPT_TILE // BLK

    @pl.when(t == 0)
    def _():
        st_scr[...] = jnp.zeros_like(st_scr)
        kprev_scr[...] = jnp.zeros_like(kprev_scr)
        vprev_scr[...] = jnp.zeros_like(vprev_scr)

    lvl = lvl_ref[...]
    attn_g = v512_ref[0:1, :]
    log_lb = v512_ref[1:2, :]
    log1m_lb = v512_ref[2:3, :]
    hg_g = v512_ref[3:4, :]

    row = lax.broadcasted_iota(jnp.int32, (2 * BLK, BLK), 0)
    col = lax.broadcasted_iota(jnp.int32, (2 * BLK, BLK), 1)
    from_prev = col > (row & (BLK - 1))
    first_rows = lax.broadcasted_iota(jnp.int32, (2 * BLK, 1), 0) < BLK
    carry = [{} for _ in range(PROMPT_SEQS)]

    ngrp = nblk // IN_GROUP
    grp_rows = IN_GROUP * BLK

    def stage_in(q, g):
        cell = {}
        first = (g % ngrp) * IN_GROUP

        def chunk(c):
            def run():
                if "xb" not in cell:
                    src = xn_ref[q] if g == ngrp else x_ref[q, g * grp_rows:(g + 1) * grp_rows, :]
                    cell["xb"] = src.astype(BF16)
                if c < IN_CHUNKS:
                    cols, zz = slice(c * CHUNK_COLS, (c + 1) * CHUNK_COLS), _dot(cell["xb"], w_in[c][...])
                else:
                    cols, zz = slice(IN_CHUNKS * CHUNK_COLS, IN_COLS), _dot(cell["xb"], w_tail[...])
                for i in range(IN_GROUP):
                    z_scr[q, first + i, :, cols] = zz[i * BLK:(i + 1) * BLK, :]
            return run
        return [chunk(c) for c in range(IN_CHUNKS + 1)]

    def stage_out(q, j):
        slot, rows_j, cell = j % 2, slice(j * BLK, (j + 1) * BLK), {"m": [], "g": []}

        def mix_chunk(c):
            def run():
                cell["m"].append(_dot(mix_scr[q, slot], w_out[c][...]))
                if c + 1 == OUT_CHUNKS:
                    hpre = DN_ALPHA * x_ref[q, rows_j, :] + jnp.concatenate(cell.pop("m"), axis=1)
                    h = _layernorm(hpre, v1024_ref[0:1, :], v1024_ref[1:2, :])
                    cell["h"] = h
                    cell["hb"] = h.astype(BF16)
            return run

        def gate_chunk(c):
            def run():
                cell["g"].append(_sigmoid(_dot(cell["hb"], w_pg[c][...])))
            return run

        def finish():
            pb = p_ref[q, rows_j, :].astype(BF16)
            for c in range(OUT_CHUNKS):
                cs = slice(c * CHUNK_COLS, (c + 1) * CHUNK_COLS)
                y_ref[q, rows_j, cs] = cell["h"][:, cs] + cell["g"][c] * _dot(pb, w_pp[c][...])
        return ([mix_chunk(c) for c in range(OUT_CHUNKS)] + [gate_chunk(c) for c in range(OUT_CHUNKS)]
                + [finish])

    def mix_units(q, j):
        slot, rows_j, s = j % 2, slice(j * BLK, (j + 1) * BLK), {}

        def zc(c0, width):
            return z_scr[q, j, :, c0:c0 + width]

        def a_rope():
            qcols = [_rope(zc(C_Q + c * LANES, LANES), cq_ref[rows_j, :], sq_ref[rows_j, :]).astype(BF16)
                     for c in range(4)]
            k_rot = _rope(zc(C_K, KV_WIDTH), ck_ref[rows_j, :], sk_ref[rows_j, :])
            v_new = zc(C_V, KV_WIDTH)
            if j == 0:
                s["k_prev"] = [kprev_scr[q, i] for i in range(4)]
                s["v_prev"] = [vprev_scr[q, i] for i in range(4)]
            else:
                s["k_prev"], s["v_prev"] = carry[q]["k_var"], carry[q]["v_var"]
            k_var = _kv_variants(k_rot, 0.0)
            v_var = _kv_variants(v_new, 1.0)
            carry[q]["k_var"], carry[q]["v_var"] = k_var, v_var
            s["k_var"], s["v_var"] = k_var, v_var
            if j == nblk - 1:
                for i in range(4):
                    kprev_scr[q, i] = k_var[i]
                    vprev_scr[q, i] = v_var[i]
                kk_ref[q] = k_rot
                vk_ref[q] = v_new
            s["qst"] = [jnp.concatenate([qcols[2 * g], qcols[2 * g + 1]], axis=0) for g in range(2)]
            s["scores"], s["probs"], s["res"] = {}, {}, {}

        def a_scores(i):
            def run():
                s_prev = _dot_nt(s["qst"][i // 2], s["k_prev"][i])
                if j == 0:
                    s_prev = jnp.where(t > 0, s_prev, NEG_INF)
                s["scores"][i] = jnp.where(from_prev, s_prev, _dot_nt(s["qst"][i // 2], s["k_var"][i]))
            return run

        def a_softmax(i):
            def run():
                g, e = i // 2, i % 2
                sc = s["scores"].pop(i)
                sink_col = jnp.where(first_rows, sinks_ref[4 * g + e], sinks_ref[4 * g + 2 + e])
                m = jnp.maximum(jnp.max(sc, axis=-1, keepdims=True), sink_col)
                p = jnp.exp(sc - m)
                p_both = jnp.concatenate([jnp.where(from_prev, p, 0.0), jnp.where(from_prev, 0.0, p)], axis=1)
                s["probs"][i] = (p_both.astype(BF16), jnp.exp(sink_col - m))
            return run

        def a_values(i):
            def run():
                p_both, es = s["probs"].pop(i)
                r = _dot(p_both, jnp.concatenate([s["v_prev"][i], s["v_var"][i]], axis=0))
                s["res"][i] = r + _sum_half(i % 2, es, 2 * BLK)
            return run

        def a_finish():
            acols = _assemble_attn([s["res"][i] for i in range(4)], BLK)
            acols = _gated_rmsnorm_cols(acols, attn_g, zc(C_GA, ATTN_WIDTH), ATTN_WIDTH)
            for c in range(4):
                mix_scr[q, slot, :, c * LANES:(c + 1) * LANES] = acols[c].astype(BF16)

        def h_prep():
            lf = _log_decay(zc(C_HF, HG_KW), log_lb, log1m_lb) * LOG2E
            kin = 1.0 - jnp.exp2(lf)
            lf_hi, lf_lo = _split_bf16(lf)
            odd = (lax.broadcasted_iota(jnp.int32, (BLK, HG_KW), 0) & 1) == 1
            m_low = mlow_ref[BLK:LOW_LEVELS * BLK, :]
            m_cum = mlow_ref[LOW_LEVELS * BLK:(LOW_LEVELS + 1) * BLK, :]
            eg = jnp.concatenate([jnp.where(odd, lf, 0.0), _dot(m_low, lf_hi),
                                  _dot(m_cum, lf_hi) + _dot(m_cum, lf_lo)], axis=0)
            s["heads"] = []
            for h in range(HG_HEADS):
                hs = slice(h * LANES, (h + 1) * LANES)
                s["heads"].append(dict(
                    q=zc(C_HQ + h * LANES, LANES), k=kin[:, hs],
                    e_low=[eg[l * BLK:(l + 1) * BLK, hs] for l in range(LOW_LEVELS)],
                    g=eg[LOW_LEVELS * BLK:(LOW_LEVELS + 1) * BLK, hs], u=[], p=[]))

        def h_operand(h, l):
            def run():
                hd = s["heads"][h]
                hd["u"].append(_level_operand(l, hd["q"], hd["k"], hd["e_low"], hd["g"]).astype(BF16))
            return run

        def h_products(h):
            def run():
                hd = s["heads"][h]
                hd["p"] = [_dot_nt(u, u) for u in hd.pop("u")]
            return run

        def h_combine(h):
            def run():
                hd = s["heads"][h]
                a = jnp.zeros((BLK, BLK), F32)
                for l, p in enumerate(hd.pop("p")):
                    a = jnp.where(lvl == l, p, a)
                diag = jnp.sum(hd["q"] * hd["k"], axis=-1, keepdims=True)
                hd["a"] = jnp.where(lvl == -1, diag, a).astype(BF16)
                g_last = hd["g"][BLK - 1:BLK, :]
                hd["qe"] = (hd["q"] * jnp.exp2(hd["g"])).astype(BF16)
                hd["kd"] = (hd["k"] * jnp.exp2(g_last - hd["g"])).astype(BF16)
                hd["dec"] = jnp.exp2(g_last)
            return run

        def h_output(h):
            def run():
                hd = s["heads"][h]
                v_h = zc(C_HI + h * LANES, LANES).astype(BF16)
                st = st_scr[q, h]
                hd["o"] = _dot(hd.pop("a"), v_h) + _dot_nt(hd.pop("qe"), st.astype(BF16))
                st_scr[q, h] = st * hd.pop("dec") + _dot_tn(v_h, hd.pop("kd"))
            return run

        def h_finish(h):
            def run():
                o_h = s["heads"][h].pop("o")
                gt = zc(C_GH + h * LANES, LANES)
                inv = lax.rsqrt(jnp.mean(o_h * o_h, axis=-1, keepdims=True) + NORM_EPS)
                mix_scr[q, slot, :, ATTN_WIDTH + h * LANES:ATTN_WIDTH + (h + 1) * LANES] = (
                    o_h * inv * hg_g[:, h * LANES:(h + 1) * LANES] * _silu(gt)).astype(BF16)
            return run

        heads, variants = range(HG_HEADS), range(4)
        return ([a_rope, h_prep]
                + _merge([a_scores(i) for i in variants],
                         [h_operand(h, l) for h in heads for l in range(PROMPT_LEVELS)])
                + _merge([a_softmax(i) for i in variants], [h_products(h) for h in heads])
                + _merge([a_values(i) for i in variants], [h_combine(h) for h in heads])
                + _merge([a_finish], [h_output(h) for h in heads])
                + [h_finish(h) for h in heads])

    def stage_mix(j):
        units = mix_units(0, j)
        for q in range(1, PROMPT_SEQS):
            units = _merge(units, mix_units(q, j))
        return units

    @pl.when(jnp.logical_and(pl.program_id(0) == 0, t == 0))
    def _():
        for q in range(PROMPT_SEQS):
            for th in stage_in(q, 0):
                th()

    def both(stage, j):
        lists = [stage(q, j) for q in range(PROMPT_SEQS)]
        return [th for group in zip(*lists) for th in group]

    for j in range(nblk):
        nxt = both(stage_in, j // IN_GROUP + 1)
        part = j % IN_GROUP
        per = -(-len(nxt) // IN_GROUP)
        side = nxt[part * per:(part + 1) * per]
        if j >= 1:
            side = side + both(stage_out, j - 1)
        _interleave(stage_mix(j), side)
    for th in both(stage_out, nblk - 1):
        th()

    @pl.when(t == pl.num_programs(1) - 1)
    def _():
        for q in range(PROMPT_SEQS):
            for h in range(HG_HEADS):
                sfin_ref[q, h] = st_scr[q, h].T


def _sample_kernel(sinks_ref, x_ref, p_ref, ck_ref, cv_ref, s0_ref, cq_ref, sq_ref, ckt_ref, skt_ref,
                   lvl_ref, mlow_ref, seg_ref, *refs):
    w_in, w_tail, w_out, w_pg, w_pp, rest = _split_refs(refs)
    v512_ref, v1024_ref, y_ref, nk_ref, nv_ref, ns_ref, y_scr, qm_scr, attn_scr, mix_scr = rest
    layer = pl.program_id(0)
    blk = pl.program_id(1)
    tk = DEC_TOKENS
    rows = SAMPLE_SEQS * tk
    yrows = pl.ds(pl.multiple_of(blk * rows, rows), rows)

    @pl.when(layer == 0)
    def _():
        y_scr[yrows, :] = x_ref[...].reshape(rows, D_MODEL)

    x = y_scr[yrows, :]
    xb = x.astype(BF16)
    z_all = jnp.concatenate([_chunked_dot(xb, w_in), _dot(xb, w_tail[...])], axis=1)

    def z(c0, width):
        return z_all[:, c0:c0 + width]
    lvl = lvl_ref[...]
    attn_g = v512_ref[0:1, :]
    log_lb = v512_ref[1:2, :]
    log1m_lb = v512_ref[2:3, :]
    hg_g = v512_ref[3:4, :]

    qcols = [_rope(z(C_Q + c * LANES, LANES), cq_ref[...], sq_ref[...]) for c in range(4)]
    k_rot = _rope(z(C_K, KV_WIDTH), ckt_ref[...], skt_ref[...])
    v_new = z(C_V, KV_WIDTH)
    k_new_b = k_rot.astype(BF16)
    v_new_b = v_new.astype(BF16)
    lo2 = _lane_iota(2 * rows) < HEAD_DIM
    for g in range(2):
        xg = jnp.concatenate([qcols[2 * g], qcols[2 * g + 1]], axis=0)
        xsw = pltpu.roll(xg, HEAD_DIM, 1)
        keep = lo2 if g == 0 else jnp.logical_not(lo2)
        for e in range(2):
            qm = jnp.where(keep, xg if e == g else xsw, 0.0)
            for cc in range(2):
                r0 = ((g * 2 + e) * 2 + cc) * tk
                for s in range(SAMPLE_SEQS):
                    qm_scr[s, r0:r0 + tk, :] = qm[cc * rows + s * tk:cc * rows + (s + 1) * tk, :]

    lf = _log_decay(z(C_HF, HG_KW), log_lb, log1m_lb) * LOG2E
    kin = 1.0 - jnp.exp2(lf)
    lf_hi, lf_lo = _split_bf16(lf)
    eg = _dot(mlow_ref[...], lf_hi) + _dot(mlow_ref[...], lf_lo)
    dec_t = jnp.exp2(_dot_tn(lf_hi, seg_ref[...]) + _dot_tn(lf_lo, seg_ref[...]))
    o_intra, qe, kd, vh = [], [], [], []
    for h in range(HG_HEADS):
        hs = slice(h * LANES, (h + 1) * LANES)
        q_h = z(C_HQ + h * LANES, LANES)
        k_h = kin[:, hs]
        v_h = z(C_HI + h * LANES, LANES)
        e_low = [eg[l * rows:(l + 1) * rows, hs] for l in range(LOW_LEVELS)]
        g_h = eg[LOW_LEVELS * rows:(LOW_LEVELS + 1) * rows, hs]
        tot_h = eg[(LOW_LEVELS + 1) * rows:(LOW_LEVELS + 2) * rows, hs]
        a = _hgrn_scores(q_h, k_h, e_low, g_h, lvl, LOW_LEVELS)
        o_intra.append(_dot(a.astype(BF16), v_h.astype(BF16)))
        qe.append(q_h * jnp.exp2(g_h))
        kd.append(k_h * jnp.exp2(tot_h - g_h))
        vh.append(v_h)

    grp = SAMPLE_GROUP
    grows = grp * QM_ROWS
    rid = lax.broadcasted_iota(jnp.int32, (grows, 1), 0)
    r8 = (rid >> TOK_BITS) & (ATTN_HEADS - 1)
    sink_col = jnp.zeros((grows, 1), F32)
    for idx in range(ATTN_HEADS):
        g, e, cc = idx // 4, (idx // 2) % 2, idx % 2
        sink_col = jnp.where(r8 == idx, sinks_ref[layer * ATTN_HEADS + 4 * g + 2 * cc + e], sink_col)
    row = lax.broadcasted_iota(jnp.int32, (grows, 2 * WINDOW), 0)
    col = lax.broadcasted_iota(jnp.int32, (grows, 2 * WINDOW), 1)
    tok = row & (tk - 1)
    new_col = col - WINDOW
    mask_cache = jnp.logical_and(col < WINDOW, col > tok)
    lo8 = _lane_iota(tk) < HEAD_DIM
    o_inter = [[None] * SAMPLE_SEQS for _ in range(HG_HEADS)]
    for gi in range(SAMPLE_SEQS // grp):
        seqs = range(gi * grp, (gi + 1) * grp)
        kcs = {s: ck_ref[s] for s in seqs}
        vcs = {s: cv_ref[s] for s in seqs}
        sc = [_dot_nt(qm_scr[s].astype(BF16),
                      jnp.concatenate([kcs[s].astype(BF16), k_new_b], axis=0)) for s in seqs]
        for s in seqs:
            nk_ref[s, 0:WINDOW - tk, :] = kcs[s][tk:WINDOW, :]
            nk_ref[s, WINDOW - tk:WINDOW, :] = k_rot[s * tk:(s + 1) * tk, :]
            nv_ref[s, 0:WINDOW - tk, :] = vcs[s][tk:WINDOW, :]
            nv_ref[s, WINDOW - tk:WINDOW, :] = v_new[s * tk:(s + 1) * tk, :]
        same_seq = (new_col >> TOK_BITS) == (row >> QM_BITS) + gi * grp
        mask_new = jnp.logical_and(jnp.logical_and(col >= WINDOW, same_seq), (new_col & (tk - 1)) <= tok)
        s_all = jnp.where(jnp.logical_or(mask_cache, mask_new), jnp.concatenate(sc, axis=0), NEG_INF)
        m = jnp.maximum(jnp.max(s_all, axis=-1, keepdims=True), sink_col)
        p = jnp.exp(s_all - m)
        den = jnp.sum(p, axis=-1, keepdims=True) + jnp.exp(sink_col - m)
        pb = p.astype(BF16)
        o_all = jnp.concatenate(
            [_dot(pb[i * QM_ROWS:(i + 1) * QM_ROWS, :],
                  jnp.concatenate([vcs[s].astype(BF16), v_new_b], axis=0))
             for i, s in enumerate(seqs)], axis=0) / den
        o_sw = pltpu.roll(o_all, HEAD_DIM, 1)
        for i, s in enumerate(seqs):
            for c in range(4):
                g, cc = c // 2, c % 2
                ra = i * QM_ROWS + ((g * 2 + 0) * 2 + cc) * tk
                rb = i * QM_ROWS + ((g * 2 + 1) * 2 + cc) * tk
                part0 = (o_all if g == 0 else o_sw)[ra:ra + tk, :]
                part1 = (o_all if g == 1 else o_sw)[rb:rb + tk, :]
                attn_scr[c, s * tk:(s + 1) * tk, :] = jnp.where(lo8, part0, part1)
        s0s = {(s, h): s0_ref[s, h] for s in seqs for h in range(HG_HEADS)}
        for s in seqs:
            for h in range(HG_HEADS):
                o_inter[h][s] = _dot(qe[h][s * tk:(s + 1) * tk, :].astype(BF16), s0s[(s, h)].astype(BF16))
        upd = {(s, h): _dot_tn(kd[h][s * tk:(s + 1) * tk, :].astype(BF16), vh[h][s * tk:(s + 1) * tk, :].astype(BF16))
               for s in seqs for h in range(HG_HEADS)}
        for s in seqs:
            for h in range(HG_HEADS):
                dcol = dec_t[h * LANES:(h + 1) * LANES, s * tk:s * tk + 1]
                ns_ref[s, h] = dcol * s0s[(s, h)] + upd[(s, h)]

    acols = _gated_rmsnorm_cols([attn_scr[c] for c in range(4)], attn_g, z(C_GA, ATTN_WIDTH),
                                ATTN_WIDTH)
    for c in range(4):
        mix_scr[:, c * LANES:(c + 1) * LANES] = acols[c].astype(BF16)
    for h in range(HG_HEADS):
        hs = slice(h * LANES, (h + 1) * LANES)
        o_h = o_intra[h] + jnp.concatenate(o_inter[h], axis=0)
        gt = z(C_GH + h * LANES, LANES)
        inv = lax.rsqrt(jnp.mean(o_h * o_h, axis=-1, keepdims=True) + NORM_EPS)
        mix_scr[:, ATTN_WIDTH + h * LANES:ATTN_WIDTH + (h + 1) * LANES] = (
            o_h * inv * hg_g[:, hs] * _silu(gt)).astype(BF16)

    hpre = DN_ALPHA * x + _chunked_dot(mix_scr[...], w_out)
    hn = _layernorm(hpre, v1024_ref[0:1, :], v1024_ref[1:2, :])
    gate = _sigmoid(_chunked_dot(hn.astype(BF16), w_pg))
    pp = _chunked_dot(p_ref[...].reshape(rows, PLE_DIM).astype(BF16), w_pp)
    y = hn + gate * pp
    y_scr[yrows, :] = y
    y_ref[...] = y.reshape(SAMPLE_SEQS, tk, D_MODEL)


def _level_matrix():
    t = np.arange(BLK)[:, None]
    s = np.arange(BLK)[None, :]
    x = t ^ s
    lv = np.floor(np.log2(np.maximum(x, 1))).astype(np.int32)
    return np.where(t > s, lv, np.where(t == s, -1, -2)).astype(np.int32)


def _level_exponent_matrix(l):
    m = np.zeros((BLK, BLK), np.float32)
    b, h = 2 << l, 1 << l
    for t in range(BLK):
        mid = t - t % b + h
        if t >= mid:
            m[t, mid:t + 1] = 1.0
        else:
            m[t, t + 1:mid] = 1.0
    return m


def _mask_matrices(seq_rows):
    blocks = [_level_exponent_matrix(l) for l in range(LOW_LEVELS)]
    t = np.arange(BLK)[:, None]
    s = np.arange(BLK)[None, :]
    same = (t // seq_rows) == (s // seq_rows)
    blocks.append((same & (s <= t)).astype(np.float32))
    if seq_rows < BLK:
        blocks.append(same.astype(np.float32))
    return np.concatenate(blocks, axis=0)


def _rope_tables(pos, scale):
    half = HEAD_DIM // 2
    inv = jnp.exp(-math.log(ROPE_THETA) * jnp.arange(half, dtype=F32) * 2.0 / HEAD_DIM)
    ang = pos.astype(F32)[:, None] * inv[None, :]
    cos = jnp.cos(ang) * scale
    sin = jnp.sin(ang) * scale
    return jnp.tile(cos, (1, 4)), jnp.concatenate([-sin, sin, -sin, sin], axis=1)


def _weight_specs(layer_of):
    def window(rows, width, c):
        return pl.BlockSpec((None, rows, width), lambda *g, c=c: (layer_of(*g), 0, c),
                            pipeline_mode=pl.Buffered(1))
    specs = [window(D_MODEL, CHUNK_COLS, c) for c in range(IN_CHUNKS)]
    specs.append(window(D_MODEL, IN_TAIL, IN_CHUNKS * CHUNK_COLS // IN_TAIL))
    for rows in (MIX_WIDTH, D_MODEL, PLE_DIM):
        specs += [window(rows, CHUNK_COLS, c) for c in range(OUT_CHUNKS)]
    return specs


def _weight_args(weights):
    w_in, w_out, w_pg, w_pp = weights
    return [w_in] * (IN_CHUNKS + 1) + [w_out] * OUT_CHUNKS + [w_pg] * OUT_CHUNKS + [w_pp] * OUT_CHUNKS


def _const_spec(shape, layer=None):
    if layer is None:
        return pl.BlockSpec(shape, lambda *_: (0,) * len(shape))
    return pl.BlockSpec((None,) + shape, lambda *_: (layer,) + (0,) * len(shape))


def _prompt_layer(i, x, p_all, tables, lvl, mlow, weights, sinks, v512, v1024):
    B, T, _ = x.shape
    nt = T // PROMPT_TILE
    nb = B // PROMPT_SEQS
    tab_spec = pl.BlockSpec((PROMPT_TILE, LANES), lambda b, t: (t, 0))
    nblk = PROMPT_TILE // BLK
    assert nblk % IN_GROUP == 0 and nblk % 2 == 0

    def next_first_group(b, t):
        flat = jnp.minimum(b * nt + t + 1, nb * nt - 1)
        return (flat // nt, (flat % nt) * (nblk // IN_GROUP), 0)
    in_specs = [
        pl.BlockSpec(memory_space=pltpu.SMEM),
        pl.BlockSpec((PROMPT_SEQS, PROMPT_TILE, D_MODEL), lambda b, t: (b, t, 0)),
        pl.BlockSpec((PROMPT_SEQS, IN_GROUP * BLK, D_MODEL), next_first_group),
        pl.BlockSpec((None, PROMPT_SEQS, PROMPT_TILE, PLE_DIM), lambda b, t: (i, b, t, 0)),
        tab_spec, tab_spec, tab_spec, tab_spec,
        _const_spec((BLK, BLK)),
        _const_spec(mlow.shape),
    ] + _weight_specs(lambda b, t: i) + [
        _const_spec((4, ATTN_WIDTH), i),
        _const_spec((2, D_MODEL), i),
    ]
    out_shape = [
        jax.ShapeDtypeStruct((B, T, D_MODEL), F32),
        jax.ShapeDtypeStruct((B, WINDOW, KV_WIDTH), F32),
        jax.ShapeDtypeStruct((B, WINDOW, KV_WIDTH), F32),
        jax.ShapeDtypeStruct((B, HG_HEADS, HG_DK, HG_DV), F32),
    ]
    out_specs = [
        pl.BlockSpec((PROMPT_SEQS, PROMPT_TILE, D_MODEL), lambda b, t: (b, t, 0)),
        pl.BlockSpec((PROMPT_SEQS, WINDOW, KV_WIDTH), lambda b, t: (b, 0, 0)),
        pl.BlockSpec((PROMPT_SEQS, WINDOW, KV_WIDTH), lambda b, t: (b, 0, 0)),
        pl.BlockSpec((PROMPT_SEQS, HG_HEADS, HG_DK, HG_DV), lambda b, t: (b, 0, 0, 0)),
    ]
    scratch = [
        pltpu.VMEM((PROMPT_SEQS, HG_HEADS, HG_DV, HG_DK), F32),
        pltpu.VMEM((PROMPT_SEQS, 4, BLK, LANES), BF16),
        pltpu.VMEM((PROMPT_SEQS, 4, BLK, LANES), BF16),
        pltpu.VMEM((PROMPT_SEQS, nblk, BLK, IN_COLS), F32),
        pltpu.VMEM((PROMPT_SEQS, 2, BLK, MIX_WIDTH), BF16),
    ]
    return pl.pallas_call(
        _prompt_kernel,
        grid=(nb, nt),
        in_specs=in_specs,
        out_specs=out_specs,
        out_shape=out_shape,
        scratch_shapes=scratch,
        compiler_params=pltpu.CompilerParams(
            dimension_semantics=("arbitrary", "arbitrary"), vmem_limit_bytes=VMEM_LIMIT),
        name=f"prompt_layer{i}",
    )(sinks, x, x, p_all, *tables, lvl, mlow, *_weight_args(weights), v512, v1024)


def _sample_call(x, p_all, ck_all, cv_all, s0_all, tables, lvl, mlow, seg, weights, sinks, v512, v1024):
    B, T, _ = x.shape
    rows = SAMPLE_SEQS * T
    tab_spec = _const_spec((rows, LANES))
    in_specs = [
        pl.BlockSpec(memory_space=pltpu.SMEM),
        pl.BlockSpec((SAMPLE_SEQS, T, D_MODEL), lambda l, b: (jnp.where(l == 0, b, 0), 0, 0)),
        pl.BlockSpec((None, SAMPLE_SEQS, T, PLE_DIM), lambda l, b: (l, b, 0, 0)),
        pl.BlockSpec((None, SAMPLE_SEQS, WINDOW, KV_WIDTH), lambda l, b: (l, b, 0, 0)),
        pl.BlockSpec((None, SAMPLE_SEQS, WINDOW, KV_WIDTH), lambda l, b: (l, b, 0, 0)),
        pl.BlockSpec((None, SAMPLE_SEQS, HG_HEADS, HG_DK, HG_DV), lambda l, b: (l, b, 0, 0, 0)),
        tab_spec, tab_spec, tab_spec, tab_spec,
        _const_spec((BLK, BLK)),
        _const_spec(mlow.shape),
        _const_spec((BLK, LANES)),
    ] + _weight_specs(lambda l, b: l) + [
        pl.BlockSpec((None, 4, ATTN_WIDTH), lambda l, b: (l, 0, 0)),
        pl.BlockSpec((None, 2, D_MODEL), lambda l, b: (l, 0, 0)),
    ]
    out_shape = [
        jax.ShapeDtypeStruct((B, T, D_MODEL), F32),
        jax.ShapeDtypeStruct((DEPTH, B, WINDOW, KV_WIDTH), F32),
        jax.ShapeDtypeStruct((DEPTH, B, WINDOW, KV_WIDTH), F32),
        jax.ShapeDtypeStruct((DEPTH, B, HG_HEADS, HG_DK, HG_DV), F32),
    ]
    out_specs = [
        pl.BlockSpec((SAMPLE_SEQS, T, D_MODEL), lambda l, b: (jnp.where(l == DEPTH - 1, b, 0), 0, 0)),
        pl.BlockSpec((None, SAMPLE_SEQS, WINDOW, KV_WIDTH), lambda l, b: (l, b, 0, 0)),
        pl.BlockSpec((None, SAMPLE_SEQS, WINDOW, KV_WIDTH), lambda l, b: (l, b, 0, 0)),
        pl.BlockSpec((None, SAMPLE_SEQS, HG_HEADS, HG_DK, HG_DV), lambda l, b: (l, b, 0, 0, 0)),
    ]
    scratch = [
        pltpu.VMEM((B * T, D_MODEL), F32),
        pltpu.VMEM((SAMPLE_SEQS, QM_ROWS, LANES), F32),
        pltpu.VMEM((4, rows, LANES), F32),
        pltpu.VMEM((rows, MIX_WIDTH), BF16),
    ]
    return pl.pallas_call(
        _sample_kernel,
        grid=(DEPTH, B // SAMPLE_SEQS),
        in_specs=in_specs,
        out_specs=out_specs,
        out_shape=out_shape,
        scratch_shapes=scratch,
        compiler_params=pltpu.CompilerParams(
            dimension_semantics=("arbitrary", "arbitrary"), vmem_limit_bytes=VMEM_LIMIT),
        name="sample_layers",
    )(sinks, x, p_all, ck_all, cv_all, s0_all, *tables, lvl, mlow, seg, *_weight_args(weights), v512, v1024)


def kernel(x_prompt, x_sample, cache_k_win, cache_v_win, state_hgrn, p_prompt, p_sample, w_in, attn_sinks,
           attn_norm_g, hg_lb_logits, hg_norm_g, w_out, ln_g, ln_b, w_ple_proj, w_ple_gate):
    B, T, _ = x_prompt.shape
    SB, ST, _ = x_sample.shape
    assert T % PROMPT_TILE == 0 and B % PROMPT_SEQS == 0
    assert SB % SAMPLE_SEQS == 0 and ST == DEC_TOKENS and SAMPLE_SEQS * ST == BLK
    assert 1 << TOK_BITS == DEC_TOKENS and 1 << QM_BITS == QM_ROWS
    assert cache_k_win.shape[2] == WINDOW

    cs = jnp.cumsum(jax.nn.softmax(hg_lb_logits.astype(F32), axis=0), axis=0)
    lbs = cs - cs[:1]
    v512 = jnp.stack([attn_norm_g.astype(F32), jnp.log(lbs), jnp.log1p(-lbs), hg_norm_g.astype(F32)], axis=1)
    v1024 = jnp.stack([ln_g.astype(F32), ln_b.astype(F32)], axis=1)
    weights = tuple(w.astype(BF16) for w in (w_in, w_out, w_ple_gate, w_ple_proj))
    sinks = attn_sinks.astype(F32)

    scale = HEAD_DIM ** -0.5
    pos_p = jnp.arange(T, dtype=jnp.int32)
    pos_s = jnp.tile(PAST_LEN + jnp.arange(ST, dtype=jnp.int32), SAMPLE_SEQS)
    tab_p = _rope_tables(pos_p, scale) + _rope_tables(pos_p, 1.0)
    tab_s = _rope_tables(pos_s, scale) + _rope_tables(pos_s, 1.0)
    lvl = jnp.asarray(_level_matrix())
    mlow_p = jnp.asarray(_mask_matrices(BLK), dtype=BF16)
    mlow_s = jnp.asarray(_mask_matrices(ST), dtype=BF16)
    seg = jnp.asarray((np.arange(BLK)[:, None] // ST) == (np.arange(LANES)[None, :] // ST), dtype=BF16)

    ck = cache_k_win.reshape(DEPTH, SB, WINDOW, KV_WIDTH)
    cv = cache_v_win.reshape(DEPTH, SB, WINDOW, KV_WIDTH)
    ys, ks, vs, ss = _sample_call(x_sample, p_sample, ck, cv, state_hgrn, tab_s, lvl, mlow_s, seg, weights,
                                  sinks.reshape(DEPTH * ATTN_HEADS), v512, v1024)

    yp = x_prompt
    kp_l, vp_l, sp_l = [], [], []
    for i in range(DEPTH):
        yp, kp, vp, sp = _prompt_layer(i, yp, p_prompt, tab_p, lvl, mlow_p, weights, sinks[i], v512, v1024)
        kp_l.append(kp); vp_l.append(vp); sp_l.append(sp)

    def kv5(a, b):
        return a.reshape(DEPTH, b, WINDOW, KV_HEADS, HEAD_DIM)

    return (yp, ys, kv5(jnp.stack(kp_l), B), kv5(jnp.stack(vp_l), B), jnp.stack(sp_l),
            kv5(ks, SB), kv5(vs, SB), ss)
```

```python
import math

import numpy as np
import jax
import jax.numpy as jnp
from jax import lax
from jax.experimental import pallas as pl
from jax.experimental.pallas import tpu as pltpu

D_MODEL = 1024
DEPTH = 4
PAST_LEN = 8192
ATTN_HEADS = 8
KV_HEADS = 2
HEAD_DIM = 64
ATTN_WIDTH = ATTN_HEADS * HEAD_DIM
KV_WIDTH = KV_HEADS * HEAD_DIM
WINDOW = 128
ROPE_THETA = 10000.0
HG_HEADS = 4
HG_DK = 128
HG_DV = 128
HG_KW = HG_HEADS * HG_DK
HG_VW = HG_HEADS * HG_DV
MIX_WIDTH = ATTN_WIDTH + HG_VW
IN_COLS = 2 * ATTN_WIDTH + 2 * KV_WIDTH + 2 * HG_KW + 2 * HG_VW
PLE_DIM = 256
DN_ALPHA = (2 * DEPTH) ** 0.25
NORM_EPS = 1e-5
NEG_INF = -1e30
LOG2E = 1.4426950408889634

C_Q = 0
C_K = C_Q + ATTN_WIDTH
C_V = C_K + KV_WIDTH
C_GA = C_V + KV_WIDTH
C_HQ = C_GA + ATTN_WIDTH
C_HF = C_HQ + HG_KW
C_HI = C_HF + HG_KW
C_GH = C_HI + HG_VW

HALF_DIM = HEAD_DIM // 2
DEC_TOKENS = 8
TOK_BITS = 3
QM_ROWS = ATTN_HEADS * DEC_TOKENS
QM_BITS = 6

LANES = 128
MXU_COLS = 256
CHUNK_COLS = 2 * MXU_COLS
BLK = 128
PROMPT_TILE = 512
PROMPT_SEQS = 1
IN_GROUP = 2
SAMPLE_SEQS = 16
SAMPLE_GROUP = 8
LOW_LEVELS = 3
PROMPT_LEVELS = 7
VMEM_LIMIT = 56 * 1024 * 1024
IN_CHUNKS = IN_COLS // CHUNK_COLS
IN_TAIL = IN_COLS - IN_CHUNKS * CHUNK_COLS
OUT_CHUNKS = D_MODEL // CHUNK_COLS
N_WEIGHT_REFS = IN_CHUNKS + 1 + 3 * OUT_CHUNKS

F32 = jnp.float32
BF16 = jnp.bfloat16


def _dot(a, b):
    return jnp.dot(a, b, preferred_element_type=F32)


def _dot_nt(a, b):
    return lax.dot_general(a, b, (((1,), (1,)), ((), ())), preferred_element_type=F32)


def _dot_tn(a, b):
    return lax.dot_general(a, b, (((0,), (0,)), ((), ())), preferred_element_type=F32)


def _split_bf16(x):
    hi = x.astype(BF16)
    lo = (x - hi.astype(F32)).astype(BF16)
    return hi, lo


def _sigmoid(x):
    return 0.5 * jnp.tanh(0.5 * x) + 0.5


def _silu(x):
    h = 0.5 * x
    return h * jnp.tanh(h) + h


def _lane_iota(rows):
    return lax.broadcasted_iota(jnp.int32, (rows, LANES), 1)


def _rope(x, cos_t, sin_t):
    lane = _lane_iota(x.shape[0])
    first_half = (lane & HALF_DIM) == 0
    swapped = jnp.where(first_half, pltpu.roll(x, LANES - HALF_DIM, 1), pltpu.roll(x, HALF_DIM, 1))
    return x * cos_t + swapped * sin_t


def _kv_variants(a, fill):
    lane = _lane_iota(a.shape[0])
    lo = lane < HEAD_DIM
    sw = pltpu.roll(a, HEAD_DIM, 1)
    f = jnp.full_like(a, fill)
    out = [jnp.where(lo, a, f), jnp.where(lo, f, sw), jnp.where(lo, sw, f), jnp.where(lo, f, a)]
    return [o.astype(BF16) for o in out]


def _sum_half(e, es, rows):
    lane = _lane_iota(rows)
    sum_half = (lane >= HEAD_DIM) if e == 0 else (lane < HEAD_DIM)
    return jnp.where(sum_half, es, 0.0)


def _assemble_attn(res, rows):
    lane = _lane_iota(rows)
    lo = lane < HEAD_DIM
    cols = []
    for c in range(4):
        g, cc = c // 2, c % 2
        r0 = res[g * 2 + 0][cc * rows:(cc + 1) * rows]
        r1 = res[g * 2 + 1][cc * rows:(cc + 1) * rows]
        num = jnp.where(lo, r0, r1)
        den = pltpu.roll(jnp.where(lo, r1, r0), HEAD_DIM, 1)
        cols.append(num / den)
    return cols


def _log_decay(hf, log_lb, log1m_lb):
    ls = jnp.minimum(hf, 0.0) - jnp.log(1.0 + jnp.exp(-jnp.abs(hf)))
    b = log1m_lb + ls
    return jnp.maximum(log_lb, b) + jnp.log(1.0 + jnp.exp(-jnp.abs(log_lb - b)))


def _level_operand(l, q_h, k_h, e_low, g_h):
    if l < LOW_LEVELS:
        rows = lax.broadcasted_iota(jnp.int32, (BLK, LANES), 0)
        upper = ((rows >> l) & 1) == 1
        return jnp.where(upper, q_h, k_h) * jnp.exp2(e_low[l])
    b, h = 2 << l, 1 << l
    pieces = []
    for i in range(BLK // b):
        lower, upper = slice(i * b, i * b + h), slice(i * b + h, (i + 1) * b)
        g_mid = g_h[i * b + h - 1:i * b + h, :]
        pieces.append(k_h[lower, :] * jnp.exp2(g_mid - g_h[lower, :]))
        pieces.append(q_h[upper, :] * jnp.exp2(g_h[upper, :] - g_mid))
    return jnp.concatenate(pieces, axis=0)


def _hgrn_scores(q_h, k_h, e_low, g_h, lvl, levels):
    a = jnp.zeros((BLK, BLK), F32)
    for l in range(levels):
        u = _level_operand(l, q_h, k_h, e_low, g_h).astype(BF16)
        a = jnp.where(lvl == l, _dot_nt(u, u), a)
    diag = jnp.sum(q_h * k_h, axis=-1, keepdims=True)
    return jnp.where(lvl == -1, diag, a)


def _gated_rmsnorm_cols(cols, gain_row, gate, width):
    ss = None
    for c in cols:
        s = jnp.sum(c * c, axis=-1, keepdims=True)
        ss = s if ss is None else ss + s
    inv = lax.rsqrt(ss * (1.0 / width) + NORM_EPS)
    out = []
    for i, c in enumerate(cols):
        gt = gate[:, i * LANES:(i + 1) * LANES]
        out.append(c * inv * gain_row[:, i * LANES:(i + 1) * LANES] * _silu(gt))
    return out


def _layernorm(hpre, ln_g, ln_b):
    mu = jnp.mean(hpre, axis=-1, keepdims=True)
    cen = hpre - mu
    var = jnp.mean(cen * cen, axis=-1, keepdims=True)
    return cen * lax.rsqrt(var + NORM_EPS) * ln_g + ln_b


def _chunked_dot(a, w_refs):
    return jnp.concatenate([_dot(a, w[...]) for w in w_refs], axis=1)


def _split_refs(refs):
    n = IN_CHUNKS + 1
    return (refs[:IN_CHUNKS], refs[IN_CHUNKS], refs[n:n + OUT_CHUNKS], refs[n + OUT_CHUNKS:n + 2 * OUT_CHUNKS],
            refs[n + 2 * OUT_CHUNKS:N_WEIGHT_REFS], refs[N_WEIGHT_REFS:])


def _merge(a, b):
    out, ia, ib = [], 0, 0
    while ia < len(a) or ib < len(b):
        if ib >= len(b) or (ia < len(a) and ia * len(b) <= ib * len(a)):
            out.append(a[ia])
            ia += 1
        else:
            out.append(b[ib])
            ib += 1
    return out


def _interleave(units, thunks):
    thunks = list(thunks)
    total, done = len(thunks), 0
    for seen, fn in enumerate(units, 1):
        fn()
        want = min(total, -(-seen * total // len(units)))
        while done < want:
            thunks[done]()
            done += 1


def _prompt_kernel(sinks_ref, x_ref, xn_ref, p_ref, cq_ref, sq_ref, ck_ref, sk_ref, lvl_ref, mlow_ref, *refs):
    w_in, w_tail, w_out, w_pg, w_pp, rest = _split_refs(refs)
    (v512_ref, v1024_ref, y_ref, kk_ref, vk_ref, sfin_ref,
     st_scr, kprev_scr, vprev_scr, z_scr, mix_scr) = rest
    t = pl.program_id(1)
    nblk = PROMPT_TILE // BLK

    @pl.when(t == 0)
    def _():
        st_scr[...] = jnp.zeros_like(st_scr)
        kprev_scr[...] = jnp.zeros_like(kprev_scr)
        vprev_scr[...] = jnp.zeros_like(vprev_scr)

    lvl = lvl_ref[...]
    attn_g = v512_ref[0:1, :]
    log_lb = v512_ref[1:2, :]
    log1m_lb = v512_ref[2:3, :]
    hg_g = v512_ref[3:4, :]

    row = lax.broadcasted_iota(jnp.int32, (2 * BLK, BLK), 0)
    col = lax.broadcasted_iota(jnp.int32, (2 * BLK, BLK), 1)
    from_prev = col > (row & (BLK - 1))
    first_rows = lax.broadcasted_iota(jnp.int32, (2 * BLK, 1), 0) < BLK
    carry = [{} for _ in range(PROMPT_SEQS)]

    ngrp = nblk // IN_GROUP
    grp_rows = IN_GROUP * BLK

    def stage_in(q, g):
        cell = {}
        first = (g % ngrp) * IN_GROUP

        def chunk(c):
            def run():
                if "xb" not in cell:
                    src = xn_ref[q] if g == ngrp else x_ref[q, g * grp_rows:(g + 1) * grp_rows, :]
                    cell["xb"] = src.astype(BF16)
                if c < IN_CHUNKS:
                    cols, zz = slice(c * CHUNK_COLS, (c + 1) * CHUNK_COLS), _dot(cell["xb"], w_in[c][...])
                else:
                    cols, zz = slice(IN_CHUNKS * CHUNK_COLS, IN_COLS), _dot(cell["xb"], w_tail[...])
                for i in range(IN_GROUP):
                    z_scr[q, first + i, :, cols] = zz[i * BLK:(i + 1) * BLK, :]
            return run
        return [chunk(c) for c in range(IN_CHUNKS + 1)]

    def stage_out(q, j):
        slot, rows_j, cell = j % 2, slice(j * BLK, (j + 1) * BLK), {"m": [], "g": []}

        def mix_chunk(c):
            def run():
                cell["m"].append(_dot(mix_scr[q, slot], w_out[c][...]))
                if c + 1 == OUT_CHUNKS:
                    hpre = DN_ALPHA * x_ref[q, rows_j, :] + jnp.concatenate(cell.pop("m"), axis=1)
                    h = _layernorm(hpre, v1024_ref[0:1, :], v1024_ref[1:2, :])
                    cell["h"] = h
                    cell["hb"] = h.astype(BF16)
            return run

        def gate_chunk(c):
            def run():
                cell["g"].append(_sigmoid(_dot(cell["hb"], w_pg[c][...])))
            return run

        def finish():
            pb = p_ref[q, rows_j, :].astype(BF16)
            for c in range(OUT_CHUNKS):
                cs = slice(c * CHUNK_COLS, (c + 1) * CHUNK_COLS)
                y_ref[q, rows_j, cs] = cell["h"][:, cs] + cell["g"][c] * _dot(pb, w_pp[c][...])
        return ([mix_chunk(c) for c in range(OUT_CHUNKS)] + [gate_chunk(c) for c in range(OUT_CHUNKS)]
                + [finish])

    def mix_units(q, j):
        slot, rows_j, s = j % 2, slice(j * BLK, (j + 1) * BLK), {}

        def zc(c0, width):
            return z_scr[q, j, :, c0:c0 + width]

        def a_rope():
            qcols = [_rope(zc(C_Q + c * LANES, LANES), cq_ref[rows_j, :], sq_ref[rows_j, :]).astype(BF16)
                     for c in range(4)]
            k_rot = _rope(zc(C_K, KV_WIDTH), ck_ref[rows_j, :], sk_ref[rows_j, :])
            v_new = zc(C_V, KV_WIDTH)
            if j == 0:
                s["k_prev"] = [kprev_scr[q, i] for i in range(4)]
                s["v_prev"] = [vprev_scr[q, i] for i in range(4)]
            else:
                s["k_prev"], s["v_prev"] = carry[q]["k_var"], carry[q]["v_var"]
            k_var = _kv_variants(k_rot, 0.0)
            v_var = _kv_variants(v_new, 1.0)
            carry[q]["k_var"], carry[q]["v_var"] = k_var, v_var
            s["k_var"], s["v_var"] = k_var, v_var
            if j == nblk - 1:
                for i in range(4):
                    kprev_scr[q, i] = k_var[i]
                    vprev_scr[q, i] = v_var[i]
                kk_ref[q] = k_rot
                vk_ref[q] = v_new
            s["qst"] = [jnp.concatenate([qcols[2 * g], qcols[2 * g + 1]], axis=0) for g in range(2)]
            s["scores"], s["probs"], s["res"] = {}, {}, {}

        def a_scores(i):
            def run():
                s_prev = _dot_nt(s["qst"][i // 2], s["k_prev"][i])
                if j == 0:
                    s_prev = jnp.where(t > 0, s_prev, NEG_INF)
                s["scores"][i] = jnp.where(from_prev, s_prev, _dot_nt(s["qst"][i // 2], s["k_var"][i]))
            return run

        def a_softmax(i):
            def run():
                g, e = i // 2, i % 2
                sc = s["scores"].pop(i)
                sink_col = jnp.where(first_rows, sinks_ref[4 * g + e], sinks_ref[4 * g + 2 + e])
                m = jnp.maximum(jnp.max(sc, axis=-1, keepdims=True), sink_col)
                p = jnp.exp(sc - m)
                p_both = jnp.concatenate([jnp.where(from_prev, p, 0.0), jnp.where(from_prev, 0.0, p)], axis=1)
                s["probs"][i] = (p_both.astype(BF16), jnp.exp(sink_col - m))
            return run

        def a_values(i):
            def run():
                p_both, es = s["probs"].pop(i)
                r = _dot(p_both, jnp.concatenate([s["v_prev"][i], s["v_var"][i]], axis=0))
                s["res"][i] = r + _sum_half(i % 2, es, 2 * BLK)
            return run

        def a_finish():
            acols = _assemble_attn([s["res"][i] for i in range(4)], BLK)
            acols = _gated_rmsnorm_cols(acols, attn_g, zc(C_GA, ATTN_WIDTH), ATTN_WIDTH)
            for c in range(4):
                mix_scr[q, slot, :, c * LANES:(c + 1) * LANES] = acols[c].astype(BF16)

        def h_prep():
            lf = _log_decay(zc(C_HF, HG_KW), log_lb, log1m_lb) * LOG2E
            kin = 1.0 - jnp.exp2(lf)
            lf_hi, lf_lo = _split_bf16(lf)
            rows = lax.broadcasted_iota(jnp.int32, (BLK, HG_KW), 0)
            r4 = rows & 3
            nxt, prv = pltpu.roll(lf, BLK - 1, 0), pltpu.roll(lf, 1, 0)
            e1 = jnp.where(r4 == 0, nxt, jnp.where(r4 == 1, 0.0, jnp.where(r4 == 2, lf, lf + prv)))
            m_low = mlow_ref[2 * BLK:LOW_LEVELS * BLK, :]
            m_cum = mlow_ref[LOW_LEVELS * BLK:(LOW_LEVELS + 1) * BLK, :]
            eg = jnp.concatenate([jnp.where((rows & 1) == 1, lf, 0.0), e1, _dot(m_low, lf_hi),
                                  _dot(m_cum, lf_hi) + _dot(m_cum, lf_lo)], axis=0)
            s["heads"] = []
            for h in range(HG_HEADS):
                hs = slice(h * LANES, (h + 1) * LANES)
                s["heads"].append(dict(
                    q=zc(C_HQ + h * LANES, LANES), k=kin[:, hs],
                    e_low=[eg[l * BLK:(l + 1) * BLK, hs] for l in range(LOW_LEVELS)],
                    g=eg[LOW_LEVELS * BLK:(LOW_LEVELS + 1) * BLK, hs], u=[], p=[]))

        def h_operand(h, l):
            def run():
                hd = s["heads"][h]
                hd["u"].append(_level_operand(l, hd["q"], hd["k"], hd["e_low"], hd["g"]).astype(BF16))
            return run

        def h_products(h):
            def run():
                hd = s["heads"][h]
                hd["p"] = [_dot_nt(u, u) for u in hd.pop("u")]
            return run

        def h_combine(h):
            def run():
                hd = s["heads"][h]
                a = jnp.zeros((BLK, BLK), F32)
                for l, p in enumerate(hd.pop("p")):
                    a = jnp.where(lvl == l, p, a)
                diag = jnp.sum(hd["q"] * hd["k"], axis=-1, keepdims=True)
                hd["a"] = jnp.where(lvl == -1, diag, a).astype(BF16)
                g_last = hd["g"][BLK - 1:BLK, :]
                hd["qe"] = (hd["q"] * jnp.exp2(hd["g"])).astype(BF16)
                hd["kd"] = (hd["k"] * jnp.exp2(g_last - hd["g"])).astype(BF16)
                hd["dec"] = jnp.exp2(g_last)
            return run

        def h_output(h):
            def run():
                hd = s["heads"][h]
                v_h = zc(C_HI + h * LANES, LANES).astype(BF16)
                st = st_scr[q, h]
                hd["o"] = _dot(hd.pop("a"), v_h) + _dot_nt(hd.pop("qe"), st.astype(BF16))
                st_scr[q, h] = st * hd.pop("dec") + _dot_tn(v_h, hd.pop("kd"))
            return run

        def h_finish(h):
            def run():
                o_h = s["heads"][h].pop("o")
                gt = zc(C_GH + h * LANES, LANES)
                inv = lax.rsqrt(jnp.mean(o_h * o_h, axis=-1, keepdims=True) + NORM_EPS)
                mix_scr[q, slot, :, ATTN_WIDTH + h * LANES:ATTN_WIDTH + (h + 1) * LANES] = (
                    o_h * inv * hg_g[:, h * LANES:(h + 1) * LANES] * _silu(gt)).astype(BF16)
            return run

        heads, variants = range(HG_HEADS), range(4)
        return ([a_rope, h_prep]
                + _merge([a_scores(i) for i in variants],
                         [h_operand(h, l) for h in heads for l in range(PROMPT_LEVELS)])
                + _merge([a_softmax(i) for i in variants], [h_products(h) for h in heads])
                + _merge([a_values(i) for i in variants], [h_combine(h) for h in heads])
                + _merge([a_finish], [h_output(h) for h in heads])
                + [h_finish(h) for h in heads])

    def stage_mix(j):
        units = mix_units(0, j)
        for q in range(1, PROMPT_SEQS):
            units = _merge(units, mix_units(q, j))
        return units

    @pl.when(jnp.logical_and(pl.program_id(0) == 0, t == 0))
    def _():
        for q in range(PROMPT_SEQS):
            for th in stage_in(q, 0):
                th()

    def both(stage, j):
        lists = [stage(q, j) for q in range(PROMPT_SEQS)]
        return [th for group in zip(*lists) for th in group]

    for j in range(nblk):
        nxt = both(stage_in, j // IN_GROUP + 1)
        part = j % IN_GROUP
        per = -(-len(nxt) // IN_GROUP)
        side = nxt[part * per:(part + 1) * per]
        if j >= 1:
            side = side + both(stage_out, j - 1)
        _interleave(stage_mix(j), side)
    for th in both(stage_out, nblk - 1):
        th()

    @pl.when(t == pl.num_programs(1) - 1)
    def _():
        for q in range(PROMPT_SEQS):
            for h in range(HG_HEADS):
                sfin_ref[q, h] = st_scr[q, h].T


def _sample_kernel(sinks_ref, x_ref, p_ref, ck_ref, cv_ref, s0_ref, cq_ref, sq_ref, ckt_ref, skt_ref,
                   lvl_ref, mlow_ref, seg_ref, *refs):
    w_in, w_tail, w_out, w_pg, w_pp, rest = _split_refs(refs)
    v512_ref, v1024_ref, y_ref, nk_ref, nv_ref, ns_ref, y_scr, qm_scr, attn_scr, mix_scr = rest
    layer = pl.program_id(0)
    blk = pl.program_id(1)
    tk = DEC_TOKENS
    rows = SAMPLE_SEQS * tk
    yrows = pl.ds(pl.multiple_of(blk * rows, rows), rows)

    @pl.when(layer == 0)
    def _():
        y_scr[yrows, :] = x_ref[...].reshape(rows, D_MODEL)

    x = y_scr[yrows, :]
    xb = x.astype(BF16)
    z_all = jnp.concatenate([_chunked_dot(xb, w_in), _dot(xb, w_tail[...])], axis=1)

    def z(c0, width):
        return z_all[:, c0:c0 + width]
    lvl = lvl_ref[...]
    attn_g = v512_ref[0:1, :]
    log_lb = v512_ref[1:2, :]
    log1m_lb = v512_ref[2:3, :]
    hg_g = v512_ref[3:4, :]

    qcols = [_rope(z(C_Q + c * LANES, LANES), cq_ref[...], sq_ref[...]) for c in range(4)]
    k_rot = _rope(z(C_K, KV_WIDTH), ckt_ref[...], skt_ref[...])
    v_new = z(C_V, KV_WIDTH)
    k_new_b = k_rot.astype(BF16)
    v_new_b = v_new.astype(BF16)
    lo2 = _lane_iota(2 * rows) < HEAD_DIM
    for g in range(2):
        xg = jnp.concatenate([qcols[2 * g], qcols[2 * g + 1]], axis=0)
        xsw = pltpu.roll(xg, HEAD_DIM, 1)
        keep = lo2 if g == 0 else jnp.logical_not(lo2)
        for e in range(2):
            qm = jnp.where(keep, xg if e == g else xsw, 0.0)
            for cc in range(2):
                r0 = ((g * 2 + e) * 2 + cc) * tk
                for s in range(SAMPLE_SEQS):
                    qm_scr[s, r0:r0 + tk, :] = qm[cc * rows + s * tk:cc * rows + (s + 1) * tk, :]

    lf = _log_decay(z(C_HF, HG_KW), log_lb, log1m_lb) * LOG2E
    kin = 1.0 - jnp.exp2(lf)
    lf_hi, lf_lo = _split_bf16(lf)
    eg = _dot(mlow_ref[...], lf_hi) + _dot(mlow_ref[...], lf_lo)
    dec_t = jnp.exp2(_dot_tn(lf_hi, seg_ref[...]) + _dot_tn(lf_lo, seg_ref[...]))
    o_intra, qe, kd, vh = [], [], [], []
    for h in range(HG_HEADS):
        hs = slice(h * LANES, (h + 1) * LANES)
        q_h = z(C_HQ + h * LANES, LANES)
        k_h = kin[:, hs]
        v_h = z(C_HI + h * LANES, LANES)
        e_low = [eg[l * rows:(l + 1) * rows, hs] for l in range(LOW_LEVELS)]
        g_h = eg[LOW_LEVELS * rows:(LOW_LEVELS + 1) * rows, hs]
        tot_h = eg[(LOW_LEVELS + 1) * rows:(LOW_LEVELS + 2) * rows, hs]
        a = _hgrn_scores(q_h, k_h, e_low, g_h, lvl, LOW_LEVELS)
        o_intra.append(_dot(a.astype(BF16), v_h.astype(BF16)))
        qe.append(q_h * jnp.exp2(g_h))
        kd.append(k_h * jnp.exp2(tot_h - g_h))
        vh.append(v_h)

    grp = SAMPLE_GROUP
    grows = grp * QM_ROWS
    rid = lax.broadcasted_iota(jnp.int32, (grows, 1), 0)
    r8 = (rid >> TOK_BITS) & (ATTN_HEADS - 1)
    sink_col = jnp.zeros((grows, 1), F32)
    for idx in range(ATTN_HEADS):
        g, e, cc = idx // 4, (idx // 2) % 2, idx % 2
        sink_col = jnp.where(r8 == idx, sinks_ref[layer * ATTN_HEADS + 4 * g + 2 * cc + e], sink_col)
    row = lax.broadcasted_iota(jnp.int32, (grows, 2 * WINDOW), 0)
    col = lax.broadcasted_iota(jnp.int32, (grows, 2 * WINDOW), 1)
    tok = row & (tk - 1)
    new_col = col - WINDOW
    mask_cache = jnp.logical_and(col < WINDOW, col > tok)
    lo8 = _lane_iota(tk) < HEAD_DIM
    o_inter = [[None] * SAMPLE_SEQS for _ in range(HG_HEADS)]
    for gi in range(SAMPLE_SEQS // grp):
        seqs = range(gi * grp, (gi + 1) * grp)
        kcs = {s: ck_ref[s] for s in seqs}
        vcs = {s: cv_ref[s] for s in seqs}
        sc = [_dot_nt(qm_scr[s].astype(BF16),
                      jnp.concatenate([kcs[s].astype(BF16), k_new_b], axis=0)) for s in seqs]
        for s in seqs:
            nk_ref[s, 0:WINDOW - tk, :] = kcs[s][tk:WINDOW, :]
            nk_ref[s, WINDOW - tk:WINDOW, :] = k_rot[s * tk:(s + 1) * tk, :]
            nv_ref[s, 0:WINDOW - tk, :] = vcs[s][tk:WINDOW, :]
            nv_ref[s, WINDOW - tk:WINDOW, :] = v_new[s * tk:(s + 1) * tk, :]
        same_seq = (new_col >> TOK_BITS) == (row >> QM_BITS) + gi * grp
        mask_new = jnp.logical_and(jnp.logical_and(col >= WINDOW, same_seq), (new_col & (tk - 1)) <= tok)
        s_all = jnp.where(jnp.logical_or(mask_cache, mask_new), jnp.concatenate(sc, axis=0), NEG_INF)
        m = jnp.maximum(jnp.max(s_all, axis=-1, keepdims=True), sink_col)
        p = jnp.exp(s_all - m)
        den = jnp.sum(p, axis=-1, keepdims=True) + jnp.exp(sink_col - m)
        pb = p.astype(BF16)
        o_all = jnp.concatenate(
            [_dot(pb[i * QM_ROWS:(i + 1) * QM_ROWS, :],
                  jnp.concatenate([vcs[s].astype(BF16), v_new_b], axis=0))
             for i, s in enumerate(seqs)], axis=0) / den
        o_sw = pltpu.roll(o_all, HEAD_DIM, 1)
        for i, s in enumerate(seqs):
            for c in range(4):
                g, cc = c // 2, c % 2
                ra = i * QM_ROWS + ((g * 2 + 0) * 2 + cc) * tk
                rb = i * QM_ROWS + ((g * 2 + 1) * 2 + cc) * tk
                part0 = (o_all if g == 0 else o_sw)[ra:ra + tk, :]
                part1 = (o_all if g == 1 else o_sw)[rb:rb + tk, :]
                attn_scr[c, s * tk:(s + 1) * tk, :] = jnp.where(lo8, part0, part1)
        s0s = {(s, h): s0_ref[s, h] for s in seqs for h in range(HG_HEADS)}
        for s in seqs:
            for h in range(HG_HEADS):
                o_inter[h][s] = _dot(qe[h][s * tk:(s + 1) * tk, :].astype(BF16), s0s[(s, h)].astype(BF16))
        upd = {(s, h): _dot_tn(kd[h][s * tk:(s + 1) * tk, :].astype(BF16), vh[h][s * tk:(s + 1) * tk, :].astype(BF16))
               for s in seqs for h in range(HG_HEADS)}
        for s in seqs:
            for h in range(HG_HEADS):
                dcol = dec_t[h * LANES:(h + 1) * LANES, s * tk:s * tk + 1]
                ns_ref[s, h] = dcol * s0s[(s, h)] + upd[(s, h)]

    acols = _gated_rmsnorm_cols([attn_scr[c] for c in range(4)], attn_g, z(C_GA, ATTN_WIDTH),
                                ATTN_WIDTH)
    for c in range(4):
        mix_scr[:, c * LANES:(c + 1) * LANES] = acols[c].astype(BF16)
    for h in range(HG_HEADS):
        hs = slice(h * LANES, (h + 1) * LANES)
        o_h = o_intra[h] + jnp.concatenate(o_inter[h], axis=0)
        gt = z(C_GH + h * LANES, LANES)
        inv = lax.rsqrt(jnp.mean(o_h * o_h, axis=-1, keepdims=True) + NORM_EPS)
        mix_scr[:, ATTN_WIDTH + h * LANES:ATTN_WIDTH + (h + 1) * LANES] = (
            o_h * inv * hg_g[:, hs] * _silu(gt)).astype(BF16)

    hpre = DN_ALPHA * x + _chunked_dot(mix_scr[...], w_out)
    hn = _layernorm(hpre, v1024_ref[0:1, :], v1024_ref[1:2, :])
    gate = _sigmoid(_chunked_dot(hn.astype(BF16), w_pg))
    pp = _chunked_dot(p_ref[...].reshape(rows, PLE_DIM).astype(BF16), w_pp)
    y = hn + gate * pp
    y_scr[yrows, :] = y
    y_ref[...] = y.reshape(SAMPLE_SEQS, tk, D_MODEL)


def _level_matrix():
    t = np.arange(BLK)[:, None]
    s = np.arange(BLK)[None, :]
    x = t ^ s
    lv = np.floor(np.log2(np.maximum(x, 1))).astype(np.int32)
    return np.where(t > s, lv, np.where(t == s, -1, -2)).astype(np.int32)


def _level_exponent_matrix(l):
    m = np.zeros((BLK, BLK), np.float32)
    b, h = 2 << l, 1 << l
    for t in range(BLK):
        mid = t - t % b + h
        if t >= mid:
            m[t, mid:t + 1] = 1.0
        else:
            m[t, t + 1:mid] = 1.0
    return m


def _mask_matrices(seq_rows):
    blocks = [_level_exponent_matrix(l) for l in range(LOW_LEVELS)]
    t = np.arange(BLK)[:, None]
    s = np.arange(BLK)[None, :]
    same = (t // seq_rows) == (s // seq_rows)
    blocks.append((same & (s <= t)).astype(np.float32))
    if seq_rows < BLK:
        blocks.append(same.astype(np.float32))
    return np.concatenate(blocks, axis=0)


def _rope_tables(pos, scale):
    half = HEAD_DIM // 2
    inv = jnp.exp(-math.log(ROPE_THETA) * jnp.arange(half, dtype=F32) * 2.0 / HEAD_DIM)
    ang = pos.astype(F32)[:, None] * inv[None, :]
    cos = jnp.cos(ang) * scale
    sin = jnp.sin(ang) * scale
    return jnp.tile(cos, (1, 4)), jnp.concatenate([-sin, sin, -sin, sin], axis=1)


def _weight_specs(layer_of):
    def window(rows, width, c):
        return pl.BlockSpec((None, rows, width), lambda *g, c=c: (layer_of(*g), 0, c),
                            pipeline_mode=pl.Buffered(1))
    specs = [window(D_MODEL, CHUNK_COLS, c) for c in range(IN_CHUNKS)]
    specs.append(window(D_MODEL, IN_TAIL, IN_CHUNKS * CHUNK_COLS // IN_TAIL))
    for rows in (MIX_WIDTH, D_MODEL, PLE_DIM):
        specs += [window(rows, CHUNK_COLS, c) for c in range(OUT_CHUNKS)]
    return specs


def _weight_args(weights):
    w_in, w_out, w_pg, w_pp = weights
    return [w_in] * (IN_CHUNKS + 1) + [w_out] * OUT_CHUNKS + [w_pg] * OUT_CHUNKS + [w_pp] * OUT_CHUNKS


def _const_spec(shape, layer=None):
    if layer is None:
        return pl.BlockSpec(shape, lambda *_: (0,) * len(shape))
    return pl.BlockSpec((None,) + shape, lambda *_: (layer,) + (0,) * len(shape))


def _prompt_layer(i, x, p_all, tables, lvl, mlow, weights, sinks, v512, v1024):
    B, T, _ = x.shape
    nt = T // PROMPT_TILE
    nb = B // PROMPT_SEQS
    tab_spec = pl.BlockSpec((PROMPT_TILE, LANES), lambda b, t: (t, 0))
    nblk = PROMPT_TILE // BLK
    assert nblk % IN_GROUP == 0 and nblk % 2 == 0

    def next_first_group(b, t):
        flat = jnp.minimum(b * nt + t + 1, nb * nt - 1)
        return (flat // nt, (flat % nt) * (nblk // IN_GROUP), 0)
    in_specs = [
        pl.BlockSpec(memory_space=pltpu.SMEM),
        pl.BlockSpec((PROMPT_SEQS, PROMPT_TILE, D_MODEL), lambda b, t: (b, t, 0)),
        pl.BlockSpec((PROMPT_SEQS, IN_GROUP * BLK, D_MODEL), next_first_group),
        pl.BlockSpec((None, PROMPT_SEQS, PROMPT_TILE, PLE_DIM), lambda b, t: (i, b, t, 0)),
        tab_spec, tab_spec, tab_spec, tab_spec,
        _const_spec((BLK, BLK)),
        _const_spec(mlow.shape),
    ] + _weight_specs(lambda b, t: i) + [
        _const_spec((4, ATTN_WIDTH), i),
        _const_spec((2, D_MODEL), i),
    ]
    out_shape = [
        jax.ShapeDtypeStruct((B, T, D_MODEL), F32),
        jax.ShapeDtypeStruct((B, WINDOW, KV_WIDTH), F32),
        jax.ShapeDtypeStruct((B, WINDOW, KV_WIDTH), F32),
        jax.ShapeDtypeStruct((B, HG_HEADS, HG_DK, HG_DV), F32),
    ]
    out_specs = [
        pl.BlockSpec((PROMPT_SEQS, PROMPT_TILE, D_MODEL), lambda b, t: (b, t, 0)),
        pl.BlockSpec((PROMPT_SEQS, WINDOW, KV_WIDTH), lambda b, t: (b, 0, 0)),
        pl.BlockSpec((PROMPT_SEQS, WINDOW, KV_WIDTH), lambda b, t: (b, 0, 0)),
        pl.BlockSpec((PROMPT_SEQS, HG_HEADS, HG_DK, HG_DV), lambda b, t: (b, 0, 0, 0)),
    ]
    scratch = [
        pltpu.VMEM((PROMPT_SEQS, HG_HEADS, HG_DV, HG_DK), F32),
        pltpu.VMEM((PROMPT_SEQS, 4, BLK, LANES), BF16),
        pltpu.VMEM((PROMPT_SEQS, 4, BLK, LANES), BF16),
        pltpu.VMEM((PROMPT_SEQS, nblk, BLK, IN_COLS), F32),
        pltpu.VMEM((PROMPT_SEQS, 2, BLK, MIX_WIDTH), BF16),
    ]
    return pl.pallas_call(
        _prompt_kernel,
        grid=(nb, nt),
        in_specs=in_specs,
        out_specs=out_specs,
        out_shape=out_shape,
        scratch_shapes=scratch,
        compiler_params=pltpu.CompilerParams(
            dimension_semantics=("arbitrary", "arbitrary"), vmem_limit_bytes=VMEM_LIMIT),
        name=f"prompt_layer{i}",
    )(sinks, x, x, p_all, *tables, lvl, mlow, *_weight_args(weights), v512, v1024)


def _sample_call(x, p_all, ck_all, cv_all, s0_all, tables, lvl, mlow, seg, weights, sinks, v512, v1024):
    B, T, _ = x.shape
    rows = SAMPLE_SEQS * T
    tab_spec = _const_spec((rows, LANES))
    in_specs = [
        pl.BlockSpec(memory_space=pltpu.SMEM),
        pl.BlockSpec((SAMPLE_SEQS, T, D_MODEL), lambda l, b: (jnp.where(l == 0, b, 0), 0, 0)),
        pl.BlockSpec((None, SAMPLE_SEQS, T, PLE_DIM), lambda l, b: (l, b, 0, 0)),
        pl.BlockSpec((None, SAMPLE_SEQS, WINDOW, KV_WIDTH), lambda l, b: (l, b, 0, 0)),
        pl.BlockSpec((None, SAMPLE_SEQS, WINDOW, KV_WIDTH), lambda l, b: (l, b, 0, 0)),
        pl.BlockSpec((None, SAMPLE_SEQS, HG_HEADS, HG_DK, HG_DV), lambda l, b: (l, b, 0, 0, 0)),
        tab_spec, tab_spec, tab_spec, tab_spec,
        _const_spec((BLK, BLK)),
        _const_spec(mlow.shape),
        _const_spec((BLK, LANES)),
    ] + _weight_specs(lambda l, b: l) + [
        pl.BlockSpec((None, 4, ATTN_WIDTH), lambda l, b: (l, 0, 0)),
        pl.BlockSpec((None, 2, D_MODEL), lambda l, b: (l, 0, 0)),
    ]
    out_shape = [
        jax.ShapeDtypeStruct((B, T, D_MODEL), F32),
        jax.ShapeDtypeStruct((DEPTH, B, WINDOW, KV_WIDTH), F32),
        jax.ShapeDtypeStruct((DEPTH, B, WINDOW, KV_WIDTH), F32),
        jax.ShapeDtypeStruct((DEPTH, B, HG_HEADS, HG_DK, HG_DV), F32),
    ]
    out_specs = [
        pl.BlockSpec((SAMPLE_SEQS, T, D_MODEL), lambda l, b: (jnp.where(l == DEPTH - 1, b, 0), 0, 0)),
        pl.BlockSpec((None, SAMPLE_SEQS, WINDOW, KV_WIDTH), lambda l, b: (l, b, 0, 0)),
        pl.BlockSpec((None, SAMPLE_SEQS, WINDOW, KV_WIDTH), lambda l, b: (l, b, 0, 0)),
        pl.BlockSpec((None, SAMPLE_SEQS, HG_HEADS, HG_DK, HG_DV), lambda l, b: (l, b, 0, 0, 0)),
    ]
    scratch = [
        pltpu.VMEM((B * T, D_MODEL), F32),
        pltpu.VMEM((SAMPLE_SEQS, QM_ROWS, LANES), F32),
        pltpu.VMEM((4, rows, LANES), F32),
        pltpu.VMEM((rows, MIX_WIDTH), BF16),
    ]
    return pl.pallas_call(
        _sample_kernel,
        grid=(DEPTH, B // SAMPLE_SEQS),
        in_specs=in_specs,
        out_specs=out_specs,
        out_shape=out_shape,
        scratch_shapes=scratch,
        compiler_params=pltpu.CompilerParams(
            dimension_semantics=("arbitrary", "arbitrary"), vmem_limit_bytes=VMEM_LIMIT),
        name="sample_layers",
    )(sinks, x, p_all, ck_all, cv_all, s0_all, *tables, lvl, mlow, seg, *_weight_args(weights), v512, v1024)


def kernel(x_prompt, x_sample, cache_k_win, cache_v_win, state_hgrn, p_prompt, p_sample, w_in, attn_sinks,
           attn_norm_g, hg_lb_logits, hg_norm_g, w_out, ln_g, ln_b, w_ple_proj, w_ple_gate):
    B, T, _ = x_prompt.shape
    SB, ST, _ = x_sample.shape
    assert T % PROMPT_TILE == 0 and B % PROMPT_SEQS == 0
    assert SB % SAMPLE_SEQS == 0 and ST == DEC_TOKENS and SAMPLE_SEQS * ST == BLK
    assert 1 << TOK_BITS == DEC_TOKENS and 1 << QM_BITS == QM_ROWS
    assert cache_k_win.shape[2] == WINDOW

    cs = jnp.cumsum(jax.nn.softmax(hg_lb_logits.astype(F32), axis=0), axis=0)
    lbs = cs - cs[:1]
    v512 = jnp.stack([attn_norm_g.astype(F32), jnp.log(lbs), jnp.log1p(-lbs), hg_norm_g.astype(F32)], axis=1)
    v1024 = jnp.stack([ln_g.astype(F32), ln_b.astype(F32)], axis=1)
    weights = tuple(w.astype(BF16) for w in (w_in, w_out, w_ple_gate, w_ple_proj))
    sinks = attn_sinks.astype(F32)

    scale = HEAD_DIM ** -0.5
    pos_p = jnp.arange(T, dtype=jnp.int32)
    pos_s = jnp.tile(PAST_LEN + jnp.arange(ST, dtype=jnp.int32), SAMPLE_SEQS)
    tab_p = _rope_tables(pos_p, scale) + _rope_tables(pos_p, 1.0)
    tab_s = _rope_tables(pos_s, scale) + _rope_tables(pos_s, 1.0)
    lvl = jnp.asarray(_level_matrix())
    mlow_p = jnp.asarray(_mask_matrices(BLK), dtype=BF16)
    mlow_s = jnp.asarray(_mask_matrices(ST), dtype=BF16)
    seg = jnp.asarray((np.arange(BLK)[:, None] // ST) == (np.arange(LANES)[None, :] // ST), dtype=BF16)

    ck = cache_k_win.reshape(DEPTH, SB, WINDOW, KV_WIDTH)
    cv = cache_v_win.reshape(DEPTH, SB, WINDOW, KV_WIDTH)
    ys, ks, vs, ss = _sample_call(x_sample, p_sample, ck, cv, state_hgrn, tab_s, lvl, mlow_s, seg, weights,
                                  sinks.reshape(DEPTH * ATTN_HEADS), v512, v1024)

    yp = x_prompt
    kp_l, vp_l, sp_l = [], [], []
    for i in range(DEPTH):
        yp, kp, vp, sp = _prompt_layer(i, yp, p_prompt, tab_p, lvl, mlow_p, weights, sinks[i], v512, v1024)
        kp_l.append(kp); vp_l.append(vp); sp_l.append(sp)

    def kv5(a, b):
        return a.reshape(DEPTH, b, WINDOW, KV_HEADS, HEAD_DIM)

    return (yp, ys, kv5(jnp.stack(kp_l), B), kv5(jnp.stack(vp_l), B), jnp.stack(sp_l),
            kv5(ks, SB), kv5(vs, SB), ss)
```
